```python
import math
import jax
import jax.numpy as jnp
from jax import lax
import numpy as np

D_MODEL = 1024
BATCH = 8
SEQ = 4096
DEPTH = 4

HEAD_DIM = 64
N_MIX_HEADS = D_MODEL // HEAD_DIM
N_GROUPS = 4
HEADS_PER_GROUP = N_MIX_HEADS // N_GROUPS
GROUP_WIDTH = HEADS_PER_GROUP * HEAD_DIM
Q_BLOCK = 128
NEG = -1e30

REL_BUCKETS = 32
REL_MAX_DIST = 128

IDX_HEADS = 8
IDX_DIM = 64
DSA_TOPK = 256

CMP_LEN = 32
CMP_STRIDE = 16
SLC_LEN = 64
SLC_TOPN = 16
WIN = 512

MOBA_BLOCK = 256
MOBA_TOPK = 3
MOBA_Q_BLOCK = 32

DIFF_HALF = HEAD_DIM // 2

MEM_LEN = 256
CROSS_HEADS = 4
CROSS_DIM = D_MODEL // CROSS_HEADS
D_FF = 4 * D_MODEL

DEEPNORM_ALPHA = (2 * DEPTH) ** 0.25
DEEPNORM_BETA = (8 * DEPTH) ** -0.25

G = GROUP_WIDTH
SPLIT_SIZES = (
    G, G, G, IDX_HEADS * IDX_DIM, IDX_DIM, IDX_HEADS,
    G, HEAD_DIM, HEAD_DIM, HEAD_DIM, HEAD_DIM, HEAD_DIM, HEAD_DIM,
    3 * HEADS_PER_GROUP,
    G, G, G,
    G, G, G,
)
IN_COLS = sum(SPLIT_SIZES)

kernel_name = 'hybrid_dsa_nsa_moba_diff_trunk'


def split_cols(h):
    offs = np.cumsum(SPLIT_SIZES)[:-1].tolist()
    return jnp.split(h, offs, axis=-1)


def heads(t, n):
    return t.reshape(t.shape[0], t.shape[1], n, -1)


def gather_rows(src, idx):
    return jax.vmap(lambda s, i: s[i])(src, idx)


def layer_norm(x, g, b, eps=1e-5):
    xf = x.astype(jnp.float32)
    mu = xf.mean(-1, keepdims=True)
    var = jnp.square(xf - mu).mean(-1, keepdims=True)
    return ((xf - mu) * lax.rsqrt(var + eps) * g + b).astype(x.dtype)


def rms_norm(x, g, eps=1e-6):
    xf = x.astype(jnp.float32)
    return (xf * lax.rsqrt(jnp.mean(xf * xf, -1, keepdims=True) + eps) * g).astype(x.dtype)


def masked_softmax(logits, mask):
    logits = jnp.where(mask, logits.astype(jnp.float32), NEG)
    return jax.nn.softmax(logits, axis=-1) * mask


def rel_bucket(dist):
    n = jnp.maximum(dist, 0)
    max_exact = REL_BUCKETS // 2
    nf = jnp.maximum(n, max_exact).astype(jnp.float32)
    large = max_exact + (jnp.log(nf / max_exact) / math.log(REL_MAX_DIST / max_exact)
                         * (REL_BUCKETS - max_exact)).astype(jnp.int32)
    large = jnp.minimum(large, REL_BUCKETS - 1)
    return jnp.where(n < max_exact, n, large)


def sweep(fn, L, blk):
    starts = jnp.arange(L // blk, dtype=jnp.int32) * blk
    out = lax.map(fn, starts)
    nb, B, _, H, d = out.shape
    return out.transpose(1, 0, 2, 3, 4).reshape(B, L, H * d)


def dsa_attention(q, k, v, q_idx, k_idx, w_idx, rel_tab):
    B, L, H, Dh = q.shape
    topk = min(DSA_TOPK, L // 4)
    key_pos = jnp.arange(L, dtype=jnp.int32)

    def block(q0):
        qb = lax.dynamic_slice_in_dim(q, q0, Q_BLOCK, axis=1)
        qib = lax.dynamic_slice_in_dim(q_idx, q0, Q_BLOCK, axis=1)
        wib = lax.dynamic_slice_in_dim(w_idx, q0, Q_BLOCK, axis=1)
        qpos = q0 + jnp.arange(Q_BLOCK, dtype=jnp.int32)
        rel = jax.nn.relu(jnp.einsum('bqhd,bsd->bhqs', qib, k_idx) * IDX_DIM ** -0.5)
        score = jnp.einsum('bqh,bhqs->bqs', wib, rel).astype(jnp.float32) * IDX_HEADS ** -0.5
        score = jnp.where(key_pos[None, None, :] <= qpos[None, :, None], score, -jnp.inf)
        _, sel = lax.top_k(score, topk)
        kg = gather_rows(k, sel)
        vg = gather_rows(v, sel)
        dist = qpos[None, :, None] - sel
        bias = rel_tab[rel_bucket(dist)].transpose(0, 3, 1, 2)
        logits = jnp.einsum('bqhd,bqkhd->bhqk', qb, kg) * Dh ** -0.5 + bias
        p = masked_softmax(logits, (dist >= 0)[:, None])
        return jnp.einsum('bhqk,bqkhd->bqhd', p.astype(v.dtype), vg)

    return sweep(block, L, Q_BLOCK)


def nsa_attention(q, k_cmp_raw, v_cmp_raw, k_slc, v_slc, k_win, v_win, gate_logits,
                  pos_k, pos_v, w1_k, w2_k, w1_v, w2_v, rel_tab):
    B, L, H, Dh = q.shape
    n_cmp = (L - CMP_LEN) // CMP_STRIDE + 1
    cmp_start = np.arange(n_cmp) * CMP_STRIDE
    cmp_idx = cmp_start[:, None] + np.arange(CMP_LEN)[None, :]
    cmp_end = jnp.asarray(cmp_start + CMP_LEN - 1, jnp.int32)

    def compress(t, pos, w1, w2):
        blocks = (t[:, cmp_idx] + pos).reshape(B, n_cmp, CMP_LEN * Dh)
        return jax.nn.gelu(blocks @ w1) @ w2

    kc = compress(k_cmp_raw, pos_k, w1_k, w2_k)
    vc = compress(v_cmp_raw, pos_v, w1_v, w2_v)
    n_slc = L // SLC_LEN
    topn = min(SLC_TOPN, n_slc)
    slc_start = np.arange(n_slc) * SLC_LEN
    overlap = jnp.asarray(((cmp_start[:, None] <= slc_start[None, :] + SLC_LEN - 1)
                           & (cmp_start[:, None] + CMP_LEN - 1 >= slc_start[None, :])).astype(np.float32))
    slc_start_j = jnp.asarray(slc_start, jnp.int32)
    slc_ids = jnp.arange(n_slc, dtype=jnp.int32)
    slc_offs = jnp.arange(SLC_LEN, dtype=jnp.int32)
    k_win_pad = jnp.pad(k_win, ((0, 0), (WIN, 0), (0, 0)))
    v_win_pad = jnp.pad(v_win, ((0, 0), (WIN, 0), (0, 0)))
    win_offs = jnp.arange(WIN + Q_BLOCK, dtype=jnp.int32) - WIN
    gates = jax.nn.sigmoid(gate_logits.astype(jnp.float32)).astype(q.dtype).reshape(B, L, H, 3)
    scale = Dh ** -0.5

    def block(q0):
        qb = lax.dynamic_slice_in_dim(q, q0, Q_BLOCK, axis=1)
        gb = lax.dynamic_slice_in_dim(gates, q0, Q_BLOCK, axis=1)
        qpos = q0 + jnp.arange(Q_BLOCK, dtype=jnp.int32)
        lc = jnp.einsum('bqhd,bnd->bhqn', qb, kc) * scale
        pc = masked_softmax(lc, cmp_end[None, :] <= qpos[:, None])
        o_cmp = jnp.einsum('bhqn,bnd->bqhd', pc.astype(vc.dtype), vc)
        imp = jnp.einsum('bhqn,nj->bqj', pc, overlap)
        cur = qpos // SLC_LEN
        forced = ((slc_ids[None, :] == 0) | (slc_ids[None, :] == cur[:, None])
                  | (slc_ids[None, :] == cur[:, None] - 1))
        imp = jnp.where(forced, jnp.inf, imp)
        imp = jnp.where(slc_start_j[None, :] <= qpos[:, None], imp, -jnp.inf)
        _, sel = lax.top_k(imp, topn)
        tok = (sel[..., None] * SLC_LEN + slc_offs).reshape(B, Q_BLOCK, topn * SLC_LEN)
        ksg = gather_rows(k_slc, tok)
        vsg = gather_rows(v_slc, tok)
        dist_s = qpos[None, :, None] - tok
        ls = (jnp.einsum('bqhd,bqnd->bhqn', qb, ksg) * scale
              + rel_tab[rel_bucket(dist_s)].transpose(0, 3, 1, 2))
        ps = masked_softmax(ls, (dist_s >= 0)[:, None])
        o_slc = jnp.einsum('bhqn,bqnd->bqhd', ps.astype(vsg.dtype), vsg)
        kwb = lax.dynamic_slice_in_dim(k_win_pad, q0, WIN + Q_BLOCK, axis=1)
        vwb = lax.dynamic_slice_in_dim(v_win_pad, q0, WIN + Q_BLOCK, axis=1)
        kpos = q0 + win_offs
        dist_w = qpos[:, None] - kpos[None, :]
        mask_w = (dist_w >= 0) & (dist_w < WIN) & (kpos[None, :] >= 0)
        lw = (jnp.einsum('bqhd,bkd->bhqk', qb, kwb) * scale
              + rel_tab[rel_bucket(dist_w)].transpose(2, 0, 1)[None])
        pw = masked_softmax(lw, mask_w)
        o_win = jnp.einsum('bhqk,bkd->bqhd', pw.astype(vwb.dtype), vwb)
        return gb[..., 0:1] * o_cmp + gb[..., 1:2] * o_slc + gb[..., 2:3] * o_win

    return sweep(block, L, Q_BLOCK)


def moba_attention(q, k, v, rel_tab):
    B, L, H, Dh = q.shape
    nb = -(-L // MOBA_BLOCK)
    pad = nb * MOBA_BLOCK - L
    k_pad = jnp.pad(k, ((0, 0), (0, pad), (0, 0), (0, 0)))
    v_pad = jnp.pad(v, ((0, 0), (0, pad), (0, 0), (0, 0)))
    k_blk = k_pad.reshape(B, nb, MOBA_BLOCK, H, Dh)
    k_mean = k_blk.mean(axis=2)
    k_bh = k_blk.transpose(0, 3, 1, 2, 4)
    v_bh = v_pad.reshape(B, nb, MOBA_BLOCK, H, Dh).transpose(0, 3, 1, 2, 4)
    topk = min(MOBA_TOPK, nb - 1)
    offs = jnp.arange(MOBA_BLOCK, dtype=jnp.int32)
    blk_ids = jnp.arange(nb, dtype=jnp.int32)
    h_ids = jnp.arange(H)[None, :, None, None, None]
    tab_t = rel_tab.T
    scale = Dh ** -0.5
    gather2 = jax.vmap(jax.vmap(lambda s, i: s[i]))

    def block(q0):
        qb = lax.dynamic_slice_in_dim(q, q0, MOBA_Q_BLOCK, axis=1)
        qpos = q0 + jnp.arange(MOBA_Q_BLOCK, dtype=jnp.int32)
        cur = q0 // MOBA_BLOCK
        ko = lax.dynamic_slice_in_dim(k_pad, cur * MOBA_BLOCK, MOBA_BLOCK, axis=1)
        vo = lax.dynamic_slice_in_dim(v_pad, cur * MOBA_BLOCK, MOBA_BLOCK, axis=1)
        dist_o = qpos[:, None] - (cur * MOBA_BLOCK + offs)[None, :]
        lo = (jnp.einsum('bqhd,bkhd->bhqk', qb, ko) * scale
              + rel_tab[rel_bucket(dist_o)].transpose(2, 0, 1)[None])
        mask_o = jnp.broadcast_to(dist_o >= 0, lo.shape)
        if topk == 0:
            p = masked_softmax(lo, mask_o).astype(v.dtype)
            return jnp.einsum('bhqk,bkhd->bqhd', p, vo)
        gate = jnp.einsum('bqhd,bnhd->bhqn', qb, k_mean).astype(jnp.float32)
        gate = jnp.where(blk_ids < cur, gate, -jnp.inf)
        gval, sel = lax.top_k(gate, topk)
        kg = gather2(k_bh, sel)
        vg = gather2(v_bh, sel)
        dist_p = qpos[None, None, :, None, None] - (sel[..., None] * MOBA_BLOCK + offs)
        lp = (jnp.einsum('bqhd,bhqjkd->bhqjk', qb, kg) * scale
              + tab_t[h_ids, rel_bucket(dist_p)])
        mask_p = jnp.broadcast_to(jnp.isfinite(gval)[..., None], lp.shape)
        n_p = topk * MOBA_BLOCK
        logits = jnp.concatenate([lo, lp.reshape(B, H, MOBA_Q_BLOCK, n_p)], axis=-1)
        mask = jnp.concatenate([mask_o, mask_p.reshape(B, H, MOBA_Q_BLOCK, n_p)], axis=-1)
        p = masked_softmax(logits, mask).astype(v.dtype)
        po = p[..., :MOBA_BLOCK]
        pp = p[..., MOBA_BLOCK:].reshape(B, H, MOBA_Q_BLOCK, topk, MOBA_BLOCK)
        return (jnp.einsum('bhqk,bkhd->bqhd', po, vo)
                + jnp.einsum('bhqjk,bhqjkd->bqhd', pp, vg))

    return sweep(block, L, MOBA_Q_BLOCK)


def diff_attention(q, k, v, lam, lam_init, norm_g, rel_tab):
    B, L, H, Dh = v.shape
    q2 = q.reshape(B, L, H, 2, DIFF_HALF)
    k2 = k.reshape(B, L, H, 2, DIFF_HALF)
    key_pos = jnp.arange(L, dtype=jnp.int32)

    def block(q0):
        qb = lax.dynamic_slice_in_dim(q2, q0, Q_BLOCK, axis=1)
        qpos = q0 + jnp.arange(Q_BLOCK, dtype=jnp.int32)
        dist = qpos[:, None] - key_pos[None, :]
        bias = rel_tab[rel_bucket(dist)].transpose(2, 0, 1)
        logits = (jnp.einsum('bqhcd,bkhcd->bhcqk', qb, k2) * DIFF_HALF ** -0.5
                  + bias[None, :, None])
        p = masked_softmax(logits, dist >= 0)
        a = p[:, :, 0] - lam * p[:, :, 1]
        return jnp.einsum('bhqk,bkhd->bqhd', a.astype(v.dtype), v)

    o = sweep(block, L, Q_BLOCK).reshape(B, L, H, Dh)
    o = rms_norm(o, norm_g) * (1.0 - lam_init)
    return o.reshape(B, L, H * Dh)


def cross_attention(x, mem, wq, wk, wv, wo):
    B, L, _ = x.shape
    q = (x @ wq).reshape(B, L, CROSS_HEADS, CROSS_DIM)
    k = (mem @ wk).reshape(B, -1, CROSS_HEADS, CROSS_DIM)
    v = (mem @ wv).reshape(B, -1, CROSS_HEADS, CROSS_DIM)
    logits = jnp.einsum('bqhd,bmhd->bhqm', q, k).astype(jnp.float32) * CROSS_DIM ** -0.5
    p = jax.nn.softmax(logits, axis=-1).astype(v.dtype)
    o = jnp.einsum('bhqm,bmhd->bqhd', p, v).reshape(B, L, D_MODEL)
    return o @ wo


def squared_relu_mlp(x, w1, w2):
    return jnp.square(jax.nn.relu(x @ w1)) @ w2


def setup_inputs(seed: int = 0) -> dict:
    key = jax.random.key(seed)
    ks = jax.random.split(key, 32)

    def nrm(k, shape, scale):
        return jax.random.normal(k, shape, jnp.float32) * scale

    D = D_MODEL
    return {
        'x': nrm(ks[0], (BATCH, SEQ, D), 1.0),
        'mem': nrm(ks[1], (BATCH, MEM_LEN, D), 1.0),
        'rel_bias': nrm(ks[2], (REL_BUCKETS, N_MIX_HEADS), 0.2),
        'w_in': nrm(ks[3], (DEPTH, D, IN_COLS), D ** -0.5),
        'w_out': nrm(ks[4], (DEPTH, D, D), D ** -0.5 * DEEPNORM_BETA),
        'nsa_pos_k': nrm(ks[5], (DEPTH, CMP_LEN, HEAD_DIM), 0.1),
        'nsa_pos_v': nrm(ks[6], (DEPTH, CMP_LEN, HEAD_DIM), 0.1),
        'nsa_w1_k': nrm(ks[7], (DEPTH, CMP_LEN * HEAD_DIM, HEAD_DIM), (CMP_LEN * HEAD_DIM) ** -0.5),
        'nsa_w2_k': nrm(ks[8], (DEPTH, HEAD_DIM, HEAD_DIM), HEAD_DIM ** -0.5),
        'nsa_w1_v': nrm(ks[9], (DEPTH, CMP_LEN * HEAD_DIM, HEAD_DIM), (CMP_LEN * HEAD_DIM) ** -0.5),
        'nsa_w2_v': nrm(ks[10], (DEPTH, HEAD_DIM, HEAD_DIM), HEAD_DIM ** -0.5),
        'diff_lq1': nrm(ks[11], (DEPTH, DIFF_HALF), 0.1),
        'diff_lk1': nrm(ks[12], (DEPTH, DIFF_HALF), 0.1),
        'diff_lq2': nrm(ks[13], (DEPTH, DIFF_HALF), 0.1),
        'diff_lk2': nrm(ks[14], (DEPTH, DIFF_HALF), 0.1),
        'diff_g': 1.0 + nrm(ks[15], (DEPTH, HEAD_DIM), 0.01),
        'ln1_g': 1.0 + nrm(ks[16], (DEPTH, D), 0.01),
        'ln1_b': nrm(ks[17], (DEPTH, D), 0.01),
        'xq': nrm(ks[18], (DEPTH, D, D), D ** -0.5),
        'xk': nrm(ks[19], (DEPTH, D, D), D ** -0.5),
        'xv': nrm(ks[20], (DEPTH, D, D), D ** -0.5),
        'xo': nrm(ks[21], (DEPTH, D, D), D ** -0.5 * DEEPNORM_BETA),
        'ln2_g': 1.0 + nrm(ks[22], (DEPTH, D), 0.01),
        'ln2_b': nrm(ks[23], (DEPTH, D), 0.01),
        'mlp_w1': nrm(ks[24], (DEPTH, D, D_FF), D ** -0.5),
        'mlp_w2': nrm(ks[25], (DEPTH, D_FF, D), D_FF ** -0.5 * DEEPNORM_BETA),
        'ln3_g': 1.0 + nrm(ks[26], (DEPTH, D), 0.01),
        'ln3_b': nrm(ks[27], (DEPTH, D), 0.01),
    }


def reference(x, mem, rel_bias, w_in, w_out, nsa_pos_k, nsa_pos_v, nsa_w1_k, nsa_w2_k,
              nsa_w1_v, nsa_w2_v, diff_lq1, diff_lk1, diff_lq2, diff_lk2, diff_g,
              ln1_g, ln1_b, xq, xk, xv, xo, ln2_g, ln2_b, mlp_w1, mlp_w2, ln3_g, ln3_b):
    HG = HEADS_PER_GROUP
    tabs = [rel_bias[:, g * HG:(g + 1) * HG] for g in range(N_GROUPS)]
    for l in range(DEPTH):
        h = x @ w_in[l]
        (a_q, a_k, a_v, a_qi, a_ki, a_w,
         b_q, b_kc, b_vc, b_ks, b_vs, b_kw, b_vw, b_g,
         c_q, c_k, c_v, d_q, d_k, d_v) = split_cols(h)
        o_a = dsa_attention(heads(a_q, HG), heads(a_k, HG), heads(a_v, HG),
                            heads(a_qi, IDX_HEADS), a_ki, a_w, tabs[0])
        o_b = nsa_attention(heads(b_q, HG), b_kc, b_vc, b_ks, b_vs, b_kw, b_vw, b_g,
                            nsa_pos_k[l], nsa_pos_v[l], nsa_w1_k[l], nsa_w2_k[l],
                            nsa_w1_v[l], nsa_w2_v[l], tabs[1])
        o_c = moba_attention(heads(c_q, HG), heads(c_k, HG), heads(c_v, HG), tabs[2])
        lam_init = 0.8 - 0.6 * math.exp(-0.3 * l)
        lam = (jnp.exp(jnp.sum(diff_lq1[l].astype(jnp.float32) * diff_lk1[l]))
               - jnp.exp(jnp.sum(diff_lq2[l].astype(jnp.float32) * diff_lk2[l])) + lam_init)
        o_d = diff_attention(heads(d_q, HG), heads(d_k, HG), heads(d_v, HG), lam, lam_init,
                             diff_g[l], tabs[3])
        mix = jnp.concatenate([o_a, o_b, o_c, o_d], axis=-1).astype(x.dtype) @ w_out[l]
        x = layer_norm(DEEPNORM_ALPHA * x + mix, ln1_g[l], ln1_b[l])
        x = layer_norm(DEEPNORM_ALPHA * x + cross_attention(x, mem, xq[l], xk[l], xv[l], xo[l]),
                       ln2_g[l], ln2_b[l])
        x = layer_norm(DEEPNORM_ALPHA * x + squared_relu_mlp(x, mlp_w1[l], mlp_w2[l]),
                       ln3_g[l], ln3_b[l])
    return x
```

```python
import functools
import math

import jax
import jax.numpy as jnp
from jax import lax
from jax.experimental import pallas as pl
from jax.experimental.pallas import tpu as pltpu

f32 = jnp.float32
bf16 = jnp.bfloat16
i32 = jnp.int32

D_MODEL = 1024
DEPTH = 4
HEAD_DIM = 64
HG = 4
GW = HG * HEAD_DIM
REL_BUCKETS = 32
REL_MAX_DIST = 128
IDX_HEADS = 8
IDX_DIM = 64
DSA_TOPK = 256
CMP_LEN = 32
CMP_STRIDE = 16
SLC_LEN = 64
SLC_TOPN = 16
WIN = 512
MOBA_BLOCK = 256
MOBA_TOPK = 3
DIFF_HALF = HEAD_DIM // 2
CROSS_HEADS = 4
CROSS_DIM = D_MODEL // CROSS_HEADS
D_FF = 4 * D_MODEL
ALPHA = (2 * DEPTH) ** 0.25

NEG = -1e30
INT_MIN = -(2 ** 31)
LANES = 128
TQ = 256
TK = 256
VMEM_LIMIT = 56 * 1024 * 1024

R_AQ, R_AV, R_AQI, R_BQ, R_BVS, R_BVW, R_CQ, R_CV, R_DQ, R_DV, R_BKVC = (
    0, 256, 512, 1024, 1280, 1536, 1792, 2048, 2304, 2560, 2816)
NR = 2944
T_AK, T_CK, T_DK, T_BKS, T_BKW, T_AKI = 0, 256, 512, 768, 1024, 1280
NT = 1344
S_AW, S_BG = 0, 8


def _cparams(*sem):
    return pltpu.CompilerParams(dimension_semantics=sem, vmem_limit_bytes=VMEM_LIMIT)


def _rel_bucket(dist):
    n = jnp.maximum(dist, 0)
    max_exact = REL_BUCKETS // 2
    nf = jnp.maximum(n, max_exact).astype(f32)
    large = max_exact + (jnp.log(nf / max_exact) / math.log(REL_MAX_DIST / max_exact)
                         * (REL_BUCKETS - max_exact)).astype(i32)
    large = jnp.minimum(large, REL_BUCKETS - 1)
    return jnp.where(n < max_exact, n, large)


def _bias_tiles(rel_bias):
    d = jnp.arange(TQ, dtype=i32)[:, None] - jnp.arange(TK, dtype=i32)[None, :]
    tiles = jnp.stack([rel_bias[_rel_bucket(d + off)] for off in (0, TQ, 2 * TQ)])
    return tiles.transpose(3, 0, 1, 2).astype(f32)


def _lane_iota(shape):
    return lax.broadcasted_iota(i32, shape, len(shape) - 1)


def _head_lanes(h, width=HEAD_DIM, offset=0):
    lane = _lane_iota((1, GW))
    lo = h * HEAD_DIM + offset
    return (lane >= lo) & (lane < lo + width)


def _flash_init(m_ref, l_ref, acc_ref):
    m_ref[...] = jnp.full(m_ref.shape, NEG, f32)
    l_ref[...] = jnp.zeros(l_ref.shape, f32)
    acc_ref[...] = jnp.zeros(acc_ref.shape, f32)


def _flash_update(s, mask, v, m_ref, l_ref, acc_ref):
    s = jnp.where(mask, s, NEG)
    m_prev = m_ref[...]
    m_new = jnp.maximum(m_prev, jnp.max(s, axis=1, keepdims=True))
    alpha = jnp.exp(m_prev - m_new)
    p = jnp.where(mask, jnp.exp(s - m_new), 0.0)
    l_ref[...] = alpha * l_ref[...] + jnp.sum(p, axis=1, keepdims=True)
    acc_ref[...] = alpha * acc_ref[...] + jnp.dot(p.astype(bf16), v, preferred_element_type=f32)
    m_ref[...] = m_new


def _flash_result(l_ref, acc_ref):
    l = l_ref[...]
    return jnp.where(l > 0.0, acc_ref[...] / jnp.where(l > 0.0, l, 1.0), 0.0)


def _layer_norm(y, g, b):
    mu = jnp.mean(y, axis=-1, keepdims=True)
    yc = y - mu
    var = jnp.mean(yc * yc, axis=-1, keepdims=True)
    return yc * lax.rsqrt(var + 1e-5) * g + b


def _proj_kernel(x_ref, wr_ref, wt_ref, ws_ref, hr_ref, ht_ref, hs_ref):
    xb = x_ref[...].astype(bf16)
    hr_ref[...] = jnp.dot(xb, wr_ref[...], preferred_element_type=f32).astype(bf16)
    ht_ref[...] = lax.dot_general(wt_ref[...], xb, (((1,), (1,)), ((), ())),
                                  preferred_element_type=f32).astype(bf16)
    hs_ref[...] = jnp.dot(xb, ws_ref[...], preferred_element_type=f32)


def _project(x, wr, wt, ws):
    B, L, D = x.shape
    nch = L // TK
    return pl.pallas_call(
        _proj_kernel,
        grid=(B, nch),
        in_specs=[
            pl.BlockSpec((None, TK, D), lambda b, i: (b, i, 0)),
            pl.BlockSpec((D, NR), lambda b, i: (0, 0)),
            pl.BlockSpec((NT, D), lambda b, i: (0, 0)),
            pl.BlockSpec((D, LANES), lambda b, i: (0, 0)),
        ],
        out_specs=[
            pl.BlockSpec((None, TK, NR), lambda b, i: (b, i, 0)),
            pl.BlockSpec((None, None, NT, TK), lambda b, i: (b, i, 0, 0)),
            pl.BlockSpec((None, TK, LANES), lambda b, i: (b, i, 0)),
        ],
        out_shape=[
            jax.ShapeDtypeStruct((B, L, NR), bf16),
            jax.ShapeDtypeStruct((B, nch, NT, TK), bf16),
            jax.ShapeDtypeStruct((B, L, LANES), f32),
        ],
        compiler_params=_cparams("parallel", "parallel"),
        name="proj",
    )(x, wr, wt, ws)


def _dsa_kernel(q_ref, qi_ref, w_ref, kT_ref, kiT_ref, v_ref, bias_ref, o_ref,
                key_ref, m_ref, l_ref, acc_ref, *, topk):
    qi = pl.program_id(1)
    nch = qi + 1
    row = qi * TQ + lax.broadcasted_iota(i32, (TQ, 1), 0)
    col0 = _lane_iota((1, TK))
    wv = w_ref[...] * (IDX_DIM ** -0.5 * IDX_HEADS ** -0.5)
    low_half = _lane_iota((1, LANES)) < IDX_DIM
    qheads = []
    for h in range(IDX_HEADS):
        pair = qi_ref[:, (h // 2) * LANES:(h // 2 + 1) * LANES]
        keep = low_half if h % 2 == 0 else jnp.logical_not(low_half)
        qheads.append(jnp.where(keep, pair, jnp.zeros_like(pair)))

    def score_chunk(kc, carry):
        kiT = kiT_ref[kc]
        kiT2 = jnp.concatenate([kiT, kiT], axis=0)
        s = jnp.zeros((TQ, TK), f32)
        for h in range(IDX_HEADS):
            r = jnp.dot(qheads[h], kiT2, preferred_element_type=f32)
            s = s + wv[:, S_AW + h:S_AW + h + 1] * jnp.maximum(r, 0.0)
        s = jnp.where(s == 0.0, 0.0, s)
        bits = lax.bitcast_convert_type(s, i32)
        key = bits ^ ((bits >> 31) & 0x7FFFFFFF)
        key_ref[kc] = jnp.where(kc * TK + col0 <= row, key, INT_MIN)
        return carry

    lax.fori_loop(0, nch, score_chunk, 0)

    def count_ge(cand):
        def body(kc, acc):
            c = jnp.where(key_ref[kc] >= cand, 1.0, 0.0)
            return acc + c[:, :LANES] + c[:, LANES:]
        acc = lax.fori_loop(0, nch, body, jnp.zeros((TQ, LANES), f32))
        return jnp.sum(acc, axis=1, keepdims=True)

    def bisect(i, t_u):
        cand_u = t_u | jnp.left_shift(jnp.int32(1), 31 - i)
        cnt = count_ge(cand_u ^ INT_MIN)
        return jnp.where(cnt >= topk, cand_u, t_u)

    t_u = lax.fori_loop(0, 32, bisect, jnp.zeros((TQ, 1), i32))
    thr = jnp.maximum(t_u ^ INT_MIN, INT_MIN + 1)
    cnt_ge = count_ge(thr)

    @pl.when(jnp.max(cnt_ge) > topk)
    def _():
        cnt_gt = cnt_ge - lax.fori_loop(
            0, nch,
            lambda kc, a: a + jnp.sum(jnp.where(key_ref[kc] == thr, 1.0, 0.0), axis=1, keepdims=True),
            jnp.zeros((TQ, 1), f32))
        need = topk - cnt_gt
        tri = (lax.broadcasted_iota(i32, (TK, TK), 0) <= lax.broadcasted_iota(i32, (TK, TK), 1)).astype(bf16)

        def body(kc, seen):
            k = key_ref[kc]
            tie = k == thr
            tief = jnp.where(tie, 1.0, 0.0)
            pref = jnp.dot(tief.astype(bf16), tri, preferred_element_type=f32) + seen
            key_ref[kc] = jnp.where(tie & (pref > need), INT_MIN, k)
            return seen + jnp.sum(tief, axis=1, keepdims=True)

        lax.fori_loop(0, nch, body, jnp.zeros((TQ, 1), f32))

    q = q_ref[...]
    out = jnp.zeros((TQ, GW), f32)
    for h in range(HG):
        lanes = _head_lanes(h)
        qh = jnp.where(lanes, q, jnp.zeros_like(q))
        _flash_init(m_ref, l_ref, acc_ref)

        def body(kc, carry, qh=qh, h=h):
            s = jnp.dot(qh, kT_ref[kc], preferred_element_type=f32) * HEAD_DIM ** -0.5
            s = s + bias_ref[h, jnp.minimum(qi - kc, 2)]
            v = v_ref[pl.ds(pl.multiple_of(kc * TK, TK), TK), :]
            _flash_update(s, key_ref[kc] >= thr, v, m_ref, l_ref, acc_ref)
            return carry

        lax.fori_loop(0, nch, body, 0)
        out = jnp.where(lanes, _flash_result(l_ref, acc_ref), out)
    o_ref[...] = out.astype(o_ref.dtype)


def _dsa(hr, ht, hs, bias):
    B, L, _ = hr.shape
    nch = L // TK
    topk = min(DSA_TOPK, L // 4)
    return pl.pallas_call(
        functools.partial(_dsa_kernel, topk=topk),
        grid=(B, L // TQ),
        in_specs=[
            pl.BlockSpec((None, TQ, GW), lambda b, i: (b, i, R_AQ // GW)),
            pl.BlockSpec((None, TQ, 2 * GW), lambda b, i: (b, i, R_AQI // (2 * GW))),
            pl.BlockSpec((None, TQ, LANES), lambda b, i: (b, i, 0)),
            pl.BlockSpec((None, nch, GW, TK), lambda b, i: (b, 0, T_AK // GW, 0)),
            pl.BlockSpec((None, nch, IDX_DIM, TK), lambda b, i: (b, 0, T_AKI // IDX_DIM, 0)),
            pl.BlockSpec((None, L, GW), lambda b, i: (b, 0, R_AV // GW)),
            pl.BlockSpec((HG, 3, TQ, TK), lambda b, i: (0, 0, 0, 0)),
        ],
        out_specs=pl.BlockSpec((None, TQ, GW), lambda b, i: (b, i, 0)),
        out_shape=jax.ShapeDtypeStruct((B, L, GW), bf16),
        scratch_shapes=[
            pltpu.VMEM((nch, TQ, TK), i32),
            pltpu.VMEM((TQ, 1), f32),
            pltpu.VMEM((TQ, 1), f32),
            pltpu.VMEM((TQ, GW), f32),
        ],
        compiler_params=_cparams("parallel", "arbitrary"),
        name="dsa",
    )(hr, hr, hs, ht, ht, hr, bias)


def _moba_kernel(q_ref, kT_ref, v_ref, bias_ref, o_ref, km_ref, m_ref, l_ref, acc_ref, *, nch, topk):
    qi = pl.program_id(1)
    lane = _lane_iota((1, LANES))

    @pl.when(qi == 0)
    def _():
        km = jnp.zeros((GW, LANES), f32)
        for n in range(nch):
            mean_n = jnp.sum(kT_ref[n].astype(f32), axis=1, keepdims=True) * (1.0 / MOBA_BLOCK)
            km = jnp.where(lane == n, mean_n, km)
        km_ref[...] = km.astype(bf16)

    row = lax.broadcasted_iota(i32, (TQ, 1), 0)
    colk = _lane_iota((1, TK))
    lane_f = lane.astype(f32)
    q = q_ref[...]
    out = jnp.zeros((TQ, GW), f32)
    for h in range(HG):
        lanes = _head_lanes(h)
        qh = jnp.where(lanes, q, jnp.zeros_like(q))
        gate = jnp.dot(qh, km_ref[...], preferred_element_type=f32)
        gate = jnp.where(lane < qi, gate, -jnp.inf)
        sel = jnp.zeros((TQ, LANES), f32)
        for _ in range(topk):
            best = jnp.max(gate, axis=1, keepdims=True)
            first = jnp.min(jnp.where(gate == best, lane_f, float(LANES)), axis=1, keepdims=True)
            hit = lane_f == first
            sel = jnp.where(hit & (best > -jnp.inf), 1.0, sel)
            gate = jnp.where(hit, -jnp.inf, gate)
        _flash_init(m_ref, l_ref, acc_ref)

        def body(kc, carry, qh=qh, h=h, sel=sel):
            s = jnp.dot(qh, kT_ref[kc], preferred_element_type=f32) * HEAD_DIM ** -0.5
            s = s + bias_ref[h, jnp.minimum(qi - kc, 2)]
            picked = jnp.max(jnp.where(lane == kc, sel, 0.0), axis=1, keepdims=True)
            own = (kc == qi).astype(i32)
            limit = own * row + (1 - own) * ((picked * (TK + 1)).astype(i32) - 1)
            v = v_ref[pl.ds(pl.multiple_of(kc * TK, TK), TK), :]
            _flash_update(s, colk <= limit, v, m_ref, l_ref, acc_ref)
            return carry

        lax.fori_loop(0, qi + 1, body, 0)
        out = jnp.where(lanes, _flash_result(l_ref, acc_ref), out)
    o_ref[...] = out.astype(o_ref.dtype)


def _moba(hr, ht, bias):
    B, L, _ = hr.shape
    nch = L // TK
    topk = min(MOBA_TOPK, nch - 1)
    return pl.pallas_call(
        functools.partial(_moba_kernel, nch=nch, topk=topk),
        grid=(B, L // TQ),
        in_specs=[
            pl.BlockSpec((None, TQ, GW), lambda b, i: (b, i, R_CQ // GW)),
            pl.BlockSpec((None, nch, GW, TK), lambda b, i: (b, 0, T_CK // GW, 0)),
            pl.BlockSpec((None, L, GW), lambda b, i: (b, 0, R_CV // GW)),
            pl.BlockSpec((HG, 3, TQ, TK), lambda b, i: (0, 0, 0, 0)),
        ],
        out_specs=pl.BlockSpec((None, TQ, GW), lambda b, i: (b, i, 0)),
        out_shape=jax.ShapeDtypeStruct((B, L, GW), bf16),
        scratch_shapes=[
            pltpu.VMEM((GW, LANES), bf16),
            pltpu.VMEM((TQ, 1), f32),
            pltpu.VMEM((TQ, 1), f32),
            pltpu.VMEM((TQ, GW), f32),
        ],
        compiler_params=_cparams("parallel", "arbitrary"),
        name="moba",
    )(hr, ht, hr, bias)


def _diff_kernel(q_ref, kT_ref, v_ref, bias_ref, lq1_ref, lk1_ref, lq2_ref, lk2_ref, li_ref, g_ref,
                 o_ref, m_ref, l_ref, acc_ref):
    qi = pl.program_id(1)
    row = lax.broadcasted_iota(i32, (TQ, 1), 0)
    colk = _lane_iota((1, TK))
    lam_init = li_ref[...]
    lam = (jnp.exp(jnp.sum(lq1_ref[...] * lk1_ref[...], axis=1, keepdims=True))
           - jnp.exp(jnp.sum(lq2_ref[...] * lk2_ref[...], axis=1, keepdims=True)) + lam_init)
    q = q_ref[...]
    out = jnp.zeros((TQ, GW), f32)
    for h in range(HG):
        maps = []
        for c in range(2):
            qh = jnp.where(_head_lanes(h, DIFF_HALF, c * DIFF_HALF), q, jnp.zeros_like(q))
            _flash_init(m_ref, l_ref, acc_ref)

            def body(kc, carry, qh=qh, h=h):
                s = jnp.dot(qh, kT_ref[kc], preferred_element_type=f32) * DIFF_HALF ** -0.5
                s = s + bias_ref[h, jnp.minimum(qi - kc, 2)]
                limit = row + (kc != qi).astype(i32) * TK
                v = v_ref[pl.ds(pl.multiple_of(kc * TK, TK), TK), :]
                _flash_update(s, colk <= limit, v, m_ref, l_ref, acc_ref)
                return carry

            lax.fori_loop(0, qi + 1, body, 0)
            maps.append(_flash_result(l_ref, acc_ref))
        lanes = _head_lanes(h)
        o = jnp.where(lanes, maps[0] - lam * maps[1], 0.0)
        ms = jnp.sum(o * o, axis=1, keepdims=True) * (1.0 / HEAD_DIM)
        o = o * lax.rsqrt(ms + 1e-6) * g_ref[...] * (1.0 - lam_init)
        out = jnp.where(lanes, o, out)
    o_ref[...] = out.astype(o_ref.dtype)


def _diff(hr, ht, bias, lq1, lk1, lq2, lk2, lam_init, g_tiled):
    B, L, _ = hr.shape
    nch = L // TK
    vec = pl.BlockSpec((1, DIFF_HALF), lambda b, i: (0, 0))
    return pl.pallas_call(
        _diff_kernel,
        grid=(B, L // TQ),
        in_specs=[
            pl.BlockSpec((None, TQ, GW), lambda b, i: (b, i, R_DQ // GW)),
            pl.BlockSpec((None, nch, GW, TK), lambda b, i: (b, 0, T_DK // GW, 0)),
            pl.BlockSpec((None, L, GW), lambda b, i: (b, 0, R_DV // GW)),
            pl.BlockSpec((HG, 3, TQ, TK), lambda b, i: (0, 0, 0, 0)),
            vec, vec, vec, vec,
            pl.BlockSpec((1, 1), lambda b, i: (0, 0)),
            pl.BlockSpec((1, GW), lambda b, i: (0, 0)),
        ],
        out_specs=pl.BlockSpec((None, TQ, GW), lambda b, i: (b, i, 0)),
        out_shape=jax.ShapeDtypeStruct((B, L, GW), bf16),
        scratch_shapes=[
            pltpu.VMEM((TQ, 1), f32),
            pltpu.VMEM((TQ, 1), f32),
            pltpu.VMEM((TQ, GW), f32),
        ],
        compiler_params=_cparams("parallel", "arbitrary"),
        name="diff",
    )(hr, ht, hr, bias, lq1, lk1, lq2, lk2, lam_init, g_tiled)


def _gelu_tanh(x):
    return 0.5 * x * (1.0 + jnp.tanh(math.sqrt(2.0 / math.pi) * (x + 0.044715 * (x * x * x))))


def _compress_kernel(xk_ref, xv_ref, pk_ref, pv_ref, w1k_ref, w1v_ref, w2kT_ref, w2v_ref, kcT_ref, vc_ref):
    def pre_act(x_ref, p_ref, w1_ref):
        x = x_ref[...].astype(f32)
        first = jnp.dot((x + p_ref[0:1, :]).astype(bf16), w1_ref[0], preferred_element_type=f32)
        second = jnp.dot((x + p_ref[1:2, :]).astype(bf16), w1_ref[1], preferred_element_type=f32)
        n = first.shape[0]
        return _gelu_tanh(first + pltpu.roll(second, n - 1, 0)).astype(bf16)

    gk = pre_act(xk_ref, pk_ref, w1k_ref)
    kcT_ref[...] = lax.dot_general(w2kT_ref[...], gk, (((1,), (1,)), ((), ())),
                                   preferred_element_type=f32).astype(bf16)
    gv = pre_act(xv_ref, pv_ref, w1v_ref)
    vc_ref[...] = jnp.dot(gv, w2v_ref[...], preferred_element_type=f32).astype(bf16)


def _compress(xk, xv, pk, pv, w1k, w1v, w2kT, w2v):
    B, n, W = xk.shape
    xspec = pl.BlockSpec((None, n, W), lambda b: (b, 0, 0))
    full = lambda a: pl.BlockSpec(a.shape, lambda b: (0,) * a.ndim)
    return pl.pallas_call(
        _compress_kernel,
        grid=(B,),
        in_specs=[xspec, xspec, full(pk), full(pv), full(w1k), full(w1v), full(w2kT), full(w2v)],
        out_specs=[
            pl.BlockSpec((None, GW, n), lambda b: (b, 0, 0)),
            pl.BlockSpec((None, n, GW), lambda b: (b, 0, 0)),
        ],
        out_shape=[
            jax.ShapeDtypeStruct((B, GW, n), bf16),
            jax.ShapeDtypeStruct((B, n, GW), bf16),
        ],
        compiler_params=_cparams("parallel"),
        name="nsa_compress",
    )(xk, xv, pk, pv, w1k, w1v, w2kT, w2v)


def _split3(x):
    hi = x.astype(bf16)
    r = x - hi.astype(f32)
    mid = r.astype(bf16)
    lo = (r - mid.astype(f32)).astype(bf16)
    return hi, mid, lo


def _nsa_kernel(q_ref, g_ref, kcT_ref, vc_ref, ksT_ref, vs_ref, kwT_ref, vw_ref, ovl_ref, exp_ref,
                bias_ref, o_ref, m_ref, l_ref, acc_ref, *, n_slc, topn):
    qi = pl.program_id(1)
    ncmp = kcT_ref.shape[1]
    row_l = lax.broadcasted_iota(i32, (TQ, 1), 0)
    row = qi * TQ + row_l
    lane = _lane_iota((1, LANES))
    lane_f = lane.astype(f32)
    colk = _lane_iota((1, TK))
    scale = HEAD_DIM ** -0.5
    q = q_ref[...]
    qhs = [jnp.where(_head_lanes(h), q, jnp.zeros_like(q)) for h in range(HG)]

    cmp_ok = _lane_iota((1, ncmp)) * CMP_STRIDE + (CMP_LEN - 1) <= row
    o_cmp = jnp.zeros((TQ, GW), f32)
    pc_sum = jnp.zeros((TQ, ncmp), f32)
    for h in range(HG):
        s = jnp.dot(qhs[h], kcT_ref[...], preferred_element_type=f32) * scale
        s = jnp.where(cmp_ok, s, NEG)
        e = jnp.where(cmp_ok, jnp.exp(s - jnp.max(s, axis=1, keepdims=True)), 0.0)
        den = jnp.sum(e, axis=1, keepdims=True)
        pc = e / jnp.where(den > 0.0, den, 1.0)
        pc_sum = pc_sum + pc
        o_h = jnp.dot(pc.astype(bf16), vc_ref[...], preferred_element_type=f32)
        o_cmp = jnp.where(_head_lanes(h), o_h, o_cmp)
    imp = jnp.zeros((TQ, LANES), f32)
    for part in _split3(pc_sum):
        imp = imp + jnp.dot(part, ovl_ref[...], preferred_element_type=f32)
    cur = row // SLC_LEN
    forced = (lane == 0) | (lane == cur) | (lane == cur - 1)
    imp = jnp.where(forced, jnp.inf, imp)
    imp = jnp.where((lane * SLC_LEN <= row) & (lane < n_slc), imp, -jnp.inf)

    def pick(_, st):
        imp, sel = st
        best = jnp.max(imp, axis=1, keepdims=True)
        first = jnp.min(jnp.where(imp == best, lane_f, float(LANES)), axis=1, keepdims=True)
        hit = lane_f == first
        return jnp.where(hit, -jnp.inf, imp), jnp.where(hit & (best > -jnp.inf), 1.0, sel)

    _, sel = lax.fori_loop(0, topn, pick, (imp, jnp.zeros((TQ, LANES), f32)))
    selb = sel.astype(bf16)

    o_slc = jnp.zeros((TQ, GW), f32)
    for h in range(HG):
        _flash_init(m_ref, l_ref, acc_ref)

        def body(kc, carry, h=h):
            s = jnp.dot(qhs[h], ksT_ref[kc], preferred_element_type=f32) * scale
            s = s + bias_ref[h, jnp.minimum(qi - kc, 2)]
            tok = jnp.dot(selb, exp_ref[kc], preferred_element_type=f32) > 0.5
            mask = tok & (colk <= row_l + (kc != qi).astype(i32) * TK)
            v = vs_ref[pl.ds(pl.multiple_of(kc * TK, TK), TK), :]
            _flash_update(s, mask, v, m_ref, l_ref, acc_ref)
            return carry

        lax.fori_loop(0, qi + 1, body, 0)
        o_slc = jnp.where(_head_lanes(h), _flash_result(l_ref, acc_ref), o_slc)

    o_win = jnp.zeros((TQ, GW), f32)
    for h in range(HG):
        _flash_init(m_ref, l_ref, acc_ref)

        def body(kc, carry, h=h):
            s = jnp.dot(qhs[h], kwT_ref[kc], preferred_element_type=f32) * scale
            s = s + bias_ref[h, qi - kc]
            oldest = (kc == qi - WIN // TK).astype(i32)
            lo = row_l * oldest + oldest - 1
            hi = row_l + (kc != qi).astype(i32) * TK
            mask = (colk > lo) & (colk <= hi)
            v = vw_ref[pl.ds(pl.multiple_of(kc * TK, TK), TK), :]
            _flash_update(s, mask, v, m_ref, l_ref, acc_ref)
            return carry

        lax.fori_loop(jnp.maximum(qi - WIN // TK, 0), qi + 1, body, 0)
        o_win = jnp.where(_head_lanes(h), _flash_result(l_ref, acc_ref), o_win)

    gates = jax.nn.sigmoid(g_ref[...])
    out = jnp.zeros((TQ, GW), f32)
    for h in range(HG):
        c = S_BG + 3 * h
        mix = gates[:, c:c + 1] * o_cmp + gates[:, c + 1:c + 2] * o_slc + gates[:, c + 2:c + 3] * o_win
        out = jnp.where(_head_lanes(h), mix, out)
    o_ref[...] = out.astype(o_ref.dtype)


def _nsa(hr, ht, hs, kcT, vc, ovl, expand, bias):
    B, L, _ = hr.shape
    nch = L // TK
    n = kcT.shape[2]
    n_slc = L // SLC_LEN
    topn = min(SLC_TOPN, n_slc)
    return pl.pallas_call(
        functools.partial(_nsa_kernel, n_slc=n_slc, topn=topn),
        grid=(B, L // TQ),
        in_specs=[
            pl.BlockSpec((None, TQ, GW), lambda b, i: (b, i, R_BQ // GW)),
            pl.BlockSpec((None, TQ, LANES), lambda b, i: (b, i, 0)),
            pl.BlockSpec((None, GW, n), lambda b, i: (b, 0, 0)),
            pl.BlockSpec((None, n, GW), lambda b, i: (b, 0, 0)),
            pl.BlockSpec((None, nch, GW, TK), lambda b, i: (b, 0, T_BKS // GW, 0)),
            pl.BlockSpec((None, L, GW), lambda b, i: (b, 0, R_BVS // GW)),
            pl.BlockSpec((None, nch, GW, TK), lambda b, i: (b, 0, T_BKW // GW, 0)),
            pl.BlockSpec((None, L, GW), lambda b, i: (b, 0, R_BVW // GW)),
            pl.BlockSpec(ovl.shape, lambda b, i: (0, 0)),
            pl.BlockSpec(expand.shape, lambda b, i: (0, 0, 0)),
            pl.BlockSpec((HG, 3, TQ, TK), lambda b, i: (0, 0, 0, 0)),
        ],
        out_specs=pl.BlockSpec((None, TQ, GW), lambda b, i: (b, i, 0)),
        out_shape=jax.ShapeDtypeStruct((B, L, GW), bf16),
        scratch_shapes=[
            pltpu.VMEM((TQ, 1), f32),
            pltpu.VMEM((TQ, 1), f32),
            pltpu.VMEM((TQ, GW), f32),
        ],
        compiler_params=_cparams("parallel", "arbitrary"),
        name="nsa",
    )(hr, hs, kcT, vc, ht, hr, ht, hr, ovl, expand, bias)


def _nsa_tables(L):
    n = L // CMP_STRIDE
    n_slc = L // SLC_LEN
    c0 = jnp.arange(n, dtype=i32)[:, None] * CMP_STRIDE
    s0 = jnp.arange(LANES, dtype=i32)[None, :] * SLC_LEN
    ovl = (c0 <= s0 + SLC_LEN - 1) & (c0 + CMP_LEN - 1 >= s0)
    ovl = ovl & (jnp.arange(n)[:, None] < n - 1) & (jnp.arange(LANES)[None, :] < n_slc)
    tokblk = jnp.arange(L, dtype=i32) // SLC_LEN
    expand = jnp.arange(LANES, dtype=i32)[:, None] == tokblk[None, :]
    expand = expand.reshape(LANES, L // TK, TK).transpose(1, 0, 2)
    return ovl.astype(bf16), expand.astype(bf16)


def _outproj_kernel(oa_ref, ob_ref, oc_ref, od_ref, w_ref, x_ref, g_ref, b_ref, y_ref):
    mix = jnp.zeros(x_ref.shape, f32)
    for n, o_ref in enumerate((oa_ref, ob_ref, oc_ref, od_ref)):
        mix = mix + jnp.dot(o_ref[...], w_ref[n * GW:(n + 1) * GW, :], preferred_element_type=f32)
    y_ref[...] = _layer_norm(ALPHA * x_ref[...] + mix, g_ref[...], b_ref[...])


def _outproj(oa, ob, oc, od, w, x, g, b, tm=512):
    M, D = x.shape
    ospec = pl.BlockSpec((tm, GW), lambda i: (i, 0))
    vec = pl.BlockSpec((1, D), lambda i: (0, 0))
    return pl.pallas_call(
        _outproj_kernel,
        grid=(M // tm,),
        in_specs=[ospec, ospec, ospec, ospec, pl.BlockSpec((D, D), lambda i: (0, 0)),
                  pl.BlockSpec((tm, D), lambda i: (i, 0)), vec, vec],
        out_specs=pl.BlockSpec((tm, D), lambda i: (i, 0)),
        out_shape=jax.ShapeDtypeStruct((M, D), f32),
        compiler_params=_cparams("parallel"),
        name="outproj_ln",
    )(oa, ob, oc, od, w, x, g, b)


def _memkv_kernel(mem_ref, wkT_ref, wv_ref, kT_ref, v_ref):
    mb = mem_ref[...].astype(bf16)
    kT_ref[...] = lax.dot_general(wkT_ref[...], mb, (((1,), (1,)), ((), ())),
                                  preferred_element_type=f32).astype(bf16)
    v_ref[...] = jnp.dot(mb, wv_ref[...], preferred_element_type=f32).astype(bf16)


def _memkv(mem, wkT, wv):
    B, N, D = mem.shape
    wspec = pl.BlockSpec((D, D), lambda b: (0, 0))
    return pl.pallas_call(
        _memkv_kernel,
        grid=(B,),
        in_specs=[pl.BlockSpec((None, N, D), lambda b: (b, 0, 0)), wspec, wspec],
        out_specs=[pl.BlockSpec((None, D, N), lambda b: (b, 0, 0)),
                   pl.BlockSpec((None, N, D), lambda b: (b, 0, 0))],
        out_shape=[jax.ShapeDtypeStruct((B, D, N), bf16), jax.ShapeDtypeStruct((B, N, D), bf16)],
        compiler_params=_cparams("parallel"),
        name="cross_kv",
    )(mem, wkT, wv)


def _cross_kernel(x_ref, wq_ref, kT_ref, v_ref, wo_ref, g_ref, b_ref, y_ref):
    x = x_ref[...]
    q = jnp.dot(x.astype(bf16), wq_ref[...], preferred_element_type=f32).astype(bf16)
    outs = []
    for h in range(CROSS_HEADS):
        sl = slice(h * CROSS_DIM, (h + 1) * CROSS_DIM)
        s = jnp.dot(q[:, sl], kT_ref[sl, :], preferred_element_type=f32) * CROSS_DIM ** -0.5
        e = jnp.exp(s - jnp.max(s, axis=1, keepdims=True))
        p = e / jnp.sum(e, axis=1, keepdims=True)
        outs.append(jnp.dot(p.astype(bf16), v_ref[:, sl], preferred_element_type=f32).astype(bf16))
    o = jnp.concatenate(outs, axis=1)
    y = ALPHA * x + jnp.dot(o, wo_ref[...], preferred_element_type=f32)
    y_ref[...] = _layer_norm(y, g_ref[...], b_ref[...])


def _cross(x, wq, kT, v, wo, g, b, tm=512):
    B, L, D = x.shape
    N = v.shape[1]
    wspec = pl.BlockSpec((D, D), lambda bb, i: (0, 0))
    vec = pl.BlockSpec((1, D), lambda bb, i: (0, 0))
    return pl.pallas_call(
        _cross_kernel,
        grid=(B, L // tm),
        in_specs=[pl.BlockSpec((None, tm, D), lambda bb, i: (bb, i, 0)), wspec,
                  pl.BlockSpec((None, D, N), lambda bb, i: (bb, 0, 0)),
                  pl.BlockSpec((None, N, D), lambda bb, i: (bb, 0, 0)), wspec, vec, vec],
        out_specs=pl.BlockSpec((None, tm, D), lambda bb, i: (bb, i, 0)),
        out_shape=jax.ShapeDtypeStruct((B, L, D), f32),
        compiler_params=_cparams("parallel", "parallel"),
        name="cross_ln",
    )(x, wq, kT, v, wo, g, b)


def _mlp_kernel(x_ref, w1_ref, w2_ref, g_ref, b_ref, y_ref, acc_ref):
    j = pl.program_id(1)

    @pl.when(j == 0)
    def _():
        acc_ref[...] = jnp.zeros(acc_ref.shape, f32)

    hdn = jnp.dot(x_ref[...].astype(bf16), w1_ref[...], preferred_element_type=f32)
    hdn = jnp.square(jnp.maximum(hdn, 0.0)).astype(bf16)
    acc_ref[...] += jnp.dot(hdn, w2_ref[...], preferred_element_type=f32)

    @pl.when(j == pl.num_programs(1) - 1)
    def _():
        y_ref[...] = _layer_norm(ALPHA * x_ref[...] + acc_ref[...], g_ref[...], b_ref[...])


def _mlp(x, w1, w2, g, b, tm=512, tf=1024):
    M, D = x.shape
    F = w1.shape[1]
    vec = pl.BlockSpec((1, D), lambda i, j: (0, 0))
    return pl.pallas_call(
        _mlp_kernel,
        grid=(M // tm, F // tf),
        in_specs=[pl.BlockSpec((tm, D), lambda i, j: (i, 0)),
                  pl.BlockSpec((D, tf), lambda i, j: (0, j)),
                  pl.BlockSpec((tf, D), lambda i, j: (j, 0)), vec, vec],
        out_specs=pl.BlockSpec((tm, D), lambda i, j: (i, 0)),
        out_shape=jax.ShapeDtypeStruct((M, D), f32),
        scratch_shapes=[pltpu.VMEM((tm, D), f32)],
        compiler_params=_cparams("parallel", "arbitrary"),
        name="mlp_ln",
    )(x, w1, w2, g, b)


def _split_w_in(w):
    sizes = (GW, GW, GW, IDX_HEADS * IDX_DIM, IDX_DIM, IDX_HEADS,
             GW, HEAD_DIM, HEAD_DIM, HEAD_DIM, HEAD_DIM, HEAD_DIM, HEAD_DIM, 3 * HG,
             GW, GW, GW, GW, GW, GW)
    offs = [0]
    for s in sizes:
        offs.append(offs[-1] + s)
    return [w[:, offs[n]:offs[n + 1]] for n in range(len(sizes))]


def _layout_w_in(w):
    (a_q, a_k, a_v, a_qi, a_ki, a_w, b_q, b_kc, b_vc, b_ks, b_vs, b_kw, b_vw, b_g,
     c_q, c_k, c_v, d_q, d_k, d_v) = _split_w_in(w)
    rep = lambda t: jnp.tile(t, (1, HG))
    wr = jnp.concatenate([a_q, a_v, a_qi, b_q, rep(b_vs), rep(b_vw), c_q, c_v, d_q, d_v, b_kc, b_vc], axis=1)
    wt = jnp.concatenate([a_k, c_k, d_k, rep(b_ks), rep(b_kw), a_ki], axis=1).T
    ws = jnp.concatenate([a_w, b_g, jnp.zeros((w.shape[0], LANES - IDX_HEADS - 3 * HG), w.dtype)], axis=1)
    assert wr.shape[1] == NR and wt.shape[0] == NT
    return wr.astype(bf16), wt.astype(bf16), ws.astype(bf16)


def _mixers(x, rel_bias, w_in_l, pos_k, pos_v, w1_k, w2_k, w1_v, w2_v, lq1, lk1, lq2, lk2, diff_g, lam_init):
    B, L, _ = x.shape
    bias = _bias_tiles(rel_bias)
    wr, wt, ws = _layout_w_in(w_in_l)
    hr, ht, hs = _project(x, wr, wt, ws)
    o_a = _dsa(hr, ht, hs, bias[0:HG])

    n = L // CMP_STRIDE
    half = CMP_STRIDE * HEAD_DIM
    xk = hr[:, :, R_BKVC:R_BKVC + HEAD_DIM].reshape(B, n, half)
    xv = hr[:, :, R_BKVC + HEAD_DIM:R_BKVC + 2 * HEAD_DIM].reshape(B, n, half)
    kcT, vc = _compress(
        xk, xv, pos_k.reshape(2, half), pos_v.reshape(2, half),
        w1_k.reshape(2, half, HEAD_DIM).astype(bf16), w1_v.reshape(2, half, HEAD_DIM).astype(bf16),
        jnp.tile(w2_k.T, (HG, 1)).astype(bf16), jnp.tile(w2_v, (1, HG)).astype(bf16))
    ovl, expand = _nsa_tables(L)
    o_b = _nsa(hr, ht, hs, kcT, vc, ovl, expand, bias[HG:2 * HG])
    o_c = _moba(hr, ht, bias[2 * HG:3 * HG])
    o_d = _diff(hr, ht, bias[3 * HG:4 * HG], lq1.reshape(1, -1), lk1.reshape(1, -1),
                lq2.reshape(1, -1), lk2.reshape(1, -1), jnp.full((1, 1), lam_init, f32),
                jnp.tile(diff_g.reshape(1, -1), (1, HG)))
    return o_a, o_b, o_c, o_d


def kernel(x, mem, rel_bias, w_in, w_out, nsa_pos_k, nsa_pos_v, nsa_w1_k, nsa_w2_k, nsa_w1_v, nsa_w2_v, diff_lq1, diff_lk1, diff_lq2, diff_lk2, diff_g, ln1_g, ln1_b, xq, xk, xv, xo, ln2_g, ln2_b, mlp_w1, mlp_w2, ln3_g, ln3_b):
    B, L, D = x.shape
    for l in range(DEPTH):
        lam_init = 0.8 - 0.6 * math.exp(-0.3 * l)
        o_a, o_b, o_c, o_d = _mixers(
            x, rel_bias, w_in[l], nsa_pos_k[l], nsa_pos_v[l], nsa_w1_k[l], nsa_w2_k[l],
            nsa_w1_v[l], nsa_w2_v[l], diff_lq1[l], diff_lk1[l], diff_lq2[l], diff_lk2[l],
            diff_g[l], lam_init)
        flat = lambda t: t.reshape(B * L, -1)
        row = lambda t: t.reshape(1, D)
        x2 = _outproj(flat(o_a), flat(o_b), flat(o_c), flat(o_d), w_out[l].astype(bf16),
                      flat(x), row(ln1_g[l]), row(ln1_b[l]))
        kT, v = _memkv(mem, xk[l].T.astype(bf16), xv[l].astype(bf16))
        x3 = _cross(x2.reshape(B, L, D), xq[l].astype(bf16), kT, v, xo[l].astype(bf16),
                    row(ln2_g[l]), row(ln2_b[l]))
        x = _mlp(flat(x3), mlp_w1[l].astype(bf16), mlp_w2[l].astype(bf16),
                 row(ln3_g[l]), row(ln3_b[l])).reshape(B, L, D)
    return x
```

```python
import functools
import math

import jax
import jax.numpy as jnp
from jax import lax
from jax.experimental import pallas as pl
from jax.experimental.pallas import tpu as pltpu

f32 = jnp.float32
bf16 = jnp.bfloat16
i32 = jnp.int32

D_MODEL = 1024
DEPTH = 4
HEAD_DIM = 64
HG = 4
GW = HG * HEAD_DIM
REL_BUCKETS = 32
REL_MAX_DIST = 128
IDX_HEADS = 8
IDX_DIM = 64
DSA_TOPK = 256
CMP_LEN = 32
CMP_STRIDE = 16
SLC_LEN = 64
SLC_TOPN = 16
WIN = 512
MOBA_BLOCK = 256
MOBA_TOPK = 3
DIFF_HALF = HEAD_DIM // 2
CROSS_HEADS = 4
CROSS_DIM = D_MODEL // CROSS_HEADS
D_FF = 4 * D_MODEL
ALPHA = (2 * DEPTH) ** 0.25

NEG = -1e30
FLOOR = -1e29
INT_MIN = -(2 ** 31)
LANES = 128
TQ = 256
TK = 256
VMEM_LIMIT = 56 * 1024 * 1024
CHAINS_AHEAD = 4

R_AK, R_CK, R_DK, R_BKS, R_BKW, R_AKI, R_BKVC = 0, 256, 512, 768, 896, 1024, 1152
NR = 1280
T_AQ, T_AV, T_AQI, T_BQ, T_CQ, T_CV, T_DQ, T_DV, T_BVS, T_BVW = (
    0, 256, 512, 1024, 1280, 1536, 1792, 2048, 2304, 2368)
NT = 2432
S_AW, S_BG, NS = 0, 8, 32


def _cparams(*sem):
    return pltpu.CompilerParams(dimension_semantics=sem, vmem_limit_bytes=VMEM_LIMIT)


def _rel_bucket(dist):
    n = jnp.maximum(dist, 0)
    max_exact = REL_BUCKETS // 2
    nf = jnp.maximum(n, max_exact).astype(f32)
    large = max_exact + (jnp.log(nf / max_exact) / math.log(REL_MAX_DIST / max_exact)
                         * (REL_BUCKETS - max_exact)).astype(i32)
    large = jnp.minimum(large, REL_BUCKETS - 1)
    return jnp.where(n < max_exact, n, large)


def _bias_tiles(rel_bias):
    assert 2 * TQ - TK + 1 >= REL_MAX_DIST
    d = jnp.arange(TQ, dtype=i32)[None, :] - jnp.arange(TK, dtype=i32)[:, None]
    tiles = jnp.stack([rel_bias[_rel_bucket(d + off)] for off in (0, TQ)])
    far = rel_bias[_rel_bucket(jnp.int32(2 * TQ))]
    return (tiles - far).transpose(3, 0, 1, 2).astype(f32)


def _row_iota(shape):
    return lax.broadcasted_iota(i32, shape, 0)


def _col_iota(shape):
    return lax.broadcasted_iota(i32, shape, 1)


def _flash_init(m_ref, l_ref, acc_ref):
    m_ref[...] = jnp.full(m_ref.shape, NEG, f32)
    l_ref[...] = jnp.zeros(l_ref.shape, f32)
    acc_ref[...] = jnp.zeros(acc_ref.shape, f32)


def _flash_step(c, s, vT, m_ref, l_ref, acc_ref):
    m_prev = m_ref[c]
    m_new = jnp.maximum(m_prev, jnp.max(s, axis=0, keepdims=True))
    m_use = jnp.maximum(m_new, FLOOR)
    alpha = jnp.exp(jnp.maximum(m_prev, FLOOR) - m_use)
    p = jnp.exp(s - m_use)
    l_ref[c] = alpha * l_ref[c] + jnp.sum(p, axis=0, keepdims=True)
    acc_ref[c] = alpha * acc_ref[c] + jnp.dot(vT, p.astype(bf16), preferred_element_type=f32)
    m_ref[c] = m_new


def _chains(n, logits, finish, vT, m_ref, l_ref, acc_ref):
    pending = [logits(c) for c in range(min(CHAINS_AHEAD, n))]
    for c in range(n):
        if c + CHAINS_AHEAD < n:
            pending.append(logits(c + CHAINS_AHEAD))
        _flash_step(c, finish(c, pending[c]), vT(c), m_ref, l_ref, acc_ref)


def _flash_result(c, l_ref, acc_ref):
    l = l_ref[c]
    return jnp.where(l > 0.0, acc_ref[c] / jnp.where(l > 0.0, l, 1.0), 0.0)


def _sweep(qi, step, first=0):
    def far(kc, carry):
        step(kc, 2)
        return carry

    lax.fori_loop(first, jnp.maximum(qi - 1, first), far, 0)

    @pl.when(qi >= 1)
    def _():
        step(qi - 1, 1)

    step(qi, 0)


def _pair_rows(qT_ref, h, width=HEAD_DIM, offset=0, scale=None):
    blk = qT_ref[(h // 2) * LANES:(h // 2 + 1) * LANES, :]
    r = _row_iota((LANES, 1))
    lo = (h % 2) * HEAD_DIM + offset
    blk = jnp.where((r >= lo) & (r < lo + width), blk, jnp.zeros_like(blk))
    return blk if scale is None else blk * scale


def _layer_norm(y, g, b):
    mu = jnp.mean(y, axis=-1, keepdims=True)
    yc = y - mu
    var = jnp.mean(yc * yc, axis=-1, keepdims=True)
    return yc * lax.rsqrt(var + 1e-5) * g + b


def _proj_kernel(x_ref, wr_ref, wt_ref, ws_ref, hr_ref, ht_ref, hs_ref):
    xb = x_ref[...].astype(bf16)
    nt = (((1,), (1,)), ((), ()))
    hr_ref[...] = jnp.dot(xb, wr_ref[...], preferred_element_type=f32).astype(bf16)
    ht_ref[...] = lax.dot_general(wt_ref[...], xb, nt, preferred_element_type=f32).astype(bf16)
    hs_ref[...] = lax.dot_general(ws_ref[...], xb, nt, preferred_element_type=f32)


def _project(x, wr, wt, ws):
    B, L, D = x.shape
    nch = L // TK
    return pl.pallas_call(
        _proj_kernel,
        grid=(B, nch),
        in_specs=[
            pl.BlockSpec((None, TK, D), lambda b, i: (b, i, 0)),
            pl.BlockSpec((D, NR), lambda b, i: (0, 0)),
            pl.BlockSpec((NT, D), lambda b, i: (0, 0)),
            pl.BlockSpec((NS, D), lambda b, i: (0, 0)),
        ],
        out_specs=[
            pl.BlockSpec((None, TK, NR), lambda b, i: (b, i, 0)),
            pl.BlockSpec((None, None, NT, TK), lambda b, i: (b, i, 0, 0)),
            pl.BlockSpec((None, None, NS, TK), lambda b, i: (b, i, 0, 0)),
        ],
        out_shape=[
            jax.ShapeDtypeStruct((B, L, NR), bf16),
            jax.ShapeDtypeStruct((B, nch, NT, TK), bf16),
            jax.ShapeDtypeStruct((B, nch, NS, TK), f32),
        ],
        compiler_params=_cparams("parallel", "parallel"),
        name="proj",
    )(x, wr, wt, ws)


def _q_spec(off, rows=GW):
    return pl.BlockSpec((None, None, rows, TQ), lambda b, i: (b, i, off // rows, 0))


def _vT_spec(nch, off, rows=GW):
    return pl.BlockSpec((None, nch, rows, TK), lambda b, i: (b, 0, off // rows, 0))


def _k_spec(L, off, cols=GW):
    return pl.BlockSpec((None, L, cols), lambda b, i: (b, 0, off // cols))


_BIAS_SPEC = pl.BlockSpec((HG, 2, TK, TQ), lambda b, i: (0, 0, 0, 0))
_OUT_SPEC = pl.BlockSpec((None, TQ, GW), lambda b, i: (b, i, 0))


def _flash_scratch(chains, dv=HEAD_DIM):
    return [pltpu.VMEM((chains, 1, TQ), f32), pltpu.VMEM((chains, 1, TQ), f32),
            pltpu.VMEM((chains, dv, TQ), f32)]


def _k_chunk(k_ref, kc, pair=None):
    rows = pl.ds(pl.multiple_of(kc * TK, TK), TK)
    if pair is None:
        return k_ref[rows, :]
    return k_ref[rows, pair * LANES:(pair + 1) * LANES]


def _dsa_kernel(qT_ref, qiT_ref, wT_ref, k_ref, ki_ref, vT_ref, bias_ref, o_ref,
                key_ref, m_ref, l_ref, acc_ref, *, topk):
    qi = pl.program_id(1)
    nch = qi + 1
    qpos = qi * TQ + _col_iota((1, TQ))
    krow = _row_iota((TK, 1))
    wT = wT_ref[...] * (IDX_DIM ** -0.5 * IDX_HEADS ** -0.5)
    qidx = [_pair_rows(qiT_ref, h) for h in range(IDX_HEADS)]

    def score_chunk(kc, carry):
        ki2 = _k_chunk(ki_ref, kc)
        s = jnp.zeros((TK, TQ), f32)
        for h in range(IDX_HEADS):
            r = jnp.dot(ki2, qidx[h], preferred_element_type=f32)
            s = s + wT[S_AW + h:S_AW + h + 1, :] * jnp.maximum(r, 0.0)
        s = jnp.where(s == 0.0, 0.0, s)
        bits = lax.bitcast_convert_type(s, i32)
        key = bits ^ ((bits >> 31) & 0x7FFFFFFF)
        key_ref[kc] = jnp.where(kc * TK + krow <= qpos, key, INT_MIN)
        return carry

    lax.fori_loop(0, nch, score_chunk, 0)

    def count(pred):
        def body(kc, acc):
            return acc + jnp.sum(jnp.where(pred(key_ref[kc]), 1.0, 0.0), axis=0, keepdims=True)
        return lax.fori_loop(0, nch, body, jnp.zeros((1, TQ), f32))

    def bisect(i, t_u):
        cand_u = t_u | jnp.left_shift(jnp.int32(1), 31 - i)
        cand = cand_u ^ INT_MIN
        return jnp.where(count(lambda k: k >= cand) >= topk, cand_u, t_u)

    t_u = lax.fori_loop(0, 32, bisect, jnp.zeros((1, TQ), i32))
    thr = jnp.maximum(t_u ^ INT_MIN, INT_MIN + 1)
    cnt_ge = count(lambda k: k >= thr)

    @pl.when(jnp.max(cnt_ge) > topk)
    def _():
        need = topk - (cnt_ge - count(lambda k: k == thr))
        tri = (_col_iota((TK, TK)) <= _row_iota((TK, TK))).astype(bf16)

        def body(kc, seen):
            k = key_ref[kc]
            tie = k == thr
            tief = jnp.where(tie, 1.0, 0.0)
            pref = jnp.dot(tri, tief.astype(bf16), preferred_element_type=f32) + seen
            key_ref[kc] = jnp.where(tie & (pref > need), INT_MIN, k)
            return seen + jnp.sum(tief, axis=0, keepdims=True)

        lax.fori_loop(0, nch, body, jnp.zeros((1, TQ), f32))

    qs = [_pair_rows(qT_ref, h, scale=HEAD_DIM ** -0.5) for h in range(HG)]
    _flash_init(m_ref, l_ref, acc_ref)

    def step(kc, kind):
        keep = key_ref[kc] >= thr

        def finish(h, s):
            if kind < 2:
                s = s + bias_ref[h, kind]
            return jnp.where(keep, s, NEG)

        _chains(HG, lambda h: jnp.dot(_k_chunk(k_ref, kc, h // 2), qs[h], preferred_element_type=f32),
                finish, lambda h: vT_ref[kc, h * HEAD_DIM:(h + 1) * HEAD_DIM, :], m_ref, l_ref, acc_ref)

    _sweep(qi, step)
    outT = jnp.concatenate([_flash_result(h, l_ref, acc_ref) for h in range(HG)], axis=0)
    o_ref[...] = outT.T.astype(o_ref.dtype)


def _dsa(hr, ht, hs, bias):
    B, L, _ = hr.shape
    nch = L // TK
    topk = min(DSA_TOPK, L // 4)
    return pl.pallas_call(
        functools.partial(_dsa_kernel, topk=topk),
        grid=(B, L // TQ),
        in_specs=[
            _q_spec(T_AQ), _q_spec(T_AQI, 2 * GW), _q_spec(0, NS),
            _k_spec(L, R_AK), _k_spec(L, R_AKI, LANES), _vT_spec(nch, T_AV), _BIAS_SPEC,
        ],
        out_specs=_OUT_SPEC,
        out_shape=jax.ShapeDtypeStruct((B, L, GW), bf16),
        scratch_shapes=[pltpu.VMEM((nch, TK, TQ), i32)] + _flash_scratch(HG),
        compiler_params=_cparams("parallel", "arbitrary"),
        name="dsa",
    )(ht, ht, hs, hr, hr, ht, bias)


def _moba_kernel(qT_ref, k_ref, vT_ref, bias_ref, o_ref, km_ref, sel_ref, m_ref, l_ref, acc_ref, *, nch, topk):
    qi = pl.program_id(1)
    nb = km_ref.shape[0]

    @pl.when(qi == 0)
    def _():
        km_ref[...] = jnp.zeros(km_ref.shape, f32)
        for n in range(nch):
            blk = k_ref[n * MOBA_BLOCK:(n + 1) * MOBA_BLOCK, :].astype(f32)
            km_ref[n:n + 1, :] = jnp.sum(blk, axis=0, keepdims=True) * (1.0 / MOBA_BLOCK)

    km = km_ref[...]
    km_hi = km.astype(bf16)
    km_lo = (km - km_hi.astype(f32)).astype(bf16)
    blk_id = _row_iota((nb, 1))
    blk_f = blk_id.astype(f32)
    for h in range(HG):
        qh = qT_ref[...]
        r = _row_iota((GW, 1))
        qh = jnp.where((r >= h * HEAD_DIM) & (r < (h + 1) * HEAD_DIM), qh, jnp.zeros_like(qh))
        gate = (jnp.dot(km_hi, qh, preferred_element_type=f32)
                + jnp.dot(km_lo, qh, preferred_element_type=f32))
        gate = jnp.where(blk_id < qi, gate, -jnp.inf)
        sel = jnp.zeros((nb, TQ), f32)
        for _ in range(topk):
            best = jnp.max(gate, axis=0, keepdims=True)
            first = jnp.min(jnp.where(gate == best, blk_f, float(nb)), axis=0, keepdims=True)
            hit = blk_f == first
            sel = jnp.where(hit & (best > -jnp.inf), 1.0, sel)
            gate = jnp.where(hit, -jnp.inf, gate)
        sel_ref[h] = sel

    qs = [_pair_rows(qT_ref, h, scale=HEAD_DIM ** -0.5) for h in range(HG)]
    causal = _row_iota((TK, 1)) <= _col_iota((1, TQ))
    _flash_init(m_ref, l_ref, acc_ref)

    def step(kc, kind):
        def finish(h, s):
            if kind < 2:
                s = s + bias_ref[h, kind]
            if kind == 0:
                return jnp.where(causal, s, NEG)
            return jnp.where(sel_ref[h, pl.ds(kc, 1), :] > 0.5, s, NEG)

        _chains(HG, lambda h: jnp.dot(_k_chunk(k_ref, kc, h // 2), qs[h], preferred_element_type=f32),
                finish, lambda h: vT_ref[kc, h * HEAD_DIM:(h + 1) * HEAD_DIM, :], m_ref, l_ref, acc_ref)

    _sweep(qi, step)
    outT = jnp.concatenate([_flash_result(h, l_ref, acc_ref) for h in range(HG)], axis=0)
    o_ref[...] = outT.T.astype(o_ref.dtype)


def _moba(hr, ht, bias):
    B, L, _ = hr.shape
    nch = L // TK
    nb = -(-nch // 8) * 8
    topk = min(MOBA_TOPK, nch - 1)
    return pl.pallas_call(
        functools.partial(_moba_kernel, nch=nch, topk=topk),
        grid=(B, L // TQ),
        in_specs=[_q_spec(T_CQ), _k_spec(L, R_CK), _vT_spec(nch, T_CV), _BIAS_SPEC],
        out_specs=_OUT_SPEC,
        out_shape=jax.ShapeDtypeStruct((B, L, GW), bf16),
        scratch_shapes=[pltpu.VMEM((nb, GW), f32), pltpu.VMEM((HG, nb, TQ), f32)] + _flash_scratch(HG),
        compiler_params=_cparams("parallel", "arbitrary"),
        name="moba",
    )(ht, hr, ht, bias)


def _diff_kernel(qT_ref, k_ref, vT_ref, bias_ref, lq1_ref, lk1_ref, lq2_ref, lk2_ref, li_ref, g_ref,
                 o_ref, m_ref, l_ref, acc_ref):
    qi = pl.program_id(1)
    lam_init = li_ref[...]
    lam = (jnp.exp(jnp.sum(lq1_ref[...] * lk1_ref[...], axis=1, keepdims=True))
           - jnp.exp(jnp.sum(lq2_ref[...] * lk2_ref[...], axis=1, keepdims=True)) + lam_init)
    qs = [_pair_rows(qT_ref, h, DIFF_HALF, c * DIFF_HALF) for h in range(HG) for c in range(2)]
    causal = _row_iota((TK, 1)) <= _col_iota((1, TQ))
    _flash_init(m_ref, l_ref, acc_ref)

    def step(kc, kind):
        def finish(c, s):
            s = s * DIFF_HALF ** -0.5
            if kind < 2:
                s = s + bias_ref[c // 2, kind]
            return jnp.where(causal, s, NEG) if kind == 0 else s

        _chains(2 * HG, lambda c: jnp.dot(_k_chunk(k_ref, kc, c // 4), qs[c], preferred_element_type=f32),
                finish, lambda c: vT_ref[kc, (c // 2) * HEAD_DIM:(c // 2 + 1) * HEAD_DIM, :],
                m_ref, l_ref, acc_ref)

    _sweep(qi, step)
    heads = []
    for h in range(HG):
        o = _flash_result(2 * h, l_ref, acc_ref) - lam * _flash_result(2 * h + 1, l_ref, acc_ref)
        ms = jnp.mean(o * o, axis=0, keepdims=True)
        heads.append(o * lax.rsqrt(ms + 1e-6) * g_ref[...] * (1.0 - lam_init))
    o_ref[...] = jnp.concatenate(heads, axis=0).T.astype(o_ref.dtype)


def _diff(hr, ht, bias, lq1, lk1, lq2, lk2, lam_init, g_col):
    B, L, _ = hr.shape
    nch = L // TK
    vec = pl.BlockSpec((1, DIFF_HALF), lambda b, i: (0, 0))
    return pl.pallas_call(
        _diff_kernel,
        grid=(B, L // TQ),
        in_specs=[
            _q_spec(T_DQ), _k_spec(L, R_DK), _vT_spec(nch, T_DV), _BIAS_SPEC,
            vec, vec, vec, vec,
            pl.BlockSpec((1, 1), lambda b, i: (0, 0)),
            pl.BlockSpec((HEAD_DIM, 1), lambda b, i: (0, 0)),
        ],
        out_specs=_OUT_SPEC,
        out_shape=jax.ShapeDtypeStruct((B, L, GW), bf16),
        scratch_shapes=_flash_scratch(2 * HG),
        compiler_params=_cparams("parallel", "arbitrary"),
        name="diff",
    )(ht, hr, ht, bias, lq1, lk1, lq2, lk2, lam_init, g_col)


def _gelu_tanh(x):
    return 0.5 * x * (1.0 + jnp.tanh(math.sqrt(2.0 / math.pi) * (x + 0.044715 * (x * x * x))))


def _compress_kernel(xk_ref, xv_ref, pk_ref, pv_ref, w1k_ref, w1v_ref, w2k_ref, w2vT_ref, kc_ref, vcT_ref):
    def pre_act(x_ref, p_ref, w1_ref):
        x = x_ref[...].astype(f32)
        first = jnp.dot((x + p_ref[0:1, :]).astype(bf16), w1_ref[0], preferred_element_type=f32)
        second = jnp.dot((x + p_ref[1:2, :]).astype(bf16), w1_ref[1], preferred_element_type=f32)
        n = first.shape[0]
        return _gelu_tanh(first + pltpu.roll(second, n - 1, 0)).astype(bf16)

    gk = pre_act(xk_ref, pk_ref, w1k_ref)
    kc_ref[...] = jnp.dot(gk, w2k_ref[...], preferred_element_type=f32).astype(bf16)
    gv = pre_act(xv_ref, pv_ref, w1v_ref)
    vcT_ref[...] = lax.dot_general(w2vT_ref[...], gv, (((1,), (1,)), ((), ())),
                                   preferred_element_type=f32).astype(bf16)


def _compress(xk, xv, pk, pv, w1k, w1v, w2k, w2vT):
    B, n, W = xk.shape
    xspec = pl.BlockSpec((None, n, W), lambda b: (b, 0, 0))
    full = lambda a: pl.BlockSpec(a.shape, lambda b: (0,) * a.ndim)
    return pl.pallas_call(
        _compress_kernel,
        grid=(B,),
        in_specs=[xspec, xspec, full(pk), full(pv), full(w1k), full(w1v), full(w2k), full(w2vT)],
        out_specs=[
            pl.BlockSpec((None, n, LANES), lambda b: (b, 0, 0)),
            pl.BlockSpec((None, HEAD_DIM, n), lambda b: (b, 0, 0)),
        ],
        out_shape=[
            jax.ShapeDtypeStruct((B, n, LANES), bf16),
            jax.ShapeDtypeStruct((B, HEAD_DIM, n), bf16),
        ],
        compiler_params=_cparams("parallel"),
        name="nsa_compress",
    )(xk, xv, pk, pv, w1k, w1v, w2k, w2vT)


def _split3(x):
    hi = x.astype(bf16)
    r = x - hi.astype(f32)
    mid = r.astype(bf16)
    lo = (r - mid.astype(f32)).astype(bf16)
    return hi, mid, lo


def _nsa_kernel(qT_ref, gT_ref, kc_ref, vcT_ref, ks_ref, vsT_ref, kw_ref, vwT_ref, ovlT_ref, exp_ref,
                bias_ref, o_ref, m_ref, l_ref, acc_ref, *, n_slc, topn):
    qi = pl.program_id(1)
    ncmp = kc_ref.shape[0]
    qpos = qi * TQ + _col_iota((1, TQ))
    pad = jnp.zeros((LANES - HEAD_DIM, TQ), bf16)
    qs = [jnp.concatenate([qT_ref[h * HEAD_DIM:(h + 1) * HEAD_DIM, :] * HEAD_DIM ** -0.5, pad], axis=0)
          for h in range(HG)]

    cmp_ok = _row_iota((ncmp, 1)) * CMP_STRIDE + (CMP_LEN - 1) <= qpos
    o_cmp = []
    pc_sum = jnp.zeros((ncmp, TQ), f32)
    for h in range(HG):
        s = jnp.where(cmp_ok, jnp.dot(kc_ref[...], qs[h], preferred_element_type=f32), NEG)
        e = jnp.exp(s - jnp.maximum(jnp.max(s, axis=0, keepdims=True), FLOOR))
        den = jnp.sum(e, axis=0, keepdims=True)
        pc = e / jnp.where(den > 0.0, den, 1.0)
        pc_sum = pc_sum + pc
        o_cmp.append(jnp.dot(vcT_ref[...], pc.astype(bf16), preferred_element_type=f32))
    imp = jnp.zeros((LANES, TQ), f32)
    for part in _split3(pc_sum):
        imp = imp + jnp.dot(ovlT_ref[...], part, preferred_element_type=f32)
    blk = _row_iota((LANES, 1))
    blk_f = blk.astype(f32)
    cur = qpos // SLC_LEN
    forced = (blk == 0) | (blk == cur) | (blk == cur - 1)
    imp = jnp.where(forced, jnp.inf, imp)
    imp = jnp.where((blk * SLC_LEN <= qpos) & (blk < n_slc), imp, -jnp.inf)

    def pick(_, st):
        imp, sel = st
        best = jnp.max(imp, axis=0, keepdims=True)
        first = jnp.min(jnp.where(imp == best, blk_f, float(LANES)), axis=0, keepdims=True)
        hit = blk_f == first
        return jnp.where(hit, -jnp.inf, imp), jnp.where(hit & (best > -jnp.inf), 1.0, sel)

    _, sel = lax.fori_loop(0, topn, pick, (imp, jnp.zeros((LANES, TQ), f32)))
    selb = sel.astype(bf16)

    krow = _row_iota((TK, 1))
    qcol = _col_iota((1, TQ))
    causal = krow <= qcol

    _flash_init(m_ref, l_ref, acc_ref)

    def slc_step(kc, kind):
        keep = jnp.dot(exp_ref[kc], selb, preferred_element_type=f32) > 0.5
        if kind == 0:
            keep = keep & causal
        k = _k_chunk(ks_ref, kc)

        def finish(h, s):
            if kind < 2:
                s = s + bias_ref[h, kind]
            return jnp.where(keep, s, NEG)

        _chains(HG, lambda h: jnp.dot(k, qs[h], preferred_element_type=f32), finish,
                lambda h: vsT_ref[kc], m_ref, l_ref, acc_ref)

    _sweep(qi, slc_step)
    o_slc = [_flash_result(h, l_ref, acc_ref) for h in range(HG)]

    _flash_init(m_ref, l_ref, acc_ref)

    def win_step(kc, kind):
        k = _k_chunk(kw_ref, kc)

        def finish(h, s):
            if kind < 2:
                s = s + bias_ref[h, kind]
            if kind == 0:
                return jnp.where(causal, s, NEG)
            return jnp.where(krow > qcol, s, NEG) if kind == 2 else s

        _chains(HG, lambda h: jnp.dot(k, qs[h], preferred_element_type=f32), finish,
                lambda h: vwT_ref[kc], m_ref, l_ref, acc_ref)

    _sweep(qi, win_step, first=jnp.maximum(qi - WIN // TK, 0))
    o_win = [_flash_result(h, l_ref, acc_ref) for h in range(HG)]

    gates = jax.nn.sigmoid(gT_ref[...])
    heads = []
    for h in range(HG):
        c = S_BG + 3 * h
        heads.append(gates[c:c + 1, :] * o_cmp[h] + gates[c + 1:c + 2, :] * o_slc[h]
                     + gates[c + 2:c + 3, :] * o_win[h])
    o_ref[...] = jnp.concatenate(heads, axis=0).T.astype(o_ref.dtype)


def _nsa(hr, ht, hs, kc, vcT, ovlT, expand, bias):
    B, L, _ = hr.shape
    nch = L // TK
    n = kc.shape[1]
    n_slc = L // SLC_LEN
    topn = min(SLC_TOPN, n_slc)
    return pl.pallas_call(
        functools.partial(_nsa_kernel, n_slc=n_slc, topn=topn),
        grid=(B, L // TQ),
        in_specs=[
            _q_spec(T_BQ), _q_spec(0, NS),
            pl.BlockSpec((None, n, LANES), lambda b, i: (b, 0, 0)),
            pl.BlockSpec((None, HEAD_DIM, n), lambda b, i: (b, 0, 0)),
            _k_spec(L, R_BKS, LANES), _vT_spec(nch, T_BVS, HEAD_DIM),
            _k_spec(L, R_BKW, LANES), _vT_spec(nch, T_BVW, HEAD_DIM),
            pl.BlockSpec(ovlT.shape, lambda b, i: (0, 0)),
            pl.BlockSpec(expand.shape, lambda b, i: (0, 0, 0)),
            _BIAS_SPEC,
        ],
        out_specs=_OUT_SPEC,
        out_shape=jax.ShapeDtypeStruct((B, L, GW), bf16),
        scratch_shapes=_flash_scratch(HG),
        compiler_params=_cparams("parallel", "arbitrary"),
        name="nsa",
    )(ht, hs, kc, vcT, hr, ht, hr, ht, ovlT, expand, bias)


def _nsa_tables(L):
    n = L // CMP_STRIDE
    n_slc = L // SLC_LEN
    c0 = jnp.arange(n, dtype=i32)[None, :] * CMP_STRIDE
    s0 = jnp.arange(LANES, dtype=i32)[:, None] * SLC_LEN
    ovlT = (c0 <= s0 + SLC_LEN - 1) & (c0 + CMP_LEN - 1 >= s0)
    ovlT = ovlT & (jnp.arange(n)[None, :] < n - 1) & (jnp.arange(LANES)[:, None] < n_slc)
    tokblk = jnp.arange(L, dtype=i32) // SLC_LEN
    expand = tokblk[:, None] == jnp.arange(LANES, dtype=i32)[None, :]
    return ovlT.astype(bf16), expand.reshape(L // TK, TK, LANES).astype(bf16)


def _outproj_kernel(oa_ref, ob_ref, oc_ref, od_ref, w_ref, x_ref, g_ref, b_ref, y_ref):
    mix = jnp.zeros(x_ref.shape, f32)
    for n, o_ref in enumerate((oa_ref, ob_ref, oc_ref, od_ref)):
        mix = mix + jnp.dot(o_ref[...], w_ref[n * GW:(n + 1) * GW, :], preferred_element_type=f32)
    y_ref[...] = _layer_norm(ALPHA * x_ref[...] + mix, g_ref[...], b_ref[...])


def _outproj(oa, ob, oc, od, w, x, g, b, tm=512):
    M, D = x.shape
    ospec = pl.BlockSpec((tm, GW), lambda i: (i, 0))
    vec = pl.BlockSpec((1, D), lambda i: (0, 0))
    return pl.pallas_call(
        _outproj_kernel,
        grid=(M // tm,),
        in_specs=[ospec, ospec, ospec, ospec, pl.BlockSpec((D, D), lambda i: (0, 0)),
                  pl.BlockSpec((tm, D), lambda i: (i, 0)), vec, vec],
        out_specs=pl.BlockSpec((tm, D), lambda i: (i, 0)),
        out_shape=jax.ShapeDtypeStruct((M, D), f32),
        compiler_params=_cparams("parallel"),
        name="outproj_ln",
    )(oa, ob, oc, od, w, x, g, b)


def _memkv_kernel(mem_ref, wkT_ref, wv_ref, kT_ref, v_ref):
    mb = mem_ref[...].astype(bf16)
    kT_ref[...] = lax.dot_general(wkT_ref[...], mb, (((1,), (1,)), ((), ())),
                                  preferred_element_type=f32).astype(bf16)
    v_ref[...] = jnp.dot(mb, wv_ref[...], preferred_element_type=f32).astype(bf16)


def _memkv(mem, wkT, wv):
    B, N, D = mem.shape
    wspec = pl.BlockSpec((D, D), lambda b: (0, 0))
    return pl.pallas_call(
        _memkv_kernel,
        grid=(B,),
        in_specs=[pl.BlockSpec((None, N, D), lambda b: (b, 0, 0)), wspec, wspec],
        out_specs=[pl.BlockSpec((None, D, N), lambda b: (b, 0, 0)),
                   pl.BlockSpec((None, N, D), lambda b: (b, 0, 0))],
        out_shape=[jax.ShapeDtypeStruct((B, D, N), bf16), jax.ShapeDtypeStruct((B, N, D), bf16)],
        compiler_params=_cparams("parallel"),
        name="cross_kv",
    )(mem, wkT, wv)


def _cross_kernel(x_ref, wq_ref, kT_ref, v_ref, wo_ref, g_ref, b_ref, y_ref):
    x = x_ref[...]
    q = jnp.dot(x.astype(bf16), wq_ref[...], preferred_element_type=f32).astype(bf16)
    outs = []
    for h in range(CROSS_HEADS):
        sl = slice(h * CROSS_DIM, (h + 1) * CROSS_DIM)
        s = jnp.dot(q[:, sl], kT_ref[sl, :], preferred_element_type=f32) * CROSS_DIM ** -0.5
        e = jnp.exp(s - jnp.max(s, axis=1, keepdims=True))
        p = e / jnp.sum(e, axis=1, keepdims=True)
        outs.append(jnp.dot(p.astype(bf16), v_ref[:, sl], preferred_element_type=f32).astype(bf16))
    o = jnp.concatenate(outs, axis=1)
    y = ALPHA * x + jnp.dot(o, wo_ref[...], preferred_element_type=f32)
    y_ref[...] = _layer_norm(y, g_ref[...], b_ref[...])


def _cross(x, wq, kT, v, wo, g, b, tm=512):
    B, L, D = x.shape
    N = v.shape[1]
    wspec = pl.BlockSpec((D, D), lambda bb, i: (0, 0))
    vec = pl.BlockSpec((1, D), lambda bb, i: (0, 0))
    return pl.pallas_call(
        _cross_kernel,
        grid=(B, L // tm),
        in_specs=[pl.BlockSpec((None, tm, D), lambda bb, i: (bb, i, 0)), wspec,
                  pl.BlockSpec((None, D, N), lambda bb, i: (bb, 0, 0)),
                  pl.BlockSpec((None, N, D), lambda bb, i: (bb, 0, 0)), wspec, vec, vec],
        out_specs=pl.BlockSpec((None, tm, D), lambda bb, i: (bb, i, 0)),
        out_shape=jax.ShapeDtypeStruct((B, L, D), f32),
        compiler_params=_cparams("parallel", "parallel"),
        name="cross_ln",
    )(x, wq, kT, v, wo, g, b)


def _mlp_kernel(x_ref, w1_ref, w2_ref, g_ref, b_ref, y_ref, acc_ref):
    j = pl.program_id(1)

    @pl.when(j == 0)
    def _():
        acc_ref[...] = jnp.zeros(acc_ref.shape, f32)

    hdn = jnp.dot(x_ref[...].astype(bf16), w1_ref[...], preferred_element_type=f32)
    hdn = jnp.square(jnp.maximum(hdn, 0.0)).astype(bf16)
    acc_ref[...] += jnp.dot(hdn, w2_ref[...], preferred_element_type=f32)

    @pl.when(j == pl.num_programs(1) - 1)
    def _():
        y_ref[...] = _layer_norm(ALPHA * x_ref[...] + acc_ref[...], g_ref[...], b_ref[...])


def _mlp(x, w1, w2, g, b, tm=512, tf=1024):
    M, D = x.shape
    F = w1.shape[1]
    vec = pl.BlockSpec((1, D), lambda i, j: (0, 0))
    return pl.pallas_call(
        _mlp_kernel,
        grid=(M // tm, F // tf),
        in_specs=[pl.BlockSpec((tm, D), lambda i, j: (i, 0)),
                  pl.BlockSpec((D, tf), lambda i, j: (0, j)),
                  pl.BlockSpec((tf, D), lambda i, j: (j, 0)), vec, vec],
        out_specs=pl.BlockSpec((tm, D), lambda i, j: (i, 0)),
        out_shape=jax.ShapeDtypeStruct((M, D), f32),
        scratch_shapes=[pltpu.VMEM((tm, D), f32)],
        compiler_params=_cparams("parallel", "arbitrary"),
        name="mlp_ln",
    )(x, w1, w2, g, b)


def _split_w_in(w):
    sizes = (GW, GW, GW, IDX_HEADS * IDX_DIM, IDX_DIM, IDX_HEADS,
             GW, HEAD_DIM, HEAD_DIM, HEAD_DIM, HEAD_DIM, HEAD_DIM, HEAD_DIM, 3 * HG,
             GW, GW, GW, GW, GW, GW)
    offs = [0]
    for s in sizes:
        offs.append(offs[-1] + s)
    return [w[:, offs[n]:offs[n + 1]] for n in range(len(sizes))]


def _layout_w_in(w):
    (a_q, a_k, a_v, a_qi, a_ki, a_w, b_q, b_kc, b_vc, b_ks, b_vs, b_kw, b_vw, b_g,
     c_q, c_k, c_v, d_q, d_k, d_v) = _split_w_in(w)
    twice = lambda t: jnp.concatenate([t, t], axis=1)
    wr = jnp.concatenate([a_k, c_k, d_k, twice(b_ks), twice(b_kw), twice(a_ki), b_kc, b_vc], axis=1)
    wt = jnp.concatenate([a_q, a_v, a_qi, b_q, c_q, c_v, d_q, d_v, b_vs, b_vw], axis=1).T
    ws = jnp.concatenate([a_w, b_g, jnp.zeros((w.shape[0], NS - IDX_HEADS - 3 * HG), w.dtype)], axis=1).T
    assert wr.shape[1] == NR and wt.shape[0] == NT
    return wr.astype(bf16), wt.astype(bf16), ws.astype(bf16)


def _mixers(x, rel_bias, w_in_l, pos_k, pos_v, w1_k, w2_k, w1_v, w2_v, lq1, lk1, lq2, lk2, diff_g, lam_init):
    B, L, _ = x.shape
    bias = _bias_tiles(rel_bias)
    wr, wt, ws = _layout_w_in(w_in_l)
    hr, ht, hs = _project(x, wr, wt, ws)
    o_a = _dsa(hr, ht, hs, bias[0:HG])

    n = L // CMP_STRIDE
    half = CMP_STRIDE * HEAD_DIM
    xk = hr[:, :, R_BKVC:R_BKVC + HEAD_DIM].reshape(B, n, half)
    xv = hr[:, :, R_BKVC + HEAD_DIM:R_BKVC + 2 * HEAD_DIM].reshape(B, n, half)
    kc, vcT = _compress(
        xk, xv, pos_k.reshape(2, half), pos_v.reshape(2, half),
        w1_k.reshape(2, half, HEAD_DIM).astype(bf16), w1_v.reshape(2, half, HEAD_DIM).astype(bf16),
        jnp.concatenate([w2_k, w2_k], axis=1).astype(bf16), w2_v.T.astype(bf16))
    ovlT, expand = _nsa_tables(L)
    o_b = _nsa(hr, ht, hs, kc, vcT, ovlT, expand, bias[HG:2 * HG])
    o_c = _moba(hr, ht, bias[2 * HG:3 * HG])
    o_d = _diff(hr, ht, bias[3 * HG:4 * HG], lq1.reshape(1, -1), lk1.reshape(1, -1),
                lq2.reshape(1, -1), lk2.reshape(1, -1), jnp.full((1, 1), lam_init, f32),
                diff_g.reshape(-1, 1))
    return o_a, o_b, o_c, o_d


def kernel(x, mem, rel_bias, w_in, w_out, nsa_pos_k, nsa_pos_v, nsa_w1_k, nsa_w2_k, nsa_w1_v, nsa_w2_v, diff_lq1, diff_lk1, diff_lq2, diff_lk2, diff_g, ln1_g, ln1_b, xq, xk, xv, xo, ln2_g, ln2_b, mlp_w1, mlp_w2, ln3_g, ln3_b):
    B, L, D = x.shape
    for l in range(DEPTH):
        lam_init = 0.8 - 0.6 * math.exp(-0.3 * l)
        o_a, o_b, o_c, o_d = _mixers(
            x, rel_bias, w_in[l], nsa_pos_k[l], nsa_pos_v[l], nsa_w1_k[l], nsa_w2_k[l],
            nsa_w1_v[l], nsa_w2_v[l], diff_lq1[l], diff_lk1[l], diff_lq2[l], diff_lk2[l],
            diff_g[l], lam_init)
        flat = lambda t: t.reshape(B * L, -1)
        row = lambda t: t.reshape(1, D)
        x2 = _outproj(flat(o_a), flat(o_b), flat(o_c), flat(o_d), w_out[l].astype(bf16),
                      flat(x), row(ln1_g[l]), row(ln1_b[l]))
        kT, v = _memkv(mem, xk[l].T.astype(bf16), xv[l].astype(bf16))
        x3 = _cross(x2.reshape(B, L, D), xq[l].astype(bf16), kT, v, xo[l].astype(bf16),
                    row(ln2_g[l]), row(ln2_b[l]))
        x = _mlp(flat(x3), mlp_w1[l].astype(bf16), mlp_w2[l].astype(bf16),
                 row(ln3_g[l]), row(ln3_b[l])).reshape(B, L, D)
    return x
```

```python
import functools
import math

import jax
import jax.numpy as jnp
from jax import lax
from jax.experimental import pallas as pl
from jax.experimental.pallas import tpu as pltpu

f32 = jnp.float32
bf16 = jnp.bfloat16
i32 = jnp.int32
i16 = jnp.int16

D_MODEL = 1024
DEPTH = 4
HEAD_DIM = 64
HG = 4
GW = HG * HEAD_DIM
REL_BUCKETS = 32
REL_MAX_DIST = 128
IDX_HEADS = 8
IDX_DIM = 64
DSA_TOPK = 256
CMP_LEN = 32
CMP_STRIDE = 16
SLC_LEN = 64
SLC_TOPN = 16
WIN = 512
MOBA_BLOCK = 256
MOBA_TOPK = 3
DIFF_HALF = HEAD_DIM // 2
CROSS_HEADS = 4
CROSS_DIM = D_MODEL // CROSS_HEADS
D_FF = 4 * D_MODEL
ALPHA = (2 * DEPTH) ** 0.25

NEG = -1e30
FLOOR = -1e29
INT_MIN = -(2 ** 31)
HALF = 2 ** 15
LANES = 128
TQ = 256
TK = 256
VMEM_LIMIT = 56 * 1024 * 1024
LOG2E = math.log2(math.e)
MATMULS_AHEAD = 6

R_AK, R_CK, R_DK, R_BKS, R_BKW, R_AKI, R_BKVC = 0, 256, 512, 768, 896, 1024, 1152
NR = 1280
T_AQ, T_AV, T_AQI, T_BQ, T_CQ, T_CV, T_DQ, T_DV, T_BVS, T_BVW = (
    0, 256, 512, 1024, 1280, 1536, 1792, 2048, 2304, 2368)
NT = 2432
S_AW, S_BG, NS = 0, 8, 32


def _cparams(*sem):
    return pltpu.CompilerParams(dimension_semantics=sem, vmem_limit_bytes=VMEM_LIMIT)


def _rel_bucket(dist):
    n = jnp.maximum(dist, 0)
    max_exact = REL_BUCKETS // 2
    nf = jnp.maximum(n, max_exact).astype(f32)
    large = max_exact + (jnp.log(nf / max_exact) / math.log(REL_MAX_DIST / max_exact)
                         * (REL_BUCKETS - max_exact)).astype(i32)
    large = jnp.minimum(large, REL_BUCKETS - 1)
    return jnp.where(n < max_exact, n, large)


def _bias_tiles(rel_bias):
    assert 2 * TQ - TK + 1 >= REL_MAX_DIST
    d = jnp.arange(TQ, dtype=i32)[None, :] - jnp.arange(TK, dtype=i32)[:, None]
    tiles = jnp.stack([rel_bias[_rel_bucket(d + off)] for off in (0, TQ)])
    far = rel_bias[_rel_bucket(jnp.int32(2 * TQ))]
    return ((tiles - far) * LOG2E).transpose(3, 0, 1, 2).astype(f32)


def _row_iota(shape):
    return lax.broadcasted_iota(i32, shape, 0)


def _col_iota(shape):
    return lax.broadcasted_iota(i32, shape, 1)


def _flash_init(m_ref, l_ref, acc_ref):
    m_ref[...] = jnp.full(m_ref.shape, NEG, f32)
    l_ref[...] = jnp.zeros(l_ref.shape, f32)
    acc_ref[...] = jnp.zeros(acc_ref.shape, f32)


def _fold(x, op, rows):
    while x.shape[0] > rows:
        half = x.shape[0] // 2
        x = op(x[:half], x[half:])
    return x


def _flash_step(c, s, vT, m_ref, l_ref, acc_ref):
    m_prev = m_ref[c]
    m_new = jnp.maximum(m_prev, jnp.max(_fold(s, jnp.maximum, 8), axis=0, keepdims=True))
    m_use = jnp.maximum(m_new, FLOOR)
    alpha = jnp.exp2(jnp.maximum(m_prev, FLOOR) - m_use)
    p = jnp.exp2(s - m_use)
    l_ref[c] = alpha * l_ref[c] + jnp.sum(_fold(p, jnp.add, 8), axis=0, keepdims=True)
    acc_ref[c] = alpha * acc_ref[c] + jnp.dot(vT, p.astype(bf16), preferred_element_type=f32)
    m_ref[c] = m_new


def _flash_result(c, l_ref, acc_ref):
    l = l_ref[c]
    return jnp.where(l > 0.0, acc_ref[c] / jnp.where(l > 0.0, l, 1.0), 0.0)


def _attend(qi, n, raw, finish, vT, m_ref, l_ref, acc_ref, first=0, shared=None):
    def run(chunks):
        items = [(kc, kind, c) for kc, kind in chunks for c in range(n)]
        ctx, logits = {}, {}

        def issue(j):
            kc, kind, c = items[j]
            if c == 0 and shared is not None:
                ctx[j // n] = shared(kc, kind)
            logits[j] = raw(kc, c)

        for j in range(min(MATMULS_AHEAD, len(items))):
            issue(j)
        for j, (kc, kind, c) in enumerate(items):
            if j + MATMULS_AHEAD < len(items):
                issue(j + MATMULS_AHEAD)
            s = finish(kind, c, logits.pop(j), ctx.get(j // n))
            _flash_step(c, s, vT(kc, c), m_ref, l_ref, acc_ref)

    _flash_init(m_ref, l_ref, acc_ref)
    n_far = jnp.maximum(qi - 1 - first, 0)

    def far_pair(j, carry):
        run([(first + 2 * j, 2), (first + 2 * j + 1, 2)])
        return carry

    lax.fori_loop(0, n_far // 2, far_pair, 0)

    @pl.when(n_far % 2 == 1)
    def _():
        run([(qi - 2, 2)])

    @pl.when(qi >= 1)
    def _():
        run([(qi - 1, 1), (qi, 0)])

    @pl.when(qi == 0)
    def _():
        run([(qi, 0)])


def _pair_rows(qT_ref, h, width=HEAD_DIM, offset=0):
    blk = qT_ref[(h // 2) * LANES:(h // 2 + 1) * LANES, :]
    r = _row_iota((LANES, 1))
    lo = (h % 2) * HEAD_DIM + offset
    return jnp.where((r >= lo) & (r < lo + width), blk, jnp.zeros_like(blk))


def _layer_norm(y, g, b):
    mu = jnp.mean(y, axis=-1, keepdims=True)
    yc = y - mu
    var = jnp.mean(yc * yc, axis=-1, keepdims=True)
    return yc * lax.rsqrt(var + 1e-5) * g + b


def _proj_kernel(x_ref, wr_ref, wt_ref, ws_ref, hr_ref, ht_ref, hs_ref):
    xb = x_ref[...].astype(bf16)
    nt = (((1,), (1,)), ((), ()))
    hr_ref[...] = jnp.dot(xb, wr_ref[...], preferred_element_type=f32).astype(bf16)
    ht_ref[...] = lax.dot_general(wt_ref[...], xb, nt, preferred_element_type=f32).astype(bf16)
    hs_ref[...] = lax.dot_general(ws_ref[...], xb, nt, preferred_element_type=f32)


def _project(x, wr, wt, ws):
    B, L, D = x.shape
    nch = L // TK
    return pl.pallas_call(
        _proj_kernel,
        grid=(B, nch),
        in_specs=[
            pl.BlockSpec((None, TK, D), lambda b, i: (b, i, 0)),
            pl.BlockSpec((D, NR), lambda b, i: (0, 0)),
            pl.BlockSpec((NT, D), lambda b, i: (0, 0)),
            pl.BlockSpec((NS, D), lambda b, i: (0, 0)),
        ],
        out_specs=[
            pl.BlockSpec((None, TK, NR), lambda b, i: (b, i, 0)),
            pl.BlockSpec((None, None, NT, TK), lambda b, i: (b, i, 0, 0)),
            pl.BlockSpec((None, None, NS, TK), lambda b, i: (b, i, 0, 0)),
        ],
        out_shape=[
            jax.ShapeDtypeStruct((B, L, NR), bf16),
            jax.ShapeDtypeStruct((B, nch, NT, TK), bf16),
            jax.ShapeDtypeStruct((B, nch, NS, TK), f32),
        ],
        compiler_params=_cparams("parallel", "parallel"),
        name="proj",
    )(x, wr, wt, ws)


def _q_spec(off, rows=GW):
    return pl.BlockSpec((None, None, rows, TQ), lambda b, i: (b, i, off // rows, 0))


def _vT_spec(nch, off, rows=GW):
    return pl.BlockSpec((None, nch, rows, TK), lambda b, i: (b, 0, off // rows, 0))


def _k_spec(L, off, cols=GW):
    return pl.BlockSpec((None, L, cols), lambda b, i: (b, 0, off // cols))


_BIAS_SPEC = pl.BlockSpec((HG, 2, TK, TQ), lambda b, i: (0, 0, 0, 0))
_OUT_SPEC = pl.BlockSpec((None, TQ, GW), lambda b, i: (b, i, 0))


def _flash_scratch(chains, dv=HEAD_DIM):
    return [pltpu.VMEM((chains, 1, TQ), f32), pltpu.VMEM((chains, 1, TQ), f32),
            pltpu.VMEM((chains, dv, TQ), f32)]


def _k_chunk(k_ref, kc, pair=None):
    rows = pl.ds(pl.multiple_of(kc * TK, TK), TK)
    if pair is None:
        return k_ref[rows, :]
    return k_ref[rows, pair * LANES:(pair + 1) * LANES]


def _dsa_kernel(qT_ref, qiT_ref, wT_ref, k_ref, ki_ref, vT_ref, bias_ref, o_ref,
                key_ref, hi_ref, lo_ref, m_ref, l_ref, acc_ref, *, topk):
    qi = pl.program_id(1)
    nch = qi + 1
    qpos = qi * TQ + _col_iota((1, TQ))
    krow = _row_iota((TK, 1))
    wT = wT_ref[...] * (IDX_DIM ** -0.5 * IDX_HEADS ** -0.5)
    qidx = [_pair_rows(qiT_ref, h) for h in range(IDX_HEADS)]

    def score_chunk(kc, carry):
        ki2 = _k_chunk(ki_ref, kc)
        s = jnp.zeros((TK, TQ), f32)
        for h in range(IDX_HEADS):
            r = jnp.dot(ki2, qidx[h], preferred_element_type=f32)
            s = s + wT[S_AW + h:S_AW + h + 1, :] * jnp.maximum(r, 0.0)
        s = jnp.where(s == 0.0, 0.0, s)
        bits = lax.bitcast_convert_type(s, i32)
        key = bits ^ ((bits >> 31) & 0x7FFFFFFF)
        key = jnp.where(kc * TK + krow <= qpos, key, INT_MIN)
        key_ref[kc] = key
        hi_ref[kc] = (key >> 16).astype(i16)
        lo_ref[kc] = ((key & 0xFFFF) - HALF).astype(i16)
        return carry

    lax.fori_loop(0, nch, score_chunk, 0)

    def count(ref, pred):
        one = jnp.ones((), ref.dtype)
        zero = jnp.zeros((), ref.dtype)
        rows = 8 * 4 // ref.dtype.itemsize

        def body(kc, acc):
            hit = jnp.where(pred(ref[kc]), one, zero)
            return acc + _fold(hit, jnp.add, rows).astype(f32)
        acc = lax.fori_loop(0, nch, body, jnp.zeros((rows, TQ), f32))
        return jnp.sum(acc, axis=0, keepdims=True)

    def kth_largest_i16(ref, want):
        def bisect(i, t_u):
            cand_u = t_u | jnp.left_shift(jnp.int32(1), 15 - i)
            cand = (cand_u - HALF).astype(i16)
            return jnp.where(count(ref, lambda k: k >= cand) >= want, cand_u, t_u)
        return lax.fori_loop(0, 16, bisect, jnp.zeros((1, TQ), i32)) - HALF

    hi_t = kth_largest_i16(hi_ref, topk)
    hi_t16 = hi_t.astype(i16)
    above = count(hi_ref, lambda k: k > hi_t16)

    def keep_low(kc, carry):
        lo_ref[kc] = jnp.where(hi_ref[kc] == hi_t16, lo_ref[kc], jnp.full((), -HALF, i16))
        return carry

    lax.fori_loop(0, nch, keep_low, 0)
    lo_t = kth_largest_i16(lo_ref, topk - above)
    thr = jnp.maximum((hi_t << 16) | (lo_t + HALF), INT_MIN + 1)
    cnt_ge = count(key_ref, lambda k: k >= thr)

    @pl.when(jnp.max(cnt_ge) > topk)
    def _():
        need = topk - (cnt_ge - count(key_ref, lambda k: k == thr))
        tri = (_col_iota((TK, TK)) <= _row_iota((TK, TK))).astype(bf16)

        def body(kc, seen):
            k = key_ref[kc]
            tie = k == thr
            tief = jnp.where(tie, 1.0, 0.0)
            pref = jnp.dot(tri, tief.astype(bf16), preferred_element_type=f32) + seen
            key_ref[kc] = jnp.where(tie & (pref > need), INT_MIN, k)
            return seen + jnp.sum(tief, axis=0, keepdims=True)

        lax.fori_loop(0, nch, body, jnp.zeros((1, TQ), f32))

    qs = [_pair_rows(qT_ref, h) for h in range(HG)]

    def finish(kind, h, s, keep):
        s = s * (HEAD_DIM ** -0.5 * LOG2E)
        if kind < 2:
            s = s + bias_ref[h, kind]
        return jnp.where(keep, s, NEG)

    _attend(qi, HG, lambda kc, h: jnp.dot(_k_chunk(k_ref, kc, h // 2), qs[h], preferred_element_type=f32),
            finish, lambda kc, h: vT_ref[kc, h * HEAD_DIM:(h + 1) * HEAD_DIM, :],
            m_ref, l_ref, acc_ref, shared=lambda kc, kind: key_ref[kc] >= thr)
    outT = jnp.concatenate([_flash_result(h, l_ref, acc_ref) for h in range(HG)], axis=0)
    o_ref[...] = outT.T.astype(o_ref.dtype)


def _dsa(hr, ht, hs, bias):
    B, L, _ = hr.shape
    nch = L // TK
    topk = min(DSA_TOPK, L // 4)
    return pl.pallas_call(
        functools.partial(_dsa_kernel, topk=topk),
        grid=(B, L // TQ),
        in_specs=[
            _q_spec(T_AQ), _q_spec(T_AQI, 2 * GW), _q_spec(0, NS),
            _k_spec(L, R_AK), _k_spec(L, R_AKI, LANES), _vT_spec(nch, T_AV), _BIAS_SPEC,
        ],
        out_specs=_OUT_SPEC,
        out_shape=jax.ShapeDtypeStruct((B, L, GW), bf16),
        scratch_shapes=[pltpu.VMEM((nch, TK, TQ), i32), pltpu.VMEM((nch, TK, TQ), i16),
                        pltpu.VMEM((nch, TK, TQ), i16)] + _flash_scratch(HG),
        compiler_params=_cparams("parallel", "arbitrary"),
        name="dsa",
    )(ht, ht, hs, hr, hr, ht, bias)


def _moba_kernel(qT_ref, k_ref, vT_ref, bias_ref, o_ref, km_ref, sel_ref, m_ref, l_ref, acc_ref, *, nch, topk):
    qi = pl.program_id(1)
    nb = km_ref.shape[0]

    @pl.when(qi == 0)
    def _():
        km_ref[...] = jnp.zeros(km_ref.shape, f32)
        for n in range(nch):
            blk = k_ref[n * MOBA_BLOCK:(n + 1) * MOBA_BLOCK, :].astype(f32)
            km_ref[n:n + 1, :] = jnp.sum(blk, axis=0, keepdims=True) * (1.0 / MOBA_BLOCK)

    km = km_ref[...]
    km_hi = km.astype(bf16)
    km_lo = (km - km_hi.astype(f32)).astype(bf16)
    blk_id = _row_iota((nb, 1))
    blk_f = blk_id.astype(f32)
    for h in range(HG):
        qh = qT_ref[...]
        r = _row_iota((GW, 1))
        qh = jnp.where((r >= h * HEAD_DIM) & (r < (h + 1) * HEAD_DIM), qh, jnp.zeros_like(qh))
        gate = (jnp.dot(km_hi, qh, preferred_element_type=f32)
                + jnp.dot(km_lo, qh, preferred_element_type=f32))
        gate = jnp.where(blk_id < qi, gate, -jnp.inf)
        sel = jnp.zeros((nb, TQ), f32)
        for _ in range(topk):
            best = jnp.max(gate, axis=0, keepdims=True)
            first = jnp.min(jnp.where(gate == best, blk_f, float(nb)), axis=0, keepdims=True)
            hit = blk_f == first
            sel = jnp.where(hit & (best > -jnp.inf), 1.0, sel)
            gate = jnp.where(hit, -jnp.inf, gate)
        sel_ref[h] = sel

    qs = [_pair_rows(qT_ref, h) for h in range(HG)]
    causal = _row_iota((TK, 1)) <= _col_iota((1, TQ))

    def finish(kind, h, s, kc):
        s = s * (HEAD_DIM ** -0.5 * LOG2E)
        if kind < 2:
            s = s + bias_ref[h, kind]
        if kind == 0:
            return jnp.where(causal, s, NEG)
        return jnp.where(sel_ref[h, pl.ds(kc, 1), :] > 0.5, s, NEG)

    _attend(qi, HG, lambda kc, h: jnp.dot(_k_chunk(k_ref, kc, h // 2), qs[h], preferred_element_type=f32),
            finish, lambda kc, h: vT_ref[kc, h * HEAD_DIM:(h + 1) * HEAD_DIM, :],
            m_ref, l_ref, acc_ref, shared=lambda kc, kind: kc)
    outT = jnp.concatenate([_flash_result(h, l_ref, acc_ref) for h in range(HG)], axis=0)
    o_ref[...] = outT.T.astype(o_ref.dtype)


def _moba(hr, ht, bias):
    B, L, _ = hr.shape
    nch = L // TK
    nb = -(-nch // 8) * 8
    topk = min(MOBA_TOPK, nch - 1)
    return pl.pallas_call(
        functools.partial(_moba_kernel, nch=nch, topk=topk),
        grid=(B, L // TQ),
        in_specs=[_q_spec(T_CQ), _k_spec(L, R_CK), _vT_spec(nch, T_CV), _BIAS_SPEC],
        out_specs=_OUT_SPEC,
        out_shape=jax.ShapeDtypeStruct((B, L, GW), bf16),
        scratch_shapes=[pltpu.VMEM((nb, GW), f32), pltpu.VMEM((HG, nb, TQ), f32)] + _flash_scratch(HG),
        compiler_params=_cparams("parallel", "arbitrary"),
        name="moba",
    )(ht, hr, ht, bias)


def _diff_kernel(qT_ref, k_ref, vT_ref, bias_ref, lq1_ref, lk1_ref, lq2_ref, lk2_ref, li_ref, g_ref,
                 o_ref, m_ref, l_ref, acc_ref):
    qi = pl.program_id(1)
    lam_init = li_ref[...]
    lam = (jnp.exp(jnp.sum(lq1_ref[...] * lk1_ref[...], axis=1, keepdims=True))
           - jnp.exp(jnp.sum(lq2_ref[...] * lk2_ref[...], axis=1, keepdims=True)) + lam_init)
    qs = [_pair_rows(qT_ref, h, DIFF_HALF, c * DIFF_HALF) for h in range(HG) for c in range(2)]
    causal = _row_iota((TK, 1)) <= _col_iota((1, TQ))

    def finish(kind, c, s, _):
        s = s * (DIFF_HALF ** -0.5 * LOG2E)
        if kind < 2:
            s = s + bias_ref[c // 2, kind]
        return jnp.where(causal, s, NEG) if kind == 0 else s

    _attend(qi, 2 * HG, lambda kc, c: jnp.dot(_k_chunk(k_ref, kc, c // 4), qs[c], preferred_element_type=f32),
            finish, lambda kc, c: vT_ref[kc, (c // 2) * HEAD_DIM:(c // 2 + 1) * HEAD_DIM, :],
            m_ref, l_ref, acc_ref)
    heads = []
    for h in range(HG):
        o = _flash_result(2 * h, l_ref, acc_ref) - lam * _flash_result(2 * h + 1, l_ref, acc_ref)
        ms = jnp.mean(o * o, axis=0, keepdims=True)
        heads.append(o * lax.rsqrt(ms + 1e-6) * g_ref[...] * (1.0 - lam_init))
    o_ref[...] = jnp.concatenate(heads, axis=0).T.astype(o_ref.dtype)


def _diff(hr, ht, bias, lq1, lk1, lq2, lk2, lam_init, g_col):
    B, L, _ = hr.shape
    nch = L // TK
    vec = pl.BlockSpec((1, DIFF_HALF), lambda b, i: (0, 0))
    return pl.pallas_call(
        _diff_kernel,
        grid=(B, L // TQ),
        in_specs=[
            _q_spec(T_DQ), _k_spec(L, R_DK), _vT_spec(nch, T_DV), _BIAS_SPEC,
            vec, vec, vec, vec,
            pl.BlockSpec((1, 1), lambda b, i: (0, 0)),
            pl.BlockSpec((HEAD_DIM, 1), lambda b, i: (0, 0)),
        ],
        out_specs=_OUT_SPEC,
        out_shape=jax.ShapeDtypeStruct((B, L, GW), bf16),
        scratch_shapes=_flash_scratch(2 * HG),
        compiler_params=_cparams("parallel", "arbitrary"),
        name="diff",
    )(ht, hr, ht, bias, lq1, lk1, lq2, lk2, lam_init, g_col)


def _gelu_tanh(x):
    return 0.5 * x * (1.0 + jnp.tanh(math.sqrt(2.0 / math.pi) * (x + 0.044715 * (x * x * x))))


def _compress_kernel(xk_ref, xv_ref, pk_ref, pv_ref, w1k_ref, w1v_ref, w2k_ref, w2vT_ref, kc_ref, vcT_ref):
    def pre_act(x_ref, p_ref, w1_ref):
        x = x_ref[...].astype(f32)
        first = jnp.dot((x + p_ref[0:1, :]).astype(bf16), w1_ref[0], preferred_element_type=f32)
        second = jnp.dot((x + p_ref[1:2, :]).astype(bf16), w1_ref[1], preferred_element_type=f32)
        n = first.shape[0]
        return _gelu_tanh(first + pltpu.roll(second, n - 1, 0)).astype(bf16)

    gk = pre_act(xk_ref, pk_ref, w1k_ref)
    kc_ref[...] = jnp.dot(gk, w2k_ref[...], preferred_element_type=f32).astype(bf16)
    gv = pre_act(xv_ref, pv_ref, w1v_ref)
    vcT_ref[...] = lax.dot_general(w2vT_ref[...], gv, (((1,), (1,)), ((), ())),
                                   preferred_element_type=f32).astype(bf16)


def _compress(xk, xv, pk, pv, w1k, w1v, w2k, w2vT):
    B, n, W = xk.shape
    xspec = pl.BlockSpec((None, n, W), lambda b: (b, 0, 0))
    full = lambda a: pl.BlockSpec(a.shape, lambda b: (0,) * a.ndim)
    return pl.pallas_call(
        _compress_kernel,
        grid=(B,),
        in_specs=[xspec, xspec, full(pk), full(pv), full(w1k), full(w1v), full(w2k), full(w2vT)],
        out_specs=[
            pl.BlockSpec((None, n, LANES), lambda b: (b, 0, 0)),
            pl.BlockSpec((None, HEAD_DIM, n), lambda b: (b, 0, 0)),
        ],
        out_shape=[
            jax.ShapeDtypeStruct((B, n, LANES), bf16),
            jax.ShapeDtypeStruct((B, HEAD_DIM, n), bf16),
        ],
        compiler_params=_cparams("parallel"),
        name="nsa_compress",
    )(xk, xv, pk, pv, w1k, w1v, w2k, w2vT)


def _split3(x):
    hi = x.astype(bf16)
    r = x - hi.astype(f32)
    mid = r.astype(bf16)
    lo = (r - mid.astype(f32)).astype(bf16)
    return hi, mid, lo


def _nsa_kernel(qT_ref, gT_ref, kc_ref, vcT_ref, ks_ref, vsT_ref, kw_ref, vwT_ref, ovlT_ref, exp_ref,
                bias_ref, o_ref, m_ref, l_ref, acc_ref, *, n_slc, topn):
    qi = pl.program_id(1)
    ncmp = kc_ref.shape[0]
    qpos = qi * TQ + _col_iota((1, TQ))
    pad = jnp.zeros((LANES - HEAD_DIM, TQ), bf16)
    qs = [jnp.concatenate([qT_ref[h * HEAD_DIM:(h + 1) * HEAD_DIM, :], pad], axis=0)
          for h in range(HG)]

    cmp_ok = _row_iota((ncmp, 1)) * CMP_STRIDE + (CMP_LEN - 1) <= qpos
    o_cmp = []
    pc_sum = jnp.zeros((ncmp, TQ), f32)
    for h in range(HG):
        s = jnp.dot(kc_ref[...], qs[h], preferred_element_type=f32) * HEAD_DIM ** -0.5
        s = jnp.where(cmp_ok, s, NEG)
        e = jnp.exp(s - jnp.maximum(jnp.max(s, axis=0, keepdims=True), FLOOR))
        den = jnp.sum(e, axis=0, keepdims=True)
        pc = e / jnp.where(den > 0.0, den, 1.0)
        pc_sum = pc_sum + pc
        o_cmp.append(jnp.dot(vcT_ref[...], pc.astype(bf16), preferred_element_type=f32))
    imp = jnp.zeros((LANES, TQ), f32)
    for part in _split3(pc_sum):
        imp = imp + jnp.dot(ovlT_ref[...], part, preferred_element_type=f32)
    blk = _row_iota((LANES, 1))
    blk_f = blk.astype(f32)
    cur = qpos // SLC_LEN
    forced = (blk == 0) | (blk == cur) | (blk == cur - 1)
    imp = jnp.where(forced, jnp.inf, imp)
    imp = jnp.where((blk * SLC_LEN <= qpos) & (blk < n_slc), imp, -jnp.inf)

    def pick(_, st):
        imp, sel = st
        best = jnp.max(imp, axis=0, keepdims=True)
        first = jnp.min(jnp.where(imp == best, blk_f, float(LANES)), axis=0, keepdims=True)
        hit = blk_f == first
        return jnp.where(hit, -jnp.inf, imp), jnp.where(hit & (best > -jnp.inf), 1.0, sel)

    _, sel = lax.fori_loop(0, topn, pick, (imp, jnp.zeros((LANES, TQ), f32)))
    selb = sel.astype(bf16)

    krow = _row_iota((TK, 1))
    qcol = _col_iota((1, TQ))
    causal = krow <= qcol

    def slc_keep(kc, kind):
        keep = jnp.dot(exp_ref[kc], selb, preferred_element_type=f32) > 0.5
        return keep & causal if kind == 0 else keep

    def slc_finish(kind, h, s, keep):
        s = s * (HEAD_DIM ** -0.5 * LOG2E)
        if kind < 2:
            s = s + bias_ref[h, kind]
        return jnp.where(keep, s, NEG)

    _attend(qi, HG, lambda kc, h: jnp.dot(_k_chunk(ks_ref, kc), qs[h], preferred_element_type=f32),
            slc_finish, lambda kc, h: vsT_ref[kc], m_ref, l_ref, acc_ref, shared=slc_keep)
    o_slc = [_flash_result(h, l_ref, acc_ref) for h in range(HG)]

    def win_finish(kind, h, s, _):
        s = s * (HEAD_DIM ** -0.5 * LOG2E)
        if kind < 2:
            s = s + bias_ref[h, kind]
        if kind == 0:
            return jnp.where(causal, s, NEG)
        return jnp.where(krow > qcol, s, NEG) if kind == 2 else s

    _attend(qi, HG, lambda kc, h: jnp.dot(_k_chunk(kw_ref, kc), qs[h], preferred_element_type=f32),
            win_finish, lambda kc, h: vwT_ref[kc], m_ref, l_ref, acc_ref,
            first=jnp.maximum(qi - WIN // TK, 0))
    o_win = [_flash_result(h, l_ref, acc_ref) for h in range(HG)]

    gates = jax.nn.sigmoid(gT_ref[...])
    heads = []
    for h in range(HG):
        c = S_BG + 3 * h
        heads.append(gates[c:c + 1, :] * o_cmp[h] + gates[c + 1:c + 2, :] * o_slc[h]
                     + gates[c + 2:c + 3, :] * o_win[h])
    o_ref[...] = jnp.concatenate(heads, axis=0).T.astype(o_ref.dtype)


def _nsa(hr, ht, hs, kc, vcT, ovlT, expand, bias):
    B, L, _ = hr.shape
    nch = L // TK
    n = kc.shape[1]
    n_slc = L // SLC_LEN
    topn = min(SLC_TOPN, n_slc)
    return pl.pallas_call(
        functools.partial(_nsa_kernel, n_slc=n_slc, topn=topn),
        grid=(B, L // TQ),
        in_specs=[
            _q_spec(T_BQ), _q_spec(0, NS),
            pl.BlockSpec((None, n, LANES), lambda b, i: (b, 0, 0)),
            pl.BlockSpec((None, HEAD_DIM, n), lambda b, i: (b, 0, 0)),
            _k_spec(L, R_BKS, LANES), _vT_spec(nch, T_BVS, HEAD_DIM),
            _k_spec(L, R_BKW, LANES), _vT_spec(nch, T_BVW, HEAD_DIM),
            pl.BlockSpec(ovlT.shape, lambda b, i: (0, 0)),
            pl.BlockSpec(expand.shape, lambda b, i: (0, 0, 0)),
            _BIAS_SPEC,
        ],
        out_specs=_OUT_SPEC,
        out_shape=jax.ShapeDtypeStruct((B, L, GW), bf16),
        scratch_shapes=_flash_scratch(HG),
        compiler_params=_cparams("parallel", "arbitrary"),
        name="nsa",
    )(ht, hs, kc, vcT, hr, ht, hr, ht, ovlT, expand, bias)


def _nsa_tables(L):
    n = L // CMP_STRIDE
    n_slc = L // SLC_LEN
    c0 = jnp.arange(n, dtype=i32)[None, :] * CMP_STRIDE
    s0 = jnp.arange(LANES, dtype=i32)[:, None] * SLC_LEN
    ovlT = (c0 <= s0 + SLC_LEN - 1) & (c0 + CMP_LEN - 1 >= s0)
    ovlT = ovlT & (jnp.arange(n)[None, :] < n - 1) & (jnp.arange(LANES)[:, None] < n_slc)
    tokblk = jnp.arange(L, dtype=i32) // SLC_LEN
    expand = tokblk[:, None] == jnp.arange(LANES, dtype=i32)[None, :]
    return ovlT.astype(bf16), expand.reshape(L // TK, TK, LANES).astype(bf16)


def _outproj_kernel(oa_ref, ob_ref, oc_ref, od_ref, w_ref, x_ref, g_ref, b_ref, y_ref):
    mix = jnp.zeros(x_ref.shape, f32)
    for n, o_ref in enumerate((oa_ref, ob_ref, oc_ref, od_ref)):
        mix = mix + jnp.dot(o_ref[...], w_ref[n * GW:(n + 1) * GW, :], preferred_element_type=f32)
    y_ref[...] = _layer_norm(ALPHA * x_ref[...] + mix, g_ref[...], b_ref[...])


def _outproj(oa, ob, oc, od, w, x, g, b, tm=512):
    M, D = x.shape
    ospec = pl.BlockSpec((tm, GW), lambda i: (i, 0))
    vec = pl.BlockSpec((1, D), lambda i: (0, 0))
    return pl.pallas_call(
        _outproj_kernel,
        grid=(M // tm,),
        in_specs=[ospec, ospec, ospec, ospec, pl.BlockSpec((D, D), lambda i: (0, 0)),
                  pl.BlockSpec((tm, D), lambda i: (i, 0)), vec, vec],
        out_specs=pl.BlockSpec((tm, D), lambda i: (i, 0)),
        out_shape=jax.ShapeDtypeStruct((M, D), f32),
        compiler_params=_cparams("parallel"),
        name="outproj_ln",
    )(oa, ob, oc, od, w, x, g, b)


def _memkv_kernel(mem_ref, wkT_ref, wv_ref, kT_ref, v_ref):
    mb = mem_ref[...].astype(bf16)
    kT_ref[...] = lax.dot_general(wkT_ref[...], mb, (((1,), (1,)), ((), ())),
                                  preferred_element_type=f32).astype(bf16)
    v_ref[...] = jnp.dot(mb, wv_ref[...], preferred_element_type=f32).astype(bf16)


def _memkv(mem, wkT, wv):
    B, N, D = mem.shape
    wspec = pl.BlockSpec((D, D), lambda b: (0, 0))
    return pl.pallas_call(
        _memkv_kernel,
        grid=(B,),
        in_specs=[pl.BlockSpec((None, N, D), lambda b: (b, 0, 0)), wspec, wspec],
        out_specs=[pl.BlockSpec((None, D, N), lambda b: (b, 0, 0)),
                   pl.BlockSpec((None, N, D), lambda b: (b, 0, 0))],
        out_shape=[jax.ShapeDtypeStruct((B, D, N), bf16), jax.ShapeDtypeStruct((B, N, D), bf16)],
        compiler_params=_cparams("parallel"),
        name="cross_kv",
    )(mem, wkT, wv)


def _cross_kernel(x_ref, wq_ref, kT_ref, v_ref, wo_ref, g_ref, b_ref, y_ref):
    x = x_ref[...]
    q = jnp.dot(x.astype(bf16), wq_ref[...], preferred_element_type=f32).astype(bf16)
    outs = []
    for h in range(CROSS_HEADS):
        sl = slice(h * CROSS_DIM, (h + 1) * CROSS_DIM)
        s = jnp.dot(q[:, sl], kT_ref[sl, :], preferred_element_type=f32) * CROSS_DIM ** -0.5
        e = jnp.exp(s - jnp.max(s, axis=1, keepdims=True))
        p = e / jnp.sum(e, axis=1, keepdims=True)
        outs.append(jnp.dot(p.astype(bf16), v_ref[:, sl], preferred_element_type=f32).astype(bf16))
    o = jnp.concatenate(outs, axis=1)
    y = ALPHA * x + jnp.dot(o, wo_ref[...], preferred_element_type=f32)
    y_ref[...] = _layer_norm(y, g_ref[...], b_ref[...])


def _cross(x, wq, kT, v, wo, g, b, tm=512):
    B, L, D = x.shape
    N = v.shape[1]
    wspec = pl.BlockSpec((D, D), lambda bb, i: (0, 0))
    vec = pl.BlockSpec((1, D), lambda bb, i: (0, 0))
    return pl.pallas_call(
        _cross_kernel,
        grid=(B, L // tm),
        in_specs=[pl.BlockSpec((None, tm, D), lambda bb, i: (bb, i, 0)), wspec,
                  pl.BlockSpec((None, D, N), lambda bb, i: (bb, 0, 0)),
                  pl.BlockSpec((None, N, D), lambda bb, i: (bb, 0, 0)), wspec, vec, vec],
        out_specs=pl.BlockSpec((None, tm, D), lambda bb, i: (bb, i, 0)),
        out_shape=jax.ShapeDtypeStruct((B, L, D), f32),
        compiler_params=_cparams("parallel", "parallel"),
        name="cross_ln",
    )(x, wq, kT, v, wo, g, b)


def _mlp_kernel(x_ref, w1_ref, w2_ref, g_ref, b_ref, y_ref, acc_ref):
    j = pl.program_id(1)

    @pl.when(j == 0)
    def _():
        acc_ref[...] = jnp.zeros(acc_ref.shape, f32)

    hdn = jnp.dot(x_ref[...].astype(bf16), w1_ref[...], preferred_element_type=f32)
    hdn = jnp.square(jnp.maximum(hdn, 0.0)).astype(bf16)
    acc_ref[...] += jnp.dot(hdn, w2_ref[...], preferred_element_type=f32)

    @pl.when(j == pl.num_programs(1) - 1)
    def _():
        y_ref[...] = _layer_norm(ALPHA * x_ref[...] + acc_ref[...], g_ref[...], b_ref[...])


def _mlp(x, w1, w2, g, b, tm=512, tf=1024):
    M, D = x.shape
    F = w1.shape[1]
    vec = pl.BlockSpec((1, D), lambda i, j: (0, 0))
    return pl.pallas_call(
        _mlp_kernel,
        grid=(M // tm, F // tf),
        in_specs=[pl.BlockSpec((tm, D), lambda i, j: (i, 0)),
                  pl.BlockSpec((D, tf), lambda i, j: (0, j)),
                  pl.BlockSpec((tf, D), lambda i, j: (j, 0)), vec, vec],
        out_specs=pl.BlockSpec((tm, D), lambda i, j: (i, 0)),
        out_shape=jax.ShapeDtypeStruct((M, D), f32),
        scratch_shapes=[pltpu.VMEM((tm, D), f32)],
        compiler_params=_cparams("parallel", "arbitrary"),
        name="mlp_ln",
    )(x, w1, w2, g, b)


def _split_w_in(w):
    sizes = (GW, GW, GW, IDX_HEADS * IDX_DIM, IDX_DIM, IDX_HEADS,
             GW, HEAD_DIM, HEAD_DIM, HEAD_DIM, HEAD_DIM, HEAD_DIM, HEAD_DIM, 3 * HG,
             GW, GW, GW, GW, GW, GW)
    offs = [0]
    for s in sizes:
        offs.append(offs[-1] + s)
    return [w[:, offs[n]:offs[n + 1]] for n in range(len(sizes))]


def _layout_w_in(w):
    (a_q, a_k, a_v, a_qi, a_ki, a_w, b_q, b_kc, b_vc, b_ks, b_vs, b_kw, b_vw, b_g,
     c_q, c_k, c_v, d_q, d_k, d_v) = _split_w_in(w)
    twice = lambda t: jnp.concatenate([t, t], axis=1)
    wr = jnp.concatenate([a_k, c_k, d_k, twice(b_ks), twice(b_kw), twice(a_ki), b_kc, b_vc], axis=1)
    wt = jnp.concatenate([a_q, a_v, a_qi, b_q, c_q, c_v, d_q, d_v, b_vs, b_vw], axis=1).T
    ws = jnp.concatenate([a_w, b_g, jnp.zeros((w.shape[0], NS - IDX_HEADS - 3 * HG), w.dtype)], axis=1).T
    assert wr.shape[1] == NR and wt.shape[0] == NT
    return wr.astype(bf16), wt.astype(bf16), ws.astype(bf16)


def _mixers(x, rel_bias, w_in_l, pos_k, pos_v, w1_k, w2_k, w1_v, w2_v, lq1, lk1, lq2, lk2, diff_g, lam_init):
    B, L, _ = x.shape
    bias = _bias_tiles(rel_bias)
    wr, wt, ws = _layout_w_in(w_in_l)
    hr, ht, hs = _project(x, wr, wt, ws)
    o_a = _dsa(hr, ht, hs, bias[0:HG])

    n = L // CMP_STRIDE
    half = CMP_STRIDE * HEAD_DIM
    xk = hr[:, :, R_BKVC:R_BKVC + HEAD_DIM].reshape(B, n, half)
    xv = hr[:, :, R_BKVC + HEAD_DIM:R_BKVC + 2 * HEAD_DIM].reshape(B, n, half)
    kc, vcT = _compress(
        xk, xv, pos_k.reshape(2, half), pos_v.reshape(2, half),
        w1_k.reshape(2, half, HEAD_DIM).astype(bf16), w1_v.reshape(2, half, HEAD_DIM).astype(bf16),
        jnp.concatenate([w2_k, w2_k], axis=1).astype(bf16), w2_v.T.astype(bf16))
    ovlT, expand = _nsa_tables(L)
    o_b = _nsa(hr, ht, hs, kc, vcT, ovlT, expand, bias[HG:2 * HG])
    o_c = _moba(hr, ht, bias[2 * HG:3 * HG])
    o_d = _diff(hr, ht, bias[3 * HG:4 * HG], lq1.reshape(1, -1), lk1.reshape(1, -1),
                lq2.reshape(1, -1), lk2.reshape(1, -1), jnp.full((1, 1), lam_init, f32),
                diff_g.reshape(-1, 1))
    return o_a, o_b, o_c, o_d


def kernel(x, mem, rel_bias, w_in, w_out, nsa_pos_k, nsa_pos_v, nsa_w1_k, nsa_w2_k, nsa_w1_v, nsa_w2_v, diff_lq1, diff_lk1, diff_lq2, diff_lk2, diff_g, ln1_g, ln1_b, xq, xk, xv, xo, ln2_g, ln2_b, mlp_w1, mlp_w2, ln3_g, ln3_b):
    B, L, D = x.shape
    for l in range(DEPTH):
        lam_init = 0.8 - 0.6 * math.exp(-0.3 * l)
        o_a, o_b, o_c, o_d = _mixers(
            x, rel_bias, w_in[l], nsa_pos_k[l], nsa_pos_v[l], nsa_w1_k[l], nsa_w2_k[l],
            nsa_w1_v[l], nsa_w2_v[l], diff_lq1[l], diff_lk1[l], diff_lq2[l], diff_lk2[l],
            diff_g[l], lam_init)
        flat = lambda t: t.reshape(B * L, -1)
        row = lambda t: t.reshape(1, D)
        x2 = _outproj(flat(o_a), flat(o_b), flat(o_c), flat(o_d), w_out[l].astype(bf16),
                      flat(x), row(ln1_g[l]), row(ln1_b[l]))
        kT, v = _memkv(mem, xk[l].T.astype(bf16), xv[l].astype(bf16))
        x3 = _cross(x2.reshape(B, L, D), xq[l].astype(bf16), kT, v, xo[l].astype(bf16),
                    row(ln2_g[l]), row(ln2_b[l]))
        x = _mlp(flat(x3), mlp_w1[l].astype(bf16), mlp_w2[l].astype(bf16),
                 row(ln3_g[l]), row(ln3_b[l])).reshape(B, L, D)
    return x
```

```python
import functools
import math

import jax
import jax.numpy as jnp
from jax import lax
from jax.experimental import pallas as pl
from jax.experimental.pallas import tpu as pltpu

f32 = jnp.float32
bf16 = jnp.bfloat16
i32 = jnp.int32
i16 = jnp.int16

D_MODEL = 1024
DEPTH = 4
HEAD_DIM = 64
HG = 4
GW = HG * HEAD_DIM
REL_BUCKETS = 32
REL_MAX_DIST = 128
IDX_HEADS = 8
IDX_DIM = 64
DSA_TOPK = 256
CMP_LEN = 32
CMP_STRIDE = 16
SLC_LEN = 64
SLC_TOPN = 16
WIN = 512
MOBA_BLOCK = 256
MOBA_TOPK = 3
DIFF_HALF = HEAD_DIM // 2
CROSS_HEADS = 4
CROSS_DIM = D_MODEL // CROSS_HEADS
D_FF = 4 * D_MODEL
ALPHA = (2 * DEPTH) ** 0.25

NEG = -1e30
FLOOR = -1e29
INT_MIN = -(2 ** 31)
HALF = 2 ** 15
LANES = 128
TQ = 256
TK = 256
VMEM_LIMIT = 56 * 1024 * 1024
LOG2E = math.log2(math.e)
FAR_CHAINS = 16
MATMULS_AHEAD = 6

R_AK, R_CK, R_DK, R_BKS, R_BKW, R_AKI, R_BKVC = 0, 256, 512, 768, 896, 1024, 1152
NR = 1280
T_AQ, T_AV, T_AQI, T_BQ, T_CQ, T_CV, T_DQ, T_DV, T_BVS, T_BVW = (
    0, 256, 512, 1024, 1280, 1536, 1792, 2048, 2304, 2368)
NT = 2432
S_AW, S_BG, NS = 0, 8, 32


def _cparams(*sem):
    return pltpu.CompilerParams(dimension_semantics=sem, vmem_limit_bytes=VMEM_LIMIT)


def _rel_bucket(dist):
    n = jnp.maximum(dist, 0)
    max_exact = REL_BUCKETS // 2
    nf = jnp.maximum(n, max_exact).astype(f32)
    large = max_exact + (jnp.log(nf / max_exact) / math.log(REL_MAX_DIST / max_exact)
                         * (REL_BUCKETS - max_exact)).astype(i32)
    large = jnp.minimum(large, REL_BUCKETS - 1)
    return jnp.where(n < max_exact, n, large)


def _bias_tiles(rel_bias):
    assert 2 * TQ - TK + 1 >= REL_MAX_DIST
    d = jnp.arange(TQ, dtype=i32)[None, :] - jnp.arange(TK, dtype=i32)[:, None]
    bucket = jnp.stack([_rel_bucket(d + off) for off in (0, TQ)])
    far = rel_bias[_rel_bucket(jnp.int32(2 * TQ))]
    table = ((rel_bias - far) * LOG2E).astype(f32)
    onehot = (bucket[None] == jnp.arange(REL_BUCKETS, dtype=i32)[:, None, None, None]).astype(f32)
    return jnp.einsum('bntq,bh->hntq', onehot, table, precision=lax.Precision.HIGHEST)


def _row_iota(shape):
    return lax.broadcasted_iota(i32, shape, 0)


def _col_iota(shape):
    return lax.broadcasted_iota(i32, shape, 1)


def _flash_init(m_ref, l_ref, acc_ref):
    m_ref[...] = jnp.full(m_ref.shape, NEG, f32)
    l_ref[...] = jnp.zeros(l_ref.shape, f32)
    acc_ref[...] = jnp.zeros(acc_ref.shape, f32)


def _fold(x, op, rows):
    while x.shape[0] > rows:
        half = x.shape[0] // 2
        x = op(x[:half], x[half:])
    return x


def _flash_step(c, s, vT, m_ref, l_ref, acc_ref):
    m_prev = m_ref[c]
    m_new = jnp.maximum(m_prev, jnp.max(_fold(s, jnp.maximum, 8), axis=0, keepdims=True))
    m_use = jnp.maximum(m_new, FLOOR)
    alpha = jnp.exp2(jnp.maximum(m_prev, FLOOR) - m_use)
    p = jnp.exp2(s - m_use)
    l_ref[c] = alpha * l_ref[c] + jnp.sum(_fold(p, jnp.add, 8), axis=0, keepdims=True)
    acc_ref[c] = alpha * acc_ref[c] + jnp.dot(vT, p.astype(bf16), preferred_element_type=f32)
    m_ref[c] = m_new


def _flash_result(c, l_ref, acc_ref):
    l = l_ref[c]
    return jnp.where(l > 0.0, acc_ref[c] / jnp.where(l > 0.0, l, 1.0), 0.0)


def _attend(qi, n, raw, finish, vT, m_ref, l_ref, acc_ref, first=0, shared=None):
    def run(chunks):
        items = [(kc, kind, c) for kc, kind in chunks for c in range(n)]
        ctx, logits = {}, {}

        def issue(j):
            kc, kind, c = items[j]
            if c == 0 and shared is not None:
                ctx[j // n] = shared(kc, kind)
            logits[j] = raw(kc, c)

        for j in range(min(MATMULS_AHEAD, len(items))):
            issue(j)
        for j, (kc, kind, c) in enumerate(items):
            if j + MATMULS_AHEAD < len(items):
                issue(j + MATMULS_AHEAD)
            s = finish(kind, c, logits.pop(j), ctx.get(j // n))
            _flash_step(c, s, vT(kc, c), m_ref, l_ref, acc_ref)

    _flash_init(m_ref, l_ref, acc_ref)
    n_far = jnp.maximum(qi - 1 - first, 0)
    group = max(1, FAR_CHAINS // n)

    def far_group(j, carry):
        run([(first + group * j + t, 2) for t in range(group)])
        return carry

    lax.fori_loop(0, n_far // group, far_group, 0)
    size = group // 2
    while size >= 1:
        done = first + (n_far // (2 * size)) * (2 * size)

        @pl.when((n_far // size) % 2 == 1)
        def _(done=done, size=size):
            run([(done + t, 2) for t in range(size)])

        size //= 2

    @pl.when(qi >= 1)
    def _():
        run([(qi - 1, 1), (qi, 0)])

    @pl.when(qi == 0)
    def _():
        run([(qi, 0)])


def _pair_rows(qT_ref, h, width=HEAD_DIM, offset=0):
    blk = qT_ref[(h // 2) * LANES:(h // 2 + 1) * LANES, :]
    r = _row_iota((LANES, 1))
    lo = (h % 2) * HEAD_DIM + offset
    return jnp.where((r >= lo) & (r < lo + width), blk, jnp.zeros_like(blk))


def _layer_norm(y, g, b):
    mu = jnp.mean(y, axis=-1, keepdims=True)
    yc = y - mu
    var = jnp.mean(yc * yc, axis=-1, keepdims=True)
    return yc * lax.rsqrt(var + 1e-5) * g + b


def _proj_kernel(x_ref, wr_ref, wt_ref, ws_ref, hr_ref, ht_ref, hs_ref):
    xb = x_ref[...].astype(bf16)
    nt = (((1,), (1,)), ((), ()))
    hr_ref[...] = jnp.dot(xb, wr_ref[...], preferred_element_type=f32).astype(bf16)
    ht_ref[...] = lax.dot_general(wt_ref[...], xb, nt, preferred_element_type=f32).astype(bf16)
    hs_ref[...] = lax.dot_general(ws_ref[...], xb, nt, preferred_element_type=f32)


def _project(x, wr, wt, ws):
    B, L, D = x.shape
    nch = L // TK
    return pl.pallas_call(
        _proj_kernel,
        grid=(B, nch),
        in_specs=[
            pl.BlockSpec((None, TK, D), lambda b, i: (b, i, 0)),
            pl.BlockSpec((D, NR), lambda b, i: (0, 0)),
            pl.BlockSpec((NT, D), lambda b, i: (0, 0)),
            pl.BlockSpec((NS, D), lambda b, i: (0, 0)),
        ],
        out_specs=[
            pl.BlockSpec((None, TK, NR), lambda b, i: (b, i, 0)),
            pl.BlockSpec((None, None, NT, TK), lambda b, i: (b, i, 0, 0)),
            pl.BlockSpec((None, None, NS, TK), lambda b, i: (b, i, 0, 0)),
        ],
        out_shape=[
            jax.ShapeDtypeStruct((B, L, NR), bf16),
            jax.ShapeDtypeStruct((B, nch, NT, TK), bf16),
            jax.ShapeDtypeStruct((B, nch, NS, TK), f32),
        ],
        compiler_params=_cparams("parallel", "parallel"),
        name="proj",
    )(x, wr, wt, ws)


def _q_spec(off, rows=GW):
    return pl.BlockSpec((None, None, rows, TQ), lambda b, i: (b, i, off // rows, 0))


def _vT_spec(nch, off, rows=GW):
    return pl.BlockSpec((None, nch, rows, TK), lambda b, i: (b, 0, off // rows, 0))


def _k_spec(L, off, cols=GW):
    return pl.BlockSpec((None, L, cols), lambda b, i: (b, 0, off // cols))


_BIAS_SPEC = pl.BlockSpec((HG, 2, TK, TQ), lambda b, i: (0, 0, 0, 0))
_OUT_SPEC = pl.BlockSpec((None, TQ, GW), lambda b, i: (b, i, 0))


def _flash_scratch(chains, dv=HEAD_DIM):
    return [pltpu.VMEM((chains, 1, TQ), f32), pltpu.VMEM((chains, 1, TQ), f32),
            pltpu.VMEM((chains, dv, TQ), f32)]


def _k_chunk(k_ref, kc, pair=None):
    rows = pl.ds(pl.multiple_of(kc * TK, TK), TK)
    if pair is None:
        return k_ref[rows, :]
    return k_ref[rows, pair * LANES:(pair + 1) * LANES]


def _dsa_kernel(qT_ref, qiT_ref, wT_ref, k_ref, ki_ref, vT_ref, bias_ref, o_ref,
                key_ref, hi_ref, lo_ref, m_ref, l_ref, acc_ref, *, topk):
    qi = pl.program_id(1)
    nch = qi + 1
    qpos = qi * TQ + _col_iota((1, TQ))
    krow = _row_iota((TK, 1))
    wT = wT_ref[...] * (IDX_DIM ** -0.5 * IDX_HEADS ** -0.5)
    qidx = [_pair_rows(qiT_ref, h) for h in range(IDX_HEADS)]

    def score_chunk(kc, carry):
        ki2 = _k_chunk(ki_ref, kc)
        s = jnp.zeros((TK, TQ), f32)
        for h in range(IDX_HEADS):
            r = jnp.dot(ki2, qidx[h], preferred_element_type=f32)
            s = s + wT[S_AW + h:S_AW + h + 1, :] * jnp.maximum(r, 0.0)
        s = jnp.where(s == 0.0, 0.0, s)
        bits = lax.bitcast_convert_type(s, i32)
        key = bits ^ ((bits >> 31) & 0x7FFFFFFF)
        key = jnp.where(kc * TK + krow <= qpos, key, INT_MIN)
        key_ref[kc] = key
        hi_ref[kc] = (key >> 16).astype(i16)
        lo_ref[kc] = ((key & 0xFFFF) - HALF).astype(i16)
        return carry

    lax.fori_loop(0, nch, score_chunk, 0)

    def count(ref, pred):
        one = jnp.ones((), ref.dtype)
        zero = jnp.zeros((), ref.dtype)
        rows = 8 * 4 // ref.dtype.itemsize

        def body(kc, acc):
            hit = jnp.where(pred(ref[kc]), one, zero)
            return acc + _fold(hit, jnp.add, rows).astype(f32)
        acc = lax.fori_loop(0, nch, body, jnp.zeros((rows, TQ), f32))
        return jnp.sum(acc, axis=0, keepdims=True)

    def kth_largest_i16(ref, want):
        def bisect(i, t_u):
            cand_u = t_u | jnp.left_shift(jnp.int32(1), 15 - i)
            cand = (cand_u - HALF).astype(i16)
            return jnp.where(count(ref, lambda k: k >= cand) >= want, cand_u, t_u)
        return lax.fori_loop(0, 16, bisect, jnp.zeros((1, TQ), i32)) - HALF

    hi_t = kth_largest_i16(hi_ref, topk)
    hi_t16 = hi_t.astype(i16)
    above = count(hi_ref, lambda k: k > hi_t16)

    def keep_low(kc, carry):
        lo_ref[kc] = jnp.where(hi_ref[kc] == hi_t16, lo_ref[kc], jnp.full((), -HALF, i16))
        return carry

    lax.fori_loop(0, nch, keep_low, 0)
    lo_t = kth_largest_i16(lo_ref, topk - above)
    thr = jnp.maximum((hi_t << 16) | (lo_t + HALF), INT_MIN + 1)
    cnt_ge = count(key_ref, lambda k: k >= thr)

    @pl.when(jnp.max(cnt_ge) > topk)
    def _():
        need = topk - (cnt_ge - count(key_ref, lambda k: k == thr))
        tri = (_col_iota((TK, TK)) <= _row_iota((TK, TK))).astype(bf16)

        def body(kc, seen):
            k = key_ref[kc]
            tie = k == thr
            tief = jnp.where(tie, 1.0, 0.0)
            pref = jnp.dot(tri, tief.astype(bf16), preferred_element_type=f32) + seen
            key_ref[kc] = jnp.where(tie & (pref > need), INT_MIN, k)
            return seen + jnp.sum(tief, axis=0, keepdims=True)

        lax.fori_loop(0, nch, body, jnp.zeros((1, TQ), f32))

    qs = [_pair_rows(qT_ref, h) for h in range(HG)]

    def finish(kind, h, s, keep):
        s = s * (HEAD_DIM ** -0.5 * LOG2E)
        if kind < 2:
            s = s + bias_ref[h, kind]
        return jnp.where(keep, s, NEG)

    _attend(qi, HG, lambda kc, h: jnp.dot(_k_chunk(k_ref, kc, h // 2), qs[h], preferred_element_type=f32),
            finish, lambda kc, h: vT_ref[kc, h * HEAD_DIM:(h + 1) * HEAD_DIM, :],
            m_ref, l_ref, acc_ref, shared=lambda kc, kind: key_ref[kc] >= thr)
    outT = jnp.concatenate([_flash_result(h, l_ref, acc_ref) for h in range(HG)], axis=0)
    o_ref[...] = outT.T.astype(o_ref.dtype)


def _dsa(hr, ht, hs, bias):
    B, L, _ = hr.shape
    nch = L // TK
    topk = min(DSA_TOPK, L // 4)
    return pl.pallas_call(
        functools.partial(_dsa_kernel, topk=topk),
        grid=(B, L // TQ),
        in_specs=[
            _q_spec(T_AQ), _q_spec(T_AQI, 2 * GW), _q_spec(0, NS),
            _k_spec(L, R_AK), _k_spec(L, R_AKI, LANES), _vT_spec(nch, T_AV), _BIAS_SPEC,
        ],
        out_specs=_OUT_SPEC,
        out_shape=jax.ShapeDtypeStruct((B, L, GW), bf16),
        scratch_shapes=[pltpu.VMEM((nch, TK, TQ), i32), pltpu.VMEM((nch, TK, TQ), i16),
                        pltpu.VMEM((nch, TK, TQ), i16)] + _flash_scratch(HG),
        compiler_params=_cparams("parallel", "arbitrary"),
        name="dsa",
    )(ht, ht, hs, hr, hr, ht, bias)


def _moba_kernel(qT_ref, k_ref, vT_ref, bias_ref, o_ref, km_ref, sel_ref, m_ref, l_ref, acc_ref, *, nch, topk):
    qi = pl.program_id(1)
    nb = km_ref.shape[0]

    @pl.when(qi == 0)
    def _():
        km_ref[...] = jnp.zeros(km_ref.shape, f32)
        for n in range(nch):
            blk = k_ref[n * MOBA_BLOCK:(n + 1) * MOBA_BLOCK, :].astype(f32)
            km_ref[n:n + 1, :] = jnp.sum(blk, axis=0, keepdims=True) * (1.0 / MOBA_BLOCK)

    km = km_ref[...]
    km_hi = km.astype(bf16)
    km_lo = (km - km_hi.astype(f32)).astype(bf16)
    blk_id = _row_iota((nb, 1))
    blk_f = blk_id.astype(f32)
    for h in range(HG):
        qh = qT_ref[...]
        r = _row_iota((GW, 1))
        qh = jnp.where((r >= h * HEAD_DIM) & (r < (h + 1) * HEAD_DIM), qh, jnp.zeros_like(qh))
        gate = (jnp.dot(km_hi, qh, preferred_element_type=f32)
                + jnp.dot(km_lo, qh, preferred_element_type=f32))
        gate = jnp.where(blk_id < qi, gate, -jnp.inf)
        sel = jnp.zeros((nb, TQ), f32)
        for _ in range(topk):
            best = jnp.max(gate, axis=0, keepdims=True)
            first = jnp.min(jnp.where(gate == best, blk_f, float(nb)), axis=0, keepdims=True)
            hit = blk_f == first
            sel = jnp.where(hit & (best > -jnp.inf), 1.0, sel)
            gate = jnp.where(hit, -jnp.inf, gate)
        sel_ref[h] = sel

    qs = [_pair_rows(qT_ref, h) for h in range(HG)]
    causal = _row_iota((TK, 1)) <= _col_iota((1, TQ))

    def finish(kind, h, s, kc):
        s = s * (HEAD_DIM ** -0.5 * LOG2E)
        if kind < 2:
            s = s + bias_ref[h, kind]
        if kind == 0:
            return jnp.where(causal, s, NEG)
        return jnp.where(sel_ref[h, pl.ds(kc, 1), :] > 0.5, s, NEG)

    _attend(qi, HG, lambda kc, h: jnp.dot(_k_chunk(k_ref, kc, h // 2), qs[h], preferred_element_type=f32),
            finish, lambda kc, h: vT_ref[kc, h * HEAD_DIM:(h + 1) * HEAD_DIM, :],
            m_ref, l_ref, acc_ref, shared=lambda kc, kind: kc)
    outT = jnp.concatenate([_flash_result(h, l_ref, acc_ref) for h in range(HG)], axis=0)
    o_ref[...] = outT.T.astype(o_ref.dtype)


def _moba(hr, ht, bias):
    B, L, _ = hr.shape
    nch = L // TK
    nb = -(-nch // 8) * 8
    topk = min(MOBA_TOPK, nch - 1)
    return pl.pallas_call(
        functools.partial(_moba_kernel, nch=nch, topk=topk),
        grid=(B, L // TQ),
        in_specs=[_q_spec(T_CQ), _k_spec(L, R_CK), _vT_spec(nch, T_CV), _BIAS_SPEC],
        out_specs=_OUT_SPEC,
        out_shape=jax.ShapeDtypeStruct((B, L, GW), bf16),
        scratch_shapes=[pltpu.VMEM((nb, GW), f32), pltpu.VMEM((HG, nb, TQ), f32)] + _flash_scratch(HG),
        compiler_params=_cparams("parallel", "arbitrary"),
        name="moba",
    )(ht, hr, ht, bias)


def _diff_kernel(qT_ref, k_ref, vT_ref, bias_ref, lq1_ref, lk1_ref, lq2_ref, lk2_ref, li_ref, g_ref,
                 o_ref, m_ref, l_ref, acc_ref):
    qi = pl.program_id(1)
    lam_init = li_ref[...]
    lam = (jnp.exp(jnp.sum(lq1_ref[...] * lk1_ref[...], axis=1, keepdims=True))
           - jnp.exp(jnp.sum(lq2_ref[...] * lk2_ref[...], axis=1, keepdims=True)) + lam_init)
    qs = [_pair_rows(qT_ref, h, DIFF_HALF, c * DIFF_HALF) for h in range(HG) for c in range(2)]
    causal = _row_iota((TK, 1)) <= _col_iota((1, TQ))

    def finish(kind, c, s, _):
        s = s * (DIFF_HALF ** -0.5 * LOG2E)
        if kind < 2:
            s = s + bias_ref[c // 2, kind]
        return jnp.where(causal, s, NEG) if kind == 0 else s

    _attend(qi, 2 * HG, lambda kc, c: jnp.dot(_k_chunk(k_ref, kc, c // 4), qs[c], preferred_element_type=f32),
            finish, lambda kc, c: vT_ref[kc, (c // 2) * HEAD_DIM:(c // 2 + 1) * HEAD_DIM, :],
            m_ref, l_ref, acc_ref)
    heads = []
    for h in range(HG):
        o = _flash_result(2 * h, l_ref, acc_ref) - lam * _flash_result(2 * h + 1, l_ref, acc_ref)
        ms = jnp.mean(o * o, axis=0, keepdims=True)
        heads.append(o * lax.rsqrt(ms + 1e-6) * g_ref[...] * (1.0 - lam_init))
    o_ref[...] = jnp.concatenate(heads, axis=0).T.astype(o_ref.dtype)


def _diff(hr, ht, bias, lq1, lk1, lq2, lk2, lam_init, g_col):
    B, L, _ = hr.shape
    nch = L // TK
    vec = pl.BlockSpec((1, DIFF_HALF), lambda b, i: (0, 0))
    return pl.pallas_call(
        _diff_kernel,
        grid=(B, L // TQ),
        in_specs=[
            _q_spec(T_DQ), _k_spec(L, R_DK), _vT_spec(nch, T_DV), _BIAS_SPEC,
            vec, vec, vec, vec,
            pl.BlockSpec((1, 1), lambda b, i: (0, 0)),
            pl.BlockSpec((HEAD_DIM, 1), lambda b, i: (0, 0)),
        ],
        out_specs=_OUT_SPEC,
        out_shape=jax.ShapeDtypeStruct((B, L, GW), bf16),
        scratch_shapes=_flash_scratch(2 * HG),
        compiler_params=_cparams("parallel", "arbitrary"),
        name="diff",
    )(ht, hr, ht, bias, lq1, lk1, lq2, lk2, lam_init, g_col)


def _gelu_tanh(x):
    return 0.5 * x * (1.0 + jnp.tanh(math.sqrt(2.0 / math.pi) * (x + 0.044715 * (x * x * x))))


def _compress_kernel(xk_ref, xv_ref, pk_ref, pv_ref, w1k_ref, w1v_ref, w2k_ref, w2vT_ref, kc_ref, vcT_ref):
    def pre_act(x_ref, p_ref, w1_ref):
        x = x_ref[...].astype(f32)
        first = jnp.dot((x + p_ref[0:1, :]).astype(bf16), w1_ref[0], preferred_element_type=f32)
        second = jnp.dot((x + p_ref[1:2, :]).astype(bf16), w1_ref[1], preferred_element_type=f32)
        n = first.shape[0]
        return _gelu_tanh(first + pltpu.roll(second, n - 1, 0)).astype(bf16)

    gk = pre_act(xk_ref, pk_ref, w1k_ref)
    kc_ref[...] = jnp.dot(gk, w2k_ref[...], preferred_element_type=f32).astype(bf16)
    gv = pre_act(xv_ref, pv_ref, w1v_ref)
    vcT_ref[...] = lax.dot_general(w2vT_ref[...], gv, (((1,), (1,)), ((), ())),
                                   preferred_element_type=f32).astype(bf16)


def _compress(xk, xv, pk, pv, w1k, w1v, w2k, w2vT):
    B, n, W = xk.shape
    xspec = pl.BlockSpec((None, n, W), lambda b: (b, 0, 0))
    full = lambda a: pl.BlockSpec(a.shape, lambda b: (0,) * a.ndim)
    return pl.pallas_call(
        _compress_kernel,
        grid=(B,),
        in_specs=[xspec, xspec, full(pk), full(pv), full(w1k), full(w1v), full(w2k), full(w2vT)],
        out_specs=[
            pl.BlockSpec((None, n, LANES), lambda b: (b, 0, 0)),
            pl.BlockSpec((None, HEAD_DIM, n), lambda b: (b, 0, 0)),
        ],
        out_shape=[
            jax.ShapeDtypeStruct((B, n, LANES), bf16),
            jax.ShapeDtypeStruct((B, HEAD_DIM, n), bf16),
        ],
        compiler_params=_cparams("parallel"),
        name="nsa_compress",
    )(xk, xv, pk, pv, w1k, w1v, w2k, w2vT)


def _split3(x):
    hi = x.astype(bf16)
    r = x - hi.astype(f32)
    mid = r.astype(bf16)
    lo = (r - mid.astype(f32)).astype(bf16)
    return hi, mid, lo


def _nsa_kernel(qT_ref, gT_ref, kc_ref, vcT_ref, ks_ref, vsT_ref, kw_ref, vwT_ref, ovlT_ref, exp_ref,
                bias_ref, o_ref, m_ref, l_ref, acc_ref, *, n_slc, topn):
    qi = pl.program_id(1)
    ncmp = kc_ref.shape[0]
    qpos = qi * TQ + _col_iota((1, TQ))
    pad = jnp.zeros((LANES - HEAD_DIM, TQ), bf16)
    qs = [jnp.concatenate([qT_ref[h * HEAD_DIM:(h + 1) * HEAD_DIM, :], pad], axis=0)
          for h in range(HG)]

    cmp_ok = _row_iota((ncmp, 1)) * CMP_STRIDE + (CMP_LEN - 1) <= qpos
    o_cmp = []
    pc_sum = jnp.zeros((ncmp, TQ), f32)
    for h in range(HG):
        s = jnp.dot(kc_ref[...], qs[h], preferred_element_type=f32) * HEAD_DIM ** -0.5
        s = jnp.where(cmp_ok, s, NEG)
        e = jnp.exp(s - jnp.maximum(jnp.max(s, axis=0, keepdims=True), FLOOR))
        den = jnp.sum(e, axis=0, keepdims=True)
        pc = e / jnp.where(den > 0.0, den, 1.0)
        pc_sum = pc_sum + pc
        o_cmp.append(jnp.dot(vcT_ref[...], pc.astype(bf16), preferred_element_type=f32))
    imp = jnp.zeros((LANES, TQ), f32)
    for part in _split3(pc_sum):
        imp = imp + jnp.dot(ovlT_ref[...], part, preferred_element_type=f32)
    blk = _row_iota((LANES, 1))
    blk_f = blk.astype(f32)
    cur = qpos // SLC_LEN
    forced = (blk == 0) | (blk == cur) | (blk == cur - 1)
    imp = jnp.where(forced, jnp.inf, imp)
    imp = jnp.where((blk * SLC_LEN <= qpos) & (blk < n_slc), imp, -jnp.inf)

    def pick(_, st):
        imp, sel = st
        best = jnp.max(imp, axis=0, keepdims=True)
        first = jnp.min(jnp.where(imp == best, blk_f, float(LANES)), axis=0, keepdims=True)
        hit = blk_f == first
        return jnp.where(hit, -jnp.inf, imp), jnp.where(hit & (best > -jnp.inf), 1.0, sel)

    _, sel = lax.fori_loop(0, topn, pick, (imp, jnp.zeros((LANES, TQ), f32)))
    selb = sel.astype(bf16)

    krow = _row_iota((TK, 1))
    qcol = _col_iota((1, TQ))
    causal = krow <= qcol

    def slc_keep(kc, kind):
        keep = jnp.dot(exp_ref[kc], selb, preferred_element_type=f32) > 0.5
        return keep & causal if kind == 0 else keep

    def slc_finish(kind, h, s, keep):
        s = s * (HEAD_DIM ** -0.5 * LOG2E)
        if kind < 2:
            s = s + bias_ref[h, kind]
        return jnp.where(keep, s, NEG)

    _attend(qi, HG, lambda kc, h: jnp.dot(_k_chunk(ks_ref, kc), qs[h], preferred_element_type=f32),
            slc_finish, lambda kc, h: vsT_ref[kc], m_ref, l_ref, acc_ref, shared=slc_keep)
    o_slc = [_flash_result(h, l_ref, acc_ref) for h in range(HG)]

    def win_finish(kind, h, s, _):
        s = s * (HEAD_DIM ** -0.5 * LOG2E)
        if kind < 2:
            s = s + bias_ref[h, kind]
        if kind == 0:
            return jnp.where(causal, s, NEG)
        return jnp.where(krow > qcol, s, NEG) if kind == 2 else s

    _attend(qi, HG, lambda kc, h: jnp.dot(_k_chunk(kw_ref, kc), qs[h], preferred_element_type=f32),
            win_finish, lambda kc, h: vwT_ref[kc], m_ref, l_ref, acc_ref,
            first=jnp.maximum(qi - WIN // TK, 0))
    o_win = [_flash_result(h, l_ref, acc_ref) for h in range(HG)]

    gates = jax.nn.sigmoid(gT_ref[...])
    heads = []
    for h in range(HG):
        c = S_BG + 3 * h
        heads.append(gates[c:c + 1, :] * o_cmp[h] + gates[c + 1:c + 2, :] * o_slc[h]
                     + gates[c + 2:c + 3, :] * o_win[h])
    o_ref[...] = jnp.concatenate(heads, axis=0).T.astype(o_ref.dtype)


def _nsa(hr, ht, hs, kc, vcT, ovlT, expand, bias):
    B, L, _ = hr.shape
    nch = L // TK
    n = kc.shape[1]
    n_slc = L // SLC_LEN
    topn = min(SLC_TOPN, n_slc)
    return pl.pallas_call(
        functools.partial(_nsa_kernel, n_slc=n_slc, topn=topn),
        grid=(B, L // TQ),
        in_specs=[
            _q_spec(T_BQ), _q_spec(0, NS),
            pl.BlockSpec((None, n, LANES), lambda b, i: (b, 0, 0)),
            pl.BlockSpec((None, HEAD_DIM, n), lambda b, i: (b, 0, 0)),
            _k_spec(L, R_BKS, LANES), _vT_spec(nch, T_BVS, HEAD_DIM),
            _k_spec(L, R_BKW, LANES), _vT_spec(nch, T_BVW, HEAD_DIM),
            pl.BlockSpec(ovlT.shape, lambda b, i: (0, 0)),
            pl.BlockSpec(expand.shape, lambda b, i: (0, 0, 0)),
            _BIAS_SPEC,
        ],
        out_specs=_OUT_SPEC,
        out_shape=jax.ShapeDtypeStruct((B, L, GW), bf16),
        scratch_shapes=_flash_scratch(HG),
        compiler_params=_cparams("parallel", "arbitrary"),
        name="nsa",
    )(ht, hs, kc, vcT, hr, ht, hr, ht, ovlT, expand, bias)


def _nsa_tables(L):
    n = L // CMP_STRIDE
    n_slc = L // SLC_LEN
    c0 = jnp.arange(n, dtype=i32)[None, :] * CMP_STRIDE
    s0 = jnp.arange(LANES, dtype=i32)[:, None] * SLC_LEN
    ovlT = (c0 <= s0 + SLC_LEN - 1) & (c0 + CMP_LEN - 1 >= s0)
    ovlT = ovlT & (jnp.arange(n)[None, :] < n - 1) & (jnp.arange(LANES)[:, None] < n_slc)
    tokblk = jnp.arange(L, dtype=i32) // SLC_LEN
    expand = tokblk[:, None] == jnp.arange(LANES, dtype=i32)[None, :]
    return ovlT.astype(bf16), expand.reshape(L // TK, TK, LANES).astype(bf16)


def _outproj_kernel(oa_ref, ob_ref, oc_ref, od_ref, w_ref, x_ref, g_ref, b_ref, y_ref):
    mix = jnp.zeros(x_ref.shape, f32)
    for n, o_ref in enumerate((oa_ref, ob_ref, oc_ref, od_ref)):
        mix = mix + jnp.dot(o_ref[...], w_ref[n * GW:(n + 1) * GW, :], preferred_element_type=f32)
    y_ref[...] = _layer_norm(ALPHA * x_ref[...] + mix, g_ref[...], b_ref[...])


def _outproj(oa, ob, oc, od, w, x, g, b, tm=512):
    M, D = x.shape
    ospec = pl.BlockSpec((tm, GW), lambda i: (i, 0))
    vec = pl.BlockSpec((1, D), lambda i: (0, 0))
    return pl.pallas_call(
        _outproj_kernel,
        grid=(M // tm,),
        in_specs=[ospec, ospec, ospec, ospec, pl.BlockSpec((D, D), lambda i: (0, 0)),
                  pl.BlockSpec((tm, D), lambda i: (i, 0)), vec, vec],
        out_specs=pl.BlockSpec((tm, D), lambda i: (i, 0)),
        out_shape=jax.ShapeDtypeStruct((M, D), f32),
        compiler_params=_cparams("parallel"),
        name="outproj_ln",
    )(oa, ob, oc, od, w, x, g, b)


def _memkv_kernel(mem_ref, wkT_ref, wv_ref, kT_ref, v_ref):
    mb = mem_ref[...].astype(bf16)
    kT_ref[...] = lax.dot_general(wkT_ref[...], mb, (((1,), (1,)), ((), ())),
                                  preferred_element_type=f32).astype(bf16)
    v_ref[...] = jnp.dot(mb, wv_ref[...], preferred_element_type=f32).astype(bf16)


def _memkv(mem, wkT, wv):
    B, N, D = mem.shape
    wspec = pl.BlockSpec((D, D), lambda b: (0, 0))
    return pl.pallas_call(
        _memkv_kernel,
        grid=(B,),
        in_specs=[pl.BlockSpec((None, N, D), lambda b: (b, 0, 0)), wspec, wspec],
        out_specs=[pl.BlockSpec((None, D, N), lambda b: (b, 0, 0)),
                   pl.BlockSpec((None, N, D), lambda b: (b, 0, 0))],
        out_shape=[jax.ShapeDtypeStruct((B, D, N), bf16), jax.ShapeDtypeStruct((B, N, D), bf16)],
        compiler_params=_cparams("parallel"),
        name="cross_kv",
    )(mem, wkT, wv)


def _cross_kernel(x_ref, wq_ref, kT_ref, v_ref, wo_ref, g_ref, b_ref, y_ref):
    x = x_ref[...]
    q = jnp.dot(x.astype(bf16), wq_ref[...], preferred_element_type=f32).astype(bf16)
    outs = []
    for h in range(CROSS_HEADS):
        sl = slice(h * CROSS_DIM, (h + 1) * CROSS_DIM)
        s = jnp.dot(q[:, sl], kT_ref[sl, :], preferred_element_type=f32) * CROSS_DIM ** -0.5
        e = jnp.exp(s - jnp.max(s, axis=1, keepdims=True))
        p = e / jnp.sum(e, axis=1, keepdims=True)
        outs.append(jnp.dot(p.astype(bf16), v_ref[:, sl], preferred_element_type=f32).astype(bf16))
    o = jnp.concatenate(outs, axis=1)
    y = ALPHA * x + jnp.dot(o, wo_ref[...], preferred_element_type=f32)
    y_ref[...] = _layer_norm(y, g_ref[...], b_ref[...])


def _cross(x, wq, kT, v, wo, g, b, tm=512):
    B, L, D = x.shape
    N = v.shape[1]
    wspec = pl.BlockSpec((D, D), lambda bb, i: (0, 0))
    vec = pl.BlockSpec((1, D), lambda bb, i: (0, 0))
    return pl.pallas_call(
        _cross_kernel,
        grid=(B, L // tm),
        in_specs=[pl.BlockSpec((None, tm, D), lambda bb, i: (bb, i, 0)), wspec,
                  pl.BlockSpec((None, D, N), lambda bb, i: (bb, 0, 0)),
                  pl.BlockSpec((None, N, D), lambda bb, i: (bb, 0, 0)), wspec, vec, vec],
        out_specs=pl.BlockSpec((None, tm, D), lambda bb, i: (bb, i, 0)),
        out_shape=jax.ShapeDtypeStruct((B, L, D), f32),
        compiler_params=_cparams("parallel", "parallel"),
        name="cross_ln",
    )(x, wq, kT, v, wo, g, b)


def _mlp_kernel(x_ref, w1_ref, w2_ref, g_ref, b_ref, y_ref, acc_ref):
    j = pl.program_id(1)

    @pl.when(j == 0)
    def _():
        acc_ref[...] = jnp.zeros(acc_ref.shape, f32)

    hdn = jnp.dot(x_ref[...].astype(bf16), w1_ref[...], preferred_element_type=f32)
    hdn = jnp.square(jnp.maximum(hdn, 0.0)).astype(bf16)
    acc_ref[...] += jnp.dot(hdn, w2_ref[...], preferred_element_type=f32)

    @pl.when(j == pl.num_programs(1) - 1)
    def _():
        y_ref[...] = _layer_norm(ALPHA * x_ref[...] + acc_ref[...], g_ref[...], b_ref[...])


def _mlp(x, w1, w2, g, b, tm=512, tf=1024):
    M, D = x.shape
    F = w1.shape[1]
    vec = pl.BlockSpec((1, D), lambda i, j: (0, 0))
    return pl.pallas_call(
        _mlp_kernel,
        grid=(M // tm, F // tf),
        in_specs=[pl.BlockSpec((tm, D), lambda i, j: (i, 0)),
                  pl.BlockSpec((D, tf), lambda i, j: (0, j)),
                  pl.BlockSpec((tf, D), lambda i, j: (j, 0)), vec, vec],
        out_specs=pl.BlockSpec((tm, D), lambda i, j: (i, 0)),
        out_shape=jax.ShapeDtypeStruct((M, D), f32),
        scratch_shapes=[pltpu.VMEM((tm, D), f32)],
        compiler_params=_cparams("parallel", "arbitrary"),
        name="mlp_ln",
    )(x, w1, w2, g, b)


def _split_w_in(w):
    sizes = (GW, GW, GW, IDX_HEADS * IDX_DIM, IDX_DIM, IDX_HEADS,
             GW, HEAD_DIM, HEAD_DIM, HEAD_DIM, HEAD_DIM, HEAD_DIM, HEAD_DIM, 3 * HG,
             GW, GW, GW, GW, GW, GW)
    offs = [0]
    for s in sizes:
        offs.append(offs[-1] + s)
    return [w[:, offs[n]:offs[n + 1]] for n in range(len(sizes))]


def _layout_w_in(w):
    (a_q, a_k, a_v, a_qi, a_ki, a_w, b_q, b_kc, b_vc, b_ks, b_vs, b_kw, b_vw, b_g,
     c_q, c_k, c_v, d_q, d_k, d_v) = _split_w_in(w)
    twice = lambda t: jnp.concatenate([t, t], axis=1)
    wr = jnp.concatenate([a_k, c_k, d_k, twice(b_ks), twice(b_kw), twice(a_ki), b_kc, b_vc], axis=1)
    wt = jnp.concatenate([a_q, a_v, a_qi, b_q, c_q, c_v, d_q, d_v, b_vs, b_vw], axis=1).T
    ws = jnp.concatenate([a_w, b_g, jnp.zeros((w.shape[0], NS - IDX_HEADS - 3 * HG), w.dtype)], axis=1).T
    assert wr.shape[1] == NR and wt.shape[0] == NT
    return wr.astype(bf16), wt.astype(bf16), ws.astype(bf16)


def _mixers(x, bias, w_in_l, pos_k, pos_v, w1_k, w2_k, w1_v, w2_v, lq1, lk1, lq2, lk2, diff_g, lam_init):
    B, L, _ = x.shape
    wr, wt, ws = _layout_w_in(w_in_l)
    hr, ht, hs = _project(x, wr, wt, ws)
    o_a = _dsa(hr, ht, hs, bias[0:HG])

    n = L // CMP_STRIDE
    half = CMP_STRIDE * HEAD_DIM
    xk = hr[:, :, R_BKVC:R_BKVC + HEAD_DIM].reshape(B, n, half)
    xv = hr[:, :, R_BKVC + HEAD_DIM:R_BKVC + 2 * HEAD_DIM].reshape(B, n, half)
    kc, vcT = _compress(
        xk, xv, pos_k.reshape(2, half), pos_v.reshape(2, half),
        w1_k.reshape(2, half, HEAD_DIM).astype(bf16), w1_v.reshape(2, half, HEAD_DIM).astype(bf16),
        jnp.concatenate([w2_k, w2_k], axis=1).astype(bf16), w2_v.T.astype(bf16))
    ovlT, expand = _nsa_tables(L)
    o_b = _nsa(hr, ht, hs, kc, vcT, ovlT, expand, bias[HG:2 * HG])
    o_c = _moba(hr, ht, bias[2 * HG:3 * HG])
    o_d = _diff(hr, ht, bias[3 * HG:4 * HG], lq1.reshape(1, -1), lk1.reshape(1, -1),
                lq2.reshape(1, -1), lk2.reshape(1, -1), jnp.full((1, 1), lam_init, f32),
                diff_g.reshape(-1, 1))
    return o_a, o_b, o_c, o_d


def kernel(x, mem, rel_bias, w_in, w_out, nsa_pos_k, nsa_pos_v, nsa_w1_k, nsa_w2_k, nsa_w1_v, nsa_w2_v, diff_lq1, diff_lk1, diff_lq2, diff_lk2, diff_g, ln1_g, ln1_b, xq, xk, xv, xo, ln2_g, ln2_b, mlp_w1, mlp_w2, ln3_g, ln3_b):
    B, L, D = x.shape
    bias = _bias_tiles(rel_bias)
    for l in range(DEPTH):
        lam_init = 0.8 - 0.6 * math.exp(-0.3 * l)
        o_a, o_b, o_c, o_d = _mixers(
            x, bias, w_in[l], nsa_pos_k[l], nsa_pos_v[l], nsa_w1_k[l], nsa_w2_k[l],
            nsa_w1_v[l], nsa_w2_v[l], diff_lq1[l], diff_lk1[l], diff_lq2[l], diff_lk2[l],
            diff_g[l], lam_init)
        flat = lambda t: t.reshape(B * L, -1)
        row = lambda t: t.reshape(1, D)
        x2 = _outproj(flat(o_a), flat(o_b), flat(o_c), flat(o_d), w_out[l].astype(bf16),
                      flat(x), row(ln1_g[l]), row(ln1_b[l]))
        kT, v = _memkv(mem, xk[l].T.astype(bf16), xv[l].astype(bf16))
        x3 = _cross(x2.reshape(B, L, D), xq[l].astype(bf16), kT, v, xo[l].astype(bf16),
                    row(ln2_g[l]), row(ln2_b[l]))
        x = _mlp(flat(x3), mlp_w1[l].astype(bf16), mlp_w2[l].astype(bf16),
                 row(ln3_g[l]), row(ln3_b[l])).reshape(B, L, D)
    return x
```

```python
import functools
import math

import jax
import jax.numpy as jnp
from jax import lax
from jax.experimental import pallas as pl
from jax.experimental.pallas import tpu as pltpu

f32 = jnp.float32
bf16 = jnp.bfloat16
i32 = jnp.int32
i16 = jnp.int16

D_MODEL = 1024
DEPTH = 4
HEAD_DIM = 64
HG = 4
GW = HG * HEAD_DIM
REL_BUCKETS = 32
REL_MAX_DIST = 128
IDX_HEADS = 8
IDX_DIM = 64
DSA_TOPK = 256
CMP_LEN = 32
CMP_STRIDE = 16
SLC_LEN = 64
SLC_TOPN = 16
WIN = 512
MOBA_BLOCK = 256
MOBA_TOPK = 3
DIFF_HALF = HEAD_DIM // 2
CROSS_HEADS = 4
CROSS_DIM = D_MODEL // CROSS_HEADS
D_FF = 4 * D_MODEL
ALPHA = (2 * DEPTH) ** 0.25

NEG = -1e30
FLOOR = -1e29
INT_MIN = -(2 ** 31)
HALF = 2 ** 15
LANES = 128
TQ = 256
TK = 256
VMEM_LIMIT = 56 * 1024 * 1024
LOG2E = math.log2(math.e)
FAR_CHAINS = 16
MATMULS_AHEAD = 6

R_AK, R_CK, R_DK, R_BKS, R_BKW, R_AKI, R_BKVC = 0, 256, 512, 768, 896, 1024, 1152
NR = 1280
T_AQ, T_AV, T_AQI, T_BQ, T_CQ, T_CV, T_DQ, T_DV, T_BVS, T_BVW = (
    0, 256, 512, 1024, 1280, 1536, 1792, 2048, 2304, 2368)
NT = 2432
S_AW, S_BG, NS = 0, 8, 32


def _cparams(*sem):
    return pltpu.CompilerParams(dimension_semantics=sem, vmem_limit_bytes=VMEM_LIMIT)


def _rel_bucket(dist):
    n = jnp.maximum(dist, 0)
    max_exact = REL_BUCKETS // 2
    nf = jnp.maximum(n, max_exact).astype(f32)
    large = max_exact + (jnp.log(nf / max_exact) / math.log(REL_MAX_DIST / max_exact)
                         * (REL_BUCKETS - max_exact)).astype(i32)
    large = jnp.minimum(large, REL_BUCKETS - 1)
    return jnp.where(n < max_exact, n, large)


def _bias_tiles(rel_bias):
    assert 2 * TQ - TK + 1 >= REL_MAX_DIST
    d = jnp.arange(TQ, dtype=i32)[None, :] - jnp.arange(TK, dtype=i32)[:, None]
    bucket = jnp.stack([_rel_bucket(d + off) for off in (0, TQ)])
    far = rel_bias[_rel_bucket(jnp.int32(2 * TQ))]
    table = ((rel_bias - far) * LOG2E).astype(f32)
    onehot = (bucket[None] == jnp.arange(REL_BUCKETS, dtype=i32)[:, None, None, None]).astype(f32)
    return jnp.einsum('bntq,bh->hntq', onehot, table, precision=lax.Precision.HIGHEST)


def _row_iota(shape):
    return lax.broadcasted_iota(i32, shape, 0)


def _col_iota(shape):
    return lax.broadcasted_iota(i32, shape, 1)


def _flash_init(m_ref, l_ref, acc_ref):
    m_ref[...] = jnp.full(m_ref.shape, NEG, f32)
    l_ref[...] = jnp.zeros(l_ref.shape, f32)
    acc_ref[...] = jnp.zeros(acc_ref.shape, f32)


def _fold(x, op, rows):
    while x.shape[0] > rows:
        half = x.shape[0] // 2
        x = op(x[:half], x[half:])
    return x


def _flash_step(c, s, vT, m_ref, l_ref, acc_ref, pen=None):
    m_prev = m_ref[c]
    m_chunk = jnp.max(_fold(s, jnp.maximum, 8), axis=0, keepdims=True)
    if pen is not None:
        m_chunk = m_chunk + pen
    m_new = jnp.maximum(m_prev, m_chunk)
    m_use = jnp.maximum(m_new, FLOOR)
    alpha = jnp.exp2(jnp.maximum(m_prev, FLOOR) - m_use)
    p = jnp.exp2(s - (m_use if pen is None else m_use - pen))
    l_ref[c] = alpha * l_ref[c] + jnp.sum(_fold(p, jnp.add, 8), axis=0, keepdims=True)
    acc_ref[c] = alpha * acc_ref[c] + jnp.dot(vT, p.astype(bf16), preferred_element_type=f32)
    m_ref[c] = m_new


def _flash_result(c, l_ref, acc_ref):
    l = l_ref[c]
    return jnp.where(l > 0.0, acc_ref[c] / jnp.where(l > 0.0, l, 1.0), 0.0)


def _attend(qi, n, raw, finish, vT, m_ref, l_ref, acc_ref, first=0, shared=None):
    def run(chunks):
        items = [(kc, kind, c) for kc, kind in chunks for c in range(n)]
        ctx, logits = {}, {}

        def issue(j):
            kc, kind, c = items[j]
            if c == 0 and shared is not None:
                ctx[j // n] = shared(kc, kind)
            logits[j] = raw(kc, c)

        for j in range(min(MATMULS_AHEAD, len(items))):
            issue(j)
        for j, (kc, kind, c) in enumerate(items):
            if j + MATMULS_AHEAD < len(items):
                issue(j + MATMULS_AHEAD)
            s = finish(kind, c, logits.pop(j), ctx.get(j // n))
            s, pen = s if isinstance(s, tuple) else (s, None)
            _flash_step(c, s, vT(kc, c), m_ref, l_ref, acc_ref, pen)

    _flash_init(m_ref, l_ref, acc_ref)
    n_far = jnp.maximum(qi - 1 - first, 0)
    group = max(1, FAR_CHAINS // n)

    def far_group(j, carry):
        run([(first + group * j + t, 2) for t in range(group)])
        return carry

    lax.fori_loop(0, n_far // group, far_group, 0)
    size = group // 2
    while size >= 1:
        done = first + (n_far // (2 * size)) * (2 * size)

        @pl.when((n_far // size) % 2 == 1)
        def _(done=done, size=size):
            run([(done + t, 2) for t in range(size)])

        size //= 2

    @pl.when(qi >= 1)
    def _():
        run([(qi - 1, 1), (qi, 0)])

    @pl.when(qi == 0)
    def _():
        run([(qi, 0)])


def _pair_rows(qT_ref, h, width=HEAD_DIM, offset=0):
    blk = qT_ref[(h // 2) * LANES:(h // 2 + 1) * LANES, :]
    r = _row_iota((LANES, 1))
    lo = (h % 2) * HEAD_DIM + offset
    return jnp.where((r >= lo) & (r < lo + width), blk, jnp.zeros_like(blk))


def _layer_norm(y, g, b):
    mu = jnp.mean(y, axis=-1, keepdims=True)
    yc = y - mu
    var = jnp.mean(yc * yc, axis=-1, keepdims=True)
    return yc * lax.rsqrt(var + 1e-5) * g + b


def _proj_kernel(x_ref, wr_ref, wt_ref, ws_ref, hr_ref, ht_ref, hs_ref):
    xb = x_ref[...].astype(bf16)
    nt = (((1,), (1,)), ((), ()))
    hr_ref[...] = jnp.dot(xb, wr_ref[...], preferred_element_type=f32).astype(bf16)
    ht_ref[...] = lax.dot_general(wt_ref[...], xb, nt, preferred_element_type=f32).astype(bf16)
    hs_ref[...] = lax.dot_general(ws_ref[...], xb, nt, preferred_element_type=f32)


def _project(x, wr, wt, ws):
    B, L, D = x.shape
    nch = L // TK
    return pl.pallas_call(
        _proj_kernel,
        grid=(B, nch),
        in_specs=[
            pl.BlockSpec((None, TK, D), lambda b, i: (b, i, 0)),
            pl.BlockSpec((D, NR), lambda b, i: (0, 0)),
            pl.BlockSpec((NT, D), lambda b, i: (0, 0)),
            pl.BlockSpec((NS, D), lambda b, i: (0, 0)),
        ],
        out_specs=[
            pl.BlockSpec((None, TK, NR), lambda b, i: (b, i, 0)),
            pl.BlockSpec((None, None, NT, TK), lambda b, i: (b, i, 0, 0)),
            pl.BlockSpec((None, None, NS, TK), lambda b, i: (b, i, 0, 0)),
        ],
        out_shape=[
            jax.ShapeDtypeStruct((B, L, NR), bf16),
            jax.ShapeDtypeStruct((B, nch, NT, TK), bf16),
            jax.ShapeDtypeStruct((B, nch, NS, TK), f32),
        ],
        compiler_params=_cparams("parallel", "parallel"),
        name="proj",
    )(x, wr, wt, ws)


def _q_spec(off, rows=GW):
    return pl.BlockSpec((None, None, rows, TQ), lambda b, i: (b, i, off // rows, 0))


def _vT_spec(nch, off, rows=GW):
    return pl.BlockSpec((None, nch, rows, TK), lambda b, i: (b, 0, off // rows, 0))


def _k_spec(L, off, cols=GW):
    return pl.BlockSpec((None, L, cols), lambda b, i: (b, 0, off // cols))


_BIAS_SPEC = pl.BlockSpec((HG, 2, TK, TQ), lambda b, i: (0, 0, 0, 0))
_OUT_SPEC = pl.BlockSpec((None, TQ, GW), lambda b, i: (b, i, 0))


def _flash_scratch(chains, dv=HEAD_DIM):
    return [pltpu.VMEM((chains, 1, TQ), f32), pltpu.VMEM((chains, 1, TQ), f32),
            pltpu.VMEM((chains, dv, TQ), f32)]


def _k_chunk(k_ref, kc, pair=None):
    rows = pl.ds(pl.multiple_of(kc * TK, TK), TK)
    if pair is None:
        return k_ref[rows, :]
    return k_ref[rows, pair * LANES:(pair + 1) * LANES]


def _dsa_kernel(qT_ref, qiT_ref, wT_ref, k_ref, ki_ref, vT_ref, bias_ref, o_ref,
                key_ref, hi_ref, lo_ref, m_ref, l_ref, acc_ref, *, topk):
    qi = pl.program_id(1)
    nch = qi + 1
    qpos = qi * TQ + _col_iota((1, TQ))
    krow = _row_iota((TK, 1))
    wT = wT_ref[...] * (IDX_DIM ** -0.5 * IDX_HEADS ** -0.5)
    qidx = [_pair_rows(qiT_ref, h) for h in range(IDX_HEADS)]

    def score_chunk(kc, carry):
        ki2 = _k_chunk(ki_ref, kc)
        s = jnp.zeros((TK, TQ), f32)
        for h in range(IDX_HEADS):
            r = jnp.dot(ki2, qidx[h], preferred_element_type=f32)
            s = s + wT[S_AW + h:S_AW + h + 1, :] * jnp.maximum(r, 0.0)
        s = jnp.where(s == 0.0, 0.0, s)
        bits = lax.bitcast_convert_type(s, i32)
        key = bits ^ ((bits >> 31) & 0x7FFFFFFF)
        key = jnp.where(kc * TK + krow <= qpos, key, INT_MIN)
        key_ref[kc] = key
        hi_ref[kc] = (key >> 16).astype(i16)
        lo_ref[kc] = ((key & 0xFFFF) - HALF).astype(i16)
        return carry

    lax.fori_loop(0, nch, score_chunk, 0)

    def count(ref, pred):
        one = jnp.ones((), ref.dtype)
        zero = jnp.zeros((), ref.dtype)
        rows = 8 * 4 // ref.dtype.itemsize

        def body(kc, acc):
            hit = jnp.where(pred(ref[kc]), one, zero)
            return acc + _fold(hit, jnp.add, rows).astype(f32)
        acc = lax.fori_loop(0, nch, body, jnp.zeros((rows, TQ), f32))
        return jnp.sum(acc, axis=0, keepdims=True)

    def kth_largest_i16(ref, want):
        def bisect(i, t_u):
            cand_u = t_u | jnp.left_shift(jnp.int32(1), 15 - i)
            cand = (cand_u - HALF).astype(i16)
            return jnp.where(count(ref, lambda k: k >= cand) >= want, cand_u, t_u)
        return lax.fori_loop(0, 16, bisect, jnp.zeros((1, TQ), i32)) - HALF

    hi_t = kth_largest_i16(hi_ref, topk)
    hi_t16 = hi_t.astype(i16)
    above = count(hi_ref, lambda k: k > hi_t16)

    def keep_low(kc, carry):
        lo_ref[kc] = jnp.where(hi_ref[kc] == hi_t16, lo_ref[kc], jnp.full((), -HALF, i16))
        return carry

    lax.fori_loop(0, nch, keep_low, 0)
    lo_t = kth_largest_i16(lo_ref, topk - above)
    thr = jnp.maximum((hi_t << 16) | (lo_t + HALF), INT_MIN + 1)
    cnt_ge = count(key_ref, lambda k: k >= thr)

    @pl.when(jnp.max(cnt_ge) > topk)
    def _():
        need = topk - (cnt_ge - count(key_ref, lambda k: k == thr))
        tri = (_col_iota((TK, TK)) <= _row_iota((TK, TK))).astype(bf16)

        def body(kc, seen):
            k = key_ref[kc]
            tie = k == thr
            tief = jnp.where(tie, 1.0, 0.0)
            pref = jnp.dot(tri, tief.astype(bf16), preferred_element_type=f32) + seen
            key_ref[kc] = jnp.where(tie & (pref > need), INT_MIN, k)
            return seen + jnp.sum(tief, axis=0, keepdims=True)

        lax.fori_loop(0, nch, body, jnp.zeros((1, TQ), f32))

    qs = [_pair_rows(qT_ref, h) for h in range(HG)]

    def finish(kind, h, s, keep):
        if kind < 2:
            s = s + bias_ref[h, kind]
        return jnp.where(keep, s, NEG)

    _attend(qi, HG, lambda kc, h: jnp.dot(_k_chunk(k_ref, kc, h // 2), qs[h], preferred_element_type=f32),
            finish, lambda kc, h: vT_ref[kc, h * HEAD_DIM:(h + 1) * HEAD_DIM, :],
            m_ref, l_ref, acc_ref, shared=lambda kc, kind: key_ref[kc] >= thr)
    outT = jnp.concatenate([_flash_result(h, l_ref, acc_ref) for h in range(HG)], axis=0)
    o_ref[...] = outT.T.astype(o_ref.dtype)


def _dsa(hr, ht, hs, bias):
    B, L, _ = hr.shape
    nch = L // TK
    topk = min(DSA_TOPK, L // 4)
    return pl.pallas_call(
        functools.partial(_dsa_kernel, topk=topk),
        grid=(B, L // TQ),
        in_specs=[
            _q_spec(T_AQ), _q_spec(T_AQI, 2 * GW), _q_spec(0, NS),
            _k_spec(L, R_AK), _k_spec(L, R_AKI, LANES), _vT_spec(nch, T_AV), _BIAS_SPEC,
        ],
        out_specs=_OUT_SPEC,
        out_shape=jax.ShapeDtypeStruct((B, L, GW), bf16),
        scratch_shapes=[pltpu.VMEM((nch, TK, TQ), i32), pltpu.VMEM((nch, TK, TQ), i16),
                        pltpu.VMEM((nch, TK, TQ), i16)] + _flash_scratch(HG),
        compiler_params=_cparams("parallel", "arbitrary"),
        name="dsa",
    )(ht, ht, hs, hr, hr, ht, bias)


def _moba_kernel(qT_ref, k_ref, vT_ref, bias_ref, o_ref, km_ref, sel_ref, m_ref, l_ref, acc_ref, *, nch, topk):
    qi = pl.program_id(1)
    nb = km_ref.shape[0]

    @pl.when(qi == 0)
    def _():
        km_ref[...] = jnp.zeros(km_ref.shape, f32)
        for n in range(nch):
            blk = k_ref[n * MOBA_BLOCK:(n + 1) * MOBA_BLOCK, :].astype(f32)
            km_ref[n:n + 1, :] = jnp.sum(blk, axis=0, keepdims=True) * (1.0 / MOBA_BLOCK)

    km = km_ref[...]
    km_hi = km.astype(bf16)
    km_lo = (km - km_hi.astype(f32)).astype(bf16)
    blk_id = _row_iota((nb, 1))
    blk_f = blk_id.astype(f32)
    for h in range(HG):
        qh = qT_ref[...]
        r = _row_iota((GW, 1))
        qh = jnp.where((r >= h * HEAD_DIM) & (r < (h + 1) * HEAD_DIM), qh, jnp.zeros_like(qh))
        gate = (jnp.dot(km_hi, qh, preferred_element_type=f32)
                + jnp.dot(km_lo, qh, preferred_element_type=f32))
        gate = jnp.where(blk_id < qi, gate, -jnp.inf)
        sel = jnp.zeros((nb, TQ), f32)
        for _ in range(topk):
            best = jnp.max(gate, axis=0, keepdims=True)
            first = jnp.min(jnp.where(gate == best, blk_f, float(nb)), axis=0, keepdims=True)
            hit = blk_f == first
            sel = jnp.where(hit & (best > -jnp.inf), 1.0, sel)
            gate = jnp.where(hit, -jnp.inf, gate)
        sel_ref[h] = sel

    qs = [_pair_rows(qT_ref, h) for h in range(HG)]
    causal = _row_iota((TK, 1)) <= _col_iota((1, TQ))

    def finish(kind, h, s, kc):
        if kind < 2:
            s = s + bias_ref[h, kind]
        if kind == 0:
            return jnp.where(causal, s, NEG)
        return s, (1.0 - sel_ref[h, pl.ds(kc, 1), :]) * NEG

    _attend(qi, HG, lambda kc, h: jnp.dot(_k_chunk(k_ref, kc, h // 2), qs[h], preferred_element_type=f32),
            finish, lambda kc, h: vT_ref[kc, h * HEAD_DIM:(h + 1) * HEAD_DIM, :],
            m_ref, l_ref, acc_ref, shared=lambda kc, kind: kc)
    outT = jnp.concatenate([_flash_result(h, l_ref, acc_ref) for h in range(HG)], axis=0)
    o_ref[...] = outT.T.astype(o_ref.dtype)


def _moba(hr, ht, bias):
    B, L, _ = hr.shape
    nch = L // TK
    nb = -(-nch // 8) * 8
    topk = min(MOBA_TOPK, nch - 1)
    return pl.pallas_call(
        functools.partial(_moba_kernel, nch=nch, topk=topk),
        grid=(B, L // TQ),
        in_specs=[_q_spec(T_CQ), _k_spec(L, R_CK), _vT_spec(nch, T_CV), _BIAS_SPEC],
        out_specs=_OUT_SPEC,
        out_shape=jax.ShapeDtypeStruct((B, L, GW), bf16),
        scratch_shapes=[pltpu.VMEM((nb, GW), f32), pltpu.VMEM((HG, nb, TQ), f32)] + _flash_scratch(HG),
        compiler_params=_cparams("parallel", "arbitrary"),
        name="moba",
    )(ht, hr, ht, bias)


def _diff_kernel(qT_ref, k_ref, vT_ref, bias_ref, lq1_ref, lk1_ref, lq2_ref, lk2_ref, li_ref, g_ref,
                 o_ref, m_ref, l_ref, acc_ref):
    qi = pl.program_id(1)
    lam_init = li_ref[...]
    lam = (jnp.exp(jnp.sum(lq1_ref[...] * lk1_ref[...], axis=1, keepdims=True))
           - jnp.exp(jnp.sum(lq2_ref[...] * lk2_ref[...], axis=1, keepdims=True)) + lam_init)
    qs = [_pair_rows(qT_ref, h, DIFF_HALF, c * DIFF_HALF) for h in range(HG) for c in range(2)]
    causal = _row_iota((TK, 1)) <= _col_iota((1, TQ))

    def finish(kind, c, s, _):
        if kind < 2:
            s = s + bias_ref[c // 2, kind]
        return jnp.where(causal, s, NEG) if kind == 0 else s

    _attend(qi, 2 * HG, lambda kc, c: jnp.dot(_k_chunk(k_ref, kc, c // 4), qs[c], preferred_element_type=f32),
            finish, lambda kc, c: vT_ref[kc, (c // 2) * HEAD_DIM:(c // 2 + 1) * HEAD_DIM, :],
            m_ref, l_ref, acc_ref)
    heads = []
    for h in range(HG):
        o = _flash_result(2 * h, l_ref, acc_ref) - lam * _flash_result(2 * h + 1, l_ref, acc_ref)
        ms = jnp.mean(o * o, axis=0, keepdims=True)
        heads.append(o * lax.rsqrt(ms + 1e-6) * g_ref[...] * (1.0 - lam_init))
    o_ref[...] = jnp.concatenate(heads, axis=0).T.astype(o_ref.dtype)


def _diff(hr, ht, bias, lq1, lk1, lq2, lk2, lam_init, g_col):
    B, L, _ = hr.shape
    nch = L // TK
    vec = pl.BlockSpec((1, DIFF_HALF), lambda b, i: (0, 0))
    return pl.pallas_call(
        _diff_kernel,
        grid=(B, L // TQ),
        in_specs=[
            _q_spec(T_DQ), _k_spec(L, R_DK), _vT_spec(nch, T_DV), _BIAS_SPEC,
            vec, vec, vec, vec,
            pl.BlockSpec((1, 1), lambda b, i: (0, 0)),
            pl.BlockSpec((HEAD_DIM, 1), lambda b, i: (0, 0)),
        ],
        out_specs=_OUT_SPEC,
        out_shape=jax.ShapeDtypeStruct((B, L, GW), bf16),
        scratch_shapes=_flash_scratch(2 * HG),
        compiler_params=_cparams("parallel", "arbitrary"),
        name="diff",
    )(ht, hr, ht, bias, lq1, lk1, lq2, lk2, lam_init, g_col)


def _gelu_tanh(x):
    return 0.5 * x * (1.0 + jnp.tanh(math.sqrt(2.0 / math.pi) * (x + 0.044715 * (x * x * x))))


def _compress_kernel(xk_ref, xv_ref, pk_ref, pv_ref, w1k_ref, w1v_ref, w2k_ref, w2vT_ref, kc_ref, vcT_ref):
    def pre_act(x_ref, p_ref, w1_ref):
        x = x_ref[...].astype(f32)
        first = jnp.dot((x + p_ref[0:1, :]).astype(bf16), w1_ref[0], preferred_element_type=f32)
        second = jnp.dot((x + p_ref[1:2, :]).astype(bf16), w1_ref[1], preferred_element_type=f32)
        n = first.shape[0]
        return _gelu_tanh(first + pltpu.roll(second, n - 1, 0)).astype(bf16)

    gk = pre_act(xk_ref, pk_ref, w1k_ref)
    kc_ref[...] = jnp.dot(gk, w2k_ref[...], preferred_element_type=f32).astype(bf16)
    gv = pre_act(xv_ref, pv_ref, w1v_ref)
    vcT_ref[...] = lax.dot_general(w2vT_ref[...], gv, (((1,), (1,)), ((), ())),
                                   preferred_element_type=f32).astype(bf16)


def _compress(xk, xv, pk, pv, w1k, w1v, w2k, w2vT):
    B, n, W = xk.shape
    xspec = pl.BlockSpec((None, n, W), lambda b: (b, 0, 0))
    full = lambda a: pl.BlockSpec(a.shape, lambda b: (0,) * a.ndim)
    return pl.pallas_call(
        _compress_kernel,
        grid=(B,),
        in_specs=[xspec, xspec, full(pk), full(pv), full(w1k), full(w1v), full(w2k), full(w2vT)],
        out_specs=[
            pl.BlockSpec((None, n, LANES), lambda b: (b, 0, 0)),
            pl.BlockSpec((None, HEAD_DIM, n), lambda b: (b, 0, 0)),
        ],
        out_shape=[
            jax.ShapeDtypeStruct((B, n, LANES), bf16),
            jax.ShapeDtypeStruct((B, HEAD_DIM, n), bf16),
        ],
        compiler_params=_cparams("parallel"),
        name="nsa_compress",
    )(xk, xv, pk, pv, w1k, w1v, w2k, w2vT)


def _split3(x):
    hi = x.astype(bf16)
    r = x - hi.astype(f32)
    mid = r.astype(bf16)
    lo = (r - mid.astype(f32)).astype(bf16)
    return hi, mid, lo


def _nsa_kernel(qT_ref, gT_ref, kc_ref, vcT_ref, ks_ref, vsT_ref, kw_ref, vwT_ref, ovlT_ref, exp_ref,
                bias_ref, o_ref, m_ref, l_ref, acc_ref, *, n_slc, topn):
    qi = pl.program_id(1)
    ncmp = kc_ref.shape[0]
    qpos = qi * TQ + _col_iota((1, TQ))
    pad = jnp.zeros((LANES - HEAD_DIM, TQ), bf16)
    qs = [jnp.concatenate([qT_ref[h * HEAD_DIM:(h + 1) * HEAD_DIM, :], pad], axis=0)
          for h in range(HG)]

    cmp_ok = _row_iota((ncmp, 1)) * CMP_STRIDE + (CMP_LEN - 1) <= qpos
    o_cmp = []
    pc_sum = jnp.zeros((ncmp, TQ), f32)
    for h in range(HG):
        s = jnp.where(cmp_ok, jnp.dot(kc_ref[...], qs[h], preferred_element_type=f32), NEG)
        e = jnp.exp2(s - jnp.maximum(jnp.max(s, axis=0, keepdims=True), FLOOR))
        den = jnp.sum(e, axis=0, keepdims=True)
        pc = e / jnp.where(den > 0.0, den, 1.0)
        pc_sum = pc_sum + pc
        o_cmp.append(jnp.dot(vcT_ref[...], pc.astype(bf16), preferred_element_type=f32))
    nb = ovlT_ref.shape[0]
    imp = jnp.zeros((nb, TQ), f32)
    for part in _split3(pc_sum):
        imp = imp + jnp.dot(ovlT_ref[...], part, preferred_element_type=f32)
    blk = _row_iota((nb, 1))
    blk_f = blk.astype(f32)
    cur = qpos // SLC_LEN
    forced = (blk == 0) | (blk == cur) | (blk == cur - 1)
    imp = jnp.where(forced, jnp.inf, imp)
    imp = jnp.where((blk * SLC_LEN <= qpos) & (blk < n_slc), imp, -jnp.inf)

    def pick(_, st):
        imp, sel = st
        best = jnp.max(imp, axis=0, keepdims=True)
        first = jnp.min(jnp.where(imp == best, blk_f, float(nb)), axis=0, keepdims=True)
        hit = blk_f == first
        return jnp.where(hit, -jnp.inf, imp), jnp.where(hit & (best > -jnp.inf), 1.0, sel)

    _, sel = lax.fori_loop(0, topn, pick, (imp, jnp.zeros((nb, TQ), f32)))

    krow = _row_iota((TK, 1))
    qcol = _col_iota((1, TQ))
    causal = krow <= qcol

    selb = sel.astype(bf16)

    def slc_keep(kc, kind):
        keep = jnp.dot(exp_ref[kc], selb, preferred_element_type=f32) > 0.5
        return keep & causal if kind == 0 else keep

    def slc_finish(kind, h, s, keep):
        if kind < 2:
            s = s + bias_ref[h, kind]
        return jnp.where(keep, s, NEG)

    _attend(qi, HG, lambda kc, h: jnp.dot(_k_chunk(ks_ref, kc), qs[h], preferred_element_type=f32),
            slc_finish, lambda kc, h: vsT_ref[kc], m_ref, l_ref, acc_ref, shared=slc_keep)
    o_slc = [_flash_result(h, l_ref, acc_ref) for h in range(HG)]

    def win_finish(kind, h, s, _):
        if kind < 2:
            s = s + bias_ref[h, kind]
        if kind == 0:
            return jnp.where(causal, s, NEG)
        return jnp.where(krow > qcol, s, NEG) if kind == 2 else s

    _attend(qi, HG, lambda kc, h: jnp.dot(_k_chunk(kw_ref, kc), qs[h], preferred_element_type=f32),
            win_finish, lambda kc, h: vwT_ref[kc], m_ref, l_ref, acc_ref,
            first=jnp.maximum(qi - WIN // TK, 0))
    o_win = [_flash_result(h, l_ref, acc_ref) for h in range(HG)]

    gates = jax.nn.sigmoid(gT_ref[...])
    heads = []
    for h in range(HG):
        c = S_BG + 3 * h
        heads.append(gates[c:c + 1, :] * o_cmp[h] + gates[c + 1:c + 2, :] * o_slc[h]
                     + gates[c + 2:c + 3, :] * o_win[h])
    o_ref[...] = jnp.concatenate(heads, axis=0).T.astype(o_ref.dtype)


def _nsa(hr, ht, hs, kc, vcT, ovlT, expand, bias):
    B, L, _ = hr.shape
    nch = L // TK
    n = kc.shape[1]
    n_slc = L // SLC_LEN
    topn = min(SLC_TOPN, n_slc)
    return pl.pallas_call(
        functools.partial(_nsa_kernel, n_slc=n_slc, topn=topn),
        grid=(B, L // TQ),
        in_specs=[
            _q_spec(T_BQ), _q_spec(0, NS),
            pl.BlockSpec((None, n, LANES), lambda b, i: (b, 0, 0)),
            pl.BlockSpec((None, HEAD_DIM, n), lambda b, i: (b, 0, 0)),
            _k_spec(L, R_BKS, LANES), _vT_spec(nch, T_BVS, HEAD_DIM),
            _k_spec(L, R_BKW, LANES), _vT_spec(nch, T_BVW, HEAD_DIM),
            pl.BlockSpec(ovlT.shape, lambda b, i: (0, 0)),
            pl.BlockSpec(expand.shape, lambda b, i: (0, 0, 0)),
            _BIAS_SPEC,
        ],
        out_specs=_OUT_SPEC,
        out_shape=jax.ShapeDtypeStruct((B, L, GW), bf16),
        scratch_shapes=_flash_scratch(HG),
        compiler_params=_cparams("parallel", "arbitrary"),
        name="nsa",
    )(ht, hs, kc, vcT, hr, ht, hr, ht, ovlT, expand, bias)


def _nsa_tables(L):
    n = L // CMP_STRIDE
    n_slc = L // SLC_LEN
    nb = -(-n_slc // 16) * 16
    c0 = jnp.arange(n, dtype=i32)[None, :] * CMP_STRIDE
    s0 = jnp.arange(nb, dtype=i32)[:, None] * SLC_LEN
    ovlT = (c0 <= s0 + SLC_LEN - 1) & (c0 + CMP_LEN - 1 >= s0)
    ovlT = ovlT & (jnp.arange(n)[None, :] < n - 1) & (jnp.arange(nb)[:, None] < n_slc)
    tokblk = jnp.arange(L, dtype=i32) // SLC_LEN
    expand = tokblk[:, None] == jnp.arange(nb, dtype=i32)[None, :]
    return ovlT.astype(bf16), expand.reshape(L // TK, TK, nb).astype(bf16)


def _outproj_kernel(oa_ref, ob_ref, oc_ref, od_ref, w_ref, x_ref, g_ref, b_ref, y_ref):
    mix = jnp.zeros(x_ref.shape, f32)
    for n, o_ref in enumerate((oa_ref, ob_ref, oc_ref, od_ref)):
        mix = mix + jnp.dot(o_ref[...], w_ref[n * GW:(n + 1) * GW, :], preferred_element_type=f32)
    y_ref[...] = _layer_norm(ALPHA * x_ref[...] + mix, g_ref[...], b_ref[...])


def _outproj(oa, ob, oc, od, w, x, g, b, tm=512):
    M, D = x.shape
    ospec = pl.BlockSpec((tm, GW), lambda i: (i, 0))
    vec = pl.BlockSpec((1, D), lambda i: (0, 0))
    return pl.pallas_call(
        _outproj_kernel,
        grid=(M // tm,),
        in_specs=[ospec, ospec, ospec, ospec, pl.BlockSpec((D, D), lambda i: (0, 0)),
                  pl.BlockSpec((tm, D), lambda i: (i, 0)), vec, vec],
        out_specs=pl.BlockSpec((tm, D), lambda i: (i, 0)),
        out_shape=jax.ShapeDtypeStruct((M, D), f32),
        compiler_params=_cparams("parallel"),
        name="outproj_ln",
    )(oa, ob, oc, od, w, x, g, b)


def _memkv_kernel(mem_ref, wkT_ref, wv_ref, kT_ref, v_ref):
    mb = mem_ref[...].astype(bf16)
    kT_ref[...] = lax.dot_general(wkT_ref[...], mb, (((1,), (1,)), ((), ())),
                                  preferred_element_type=f32).astype(bf16)
    v_ref[...] = jnp.dot(mb, wv_ref[...], preferred_element_type=f32).astype(bf16)


def _memkv(mem, wkT, wv):
    B, N, D = mem.shape
    wspec = pl.BlockSpec((D, D), lambda b: (0, 0))
    return pl.pallas_call(
        _memkv_kernel,
        grid=(B,),
        in_specs=[pl.BlockSpec((None, N, D), lambda b: (b, 0, 0)), wspec, wspec],
        out_specs=[pl.BlockSpec((None, D, N), lambda b: (b, 0, 0)),
                   pl.BlockSpec((None, N, D), lambda b: (b, 0, 0))],
        out_shape=[jax.ShapeDtypeStruct((B, D, N), bf16), jax.ShapeDtypeStruct((B, N, D), bf16)],
        compiler_params=_cparams("parallel"),
        name="cross_kv",
    )(mem, wkT, wv)


def _cross_kernel(x_ref, wq_ref, kT_ref, v_ref, wo_ref, g_ref, b_ref, y_ref):
    x = x_ref[...]
    q = jnp.dot(x.astype(bf16), wq_ref[...], preferred_element_type=f32).astype(bf16)
    outs = []
    for h in range(CROSS_HEADS):
        sl = slice(h * CROSS_DIM, (h + 1) * CROSS_DIM)
        s = jnp.dot(q[:, sl], kT_ref[sl, :], preferred_element_type=f32) * CROSS_DIM ** -0.5
        e = jnp.exp(s - jnp.max(s, axis=1, keepdims=True))
        p = e / jnp.sum(e, axis=1, keepdims=True)
        outs.append(jnp.dot(p.astype(bf16), v_ref[:, sl], preferred_element_type=f32).astype(bf16))
    o = jnp.concatenate(outs, axis=1)
    y = ALPHA * x + jnp.dot(o, wo_ref[...], preferred_element_type=f32)
    y_ref[...] = _layer_norm(y, g_ref[...], b_ref[...])


def _cross(x, wq, kT, v, wo, g, b, tm=512):
    B, L, D = x.shape
    N = v.shape[1]
    wspec = pl.BlockSpec((D, D), lambda bb, i: (0, 0))
    vec = pl.BlockSpec((1, D), lambda bb, i: (0, 0))
    return pl.pallas_call(
        _cross_kernel,
        grid=(B, L // tm),
        in_specs=[pl.BlockSpec((None, tm, D), lambda bb, i: (bb, i, 0)), wspec,
                  pl.BlockSpec((None, D, N), lambda bb, i: (bb, 0, 0)),
                  pl.BlockSpec((None, N, D), lambda bb, i: (bb, 0, 0)), wspec, vec, vec],
        out_specs=pl.BlockSpec((None, tm, D), lambda bb, i: (bb, i, 0)),
        out_shape=jax.ShapeDtypeStruct((B, L, D), f32),
        compiler_params=_cparams("parallel", "parallel"),
        name="cross_ln",
    )(x, wq, kT, v, wo, g, b)


def _mlp_kernel(x_ref, w1_ref, w2_ref, g_ref, b_ref, y_ref, acc_ref):
    j = pl.program_id(1)

    @pl.when(j == 0)
    def _():
        acc_ref[...] = jnp.zeros(acc_ref.shape, f32)

    hdn = jnp.dot(x_ref[...].astype(bf16), w1_ref[...], preferred_element_type=f32)
    hdn = jnp.square(jnp.maximum(hdn, 0.0)).astype(bf16)
    acc_ref[...] += jnp.dot(hdn, w2_ref[...], preferred_element_type=f32)

    @pl.when(j == pl.num_programs(1) - 1)
    def _():
        y_ref[...] = _layer_norm(ALPHA * x_ref[...] + acc_ref[...], g_ref[...], b_ref[...])


def _mlp(x, w1, w2, g, b, tm=1024, tf=1024):
    M, D = x.shape
    F = w1.shape[1]
    vec = pl.BlockSpec((1, D), lambda i, j: (0, 0))
    return pl.pallas_call(
        _mlp_kernel,
        grid=(M // tm, F // tf),
        in_specs=[pl.BlockSpec((tm, D), lambda i, j: (i, 0)),
                  pl.BlockSpec((D, tf), lambda i, j: (0, j)),
                  pl.BlockSpec((tf, D), lambda i, j: (j, 0)), vec, vec],
        out_specs=pl.BlockSpec((tm, D), lambda i, j: (i, 0)),
        out_shape=jax.ShapeDtypeStruct((M, D), f32),
        scratch_shapes=[pltpu.VMEM((tm, D), f32)],
        compiler_params=_cparams("parallel", "arbitrary"),
        name="mlp_ln",
    )(x, w1, w2, g, b)


def _split_w_in(w):
    sizes = (GW, GW, GW, IDX_HEADS * IDX_DIM, IDX_DIM, IDX_HEADS,
             GW, HEAD_DIM, HEAD_DIM, HEAD_DIM, HEAD_DIM, HEAD_DIM, HEAD_DIM, 3 * HG,
             GW, GW, GW, GW, GW, GW)
    offs = [0]
    for s in sizes:
        offs.append(offs[-1] + s)
    return [w[:, offs[n]:offs[n + 1]] for n in range(len(sizes))]


def _layout_w_in(w):
    (a_q, a_k, a_v, a_qi, a_ki, a_w, b_q, b_kc, b_vc, b_ks, b_vs, b_kw, b_vw, b_g,
     c_q, c_k, c_v, d_q, d_k, d_v) = _split_w_in(w)
    twice = lambda t: jnp.concatenate([t, t], axis=1)
    c64, c32 = HEAD_DIM ** -0.5 * LOG2E, DIFF_HALF ** -0.5 * LOG2E
    wr = jnp.concatenate([a_k, c_k, d_k, twice(b_ks), twice(b_kw), twice(a_ki), b_kc, b_vc], axis=1)
    wt = jnp.concatenate([a_q * c64, a_v, a_qi, b_q * c64, c_q * c64, c_v, d_q * c32, d_v, b_vs, b_vw], axis=1).T
    ws = jnp.concatenate([a_w, b_g, jnp.zeros((w.shape[0], NS - IDX_HEADS - 3 * HG), w.dtype)], axis=1).T
    assert wr.shape[1] == NR and wt.shape[0] == NT
    return wr.astype(bf16), wt.astype(bf16), ws.astype(bf16)


def _mixers(x, bias, w_in_l, pos_k, pos_v, w1_k, w2_k, w1_v, w2_v, lq1, lk1, lq2, lk2, diff_g, lam_init):
    B, L, _ = x.shape
    wr, wt, ws = _layout_w_in(w_in_l)
    hr, ht, hs = _project(x, wr, wt, ws)
    o_a = _dsa(hr, ht, hs, bias[0:HG])

    n = L // CMP_STRIDE
    half = CMP_STRIDE * HEAD_DIM
    xk = hr[:, :, R_BKVC:R_BKVC + HEAD_DIM].reshape(B, n, half)
    xv = hr[:, :, R_BKVC + HEAD_DIM:R_BKVC + 2 * HEAD_DIM].reshape(B, n, half)
    kc, vcT = _compress(
        xk, xv, pos_k.reshape(2, half), pos_v.reshape(2, half),
        w1_k.reshape(2, half, HEAD_DIM).astype(bf16), w1_v.reshape(2, half, HEAD_DIM).astype(bf16),
        jnp.concatenate([w2_k, w2_k], axis=1).astype(bf16), w2_v.T.astype(bf16))
    ovlT, expand = _nsa_tables(L)
    o_b = _nsa(hr, ht, hs, kc, vcT, ovlT, expand, bias[HG:2 * HG])
    o_c = _moba(hr, ht, bias[2 * HG:3 * HG])
    o_d = _diff(hr, ht, bias[3 * HG:4 * HG], lq1.reshape(1, -1), lk1.reshape(1, -1),
                lq2.reshape(1, -1), lk2.reshape(1, -1), jnp.full((1, 1), lam_init, f32),
                diff_g.reshape(-1, 1))
    return o_a, o_b, o_c, o_d


def kernel(x, mem, rel_bias, w_in, w_out, nsa_pos_k, nsa_pos_v, nsa_w1_k, nsa_w2_k, nsa_w1_v, nsa_w2_v, diff_lq1, diff_lk1, diff_lq2, diff_lk2, diff_g, ln1_g, ln1_b, xq, xk, xv, xo, ln2_g, ln2_b, mlp_w1, mlp_w2, ln3_g, ln3_b):
    B, L, D = x.shape
    bias = _bias_tiles(rel_bias)
    for l in range(DEPTH):
        lam_init = 0.8 - 0.6 * math.exp(-0.3 * l)
        o_a, o_b, o_c, o_d = _mixers(
            x, bias, w_in[l], nsa_pos_k[l], nsa_pos_v[l], nsa_w1_k[l], nsa_w2_k[l],
            nsa_w1_v[l], nsa_w2_v[l], diff_lq1[l], diff_lk1[l], diff_lq2[l], diff_lk2[l],
            diff_g[l], lam_init)
        flat = lambda t: t.reshape(B * L, -1)
        row = lambda t: t.reshape(1, D)
        x2 = _outproj(flat(o_a), flat(o_b), flat(o_c), flat(o_d), w_out[l].astype(bf16),
                      flat(x), row(ln1_g[l]), row(ln1_b[l]))
        kT, v = _memkv(mem, xk[l].T.astype(bf16), xv[l].astype(bf16))
        x3 = _cross(x2.reshape(B, L, D), xq[l].astype(bf16), kT, v, xo[l].astype(bf16),
                    row(ln2_g[l]), row(ln2_b[l]))
        x = _mlp(flat(x3), mlp_w1[l].astype(bf16), mlp_w2[l].astype(bf16),
                 row(ln3_g[l]), row(ln3_b[l])).reshape(B, L, D)
    return x
```

```python
import functools
import math

import jax
import jax.numpy as jnp
from jax import lax
from jax.experimental import pallas as pl
from jax.experimental.pallas import tpu as pltpu

f32 = jnp.float32
bf16 = jnp.bfloat16
i32 = jnp.int32
i16 = jnp.int16

D_MODEL = 1024
DEPTH = 4
HEAD_DIM = 64
HG = 4
GW = HG * HEAD_DIM
REL_BUCKETS = 32
REL_MAX_DIST = 128
IDX_HEADS = 8
IDX_DIM = 64
DSA_TOPK = 256
CMP_LEN = 32
CMP_STRIDE = 16
SLC_LEN = 64
SLC_TOPN = 16
WIN = 512
MOBA_BLOCK = 256
MOBA_TOPK = 3
DIFF_HALF = HEAD_DIM // 2
CROSS_HEADS = 4
CROSS_DIM = D_MODEL // CROSS_HEADS
D_FF = 4 * D_MODEL
ALPHA = (2 * DEPTH) ** 0.25

NEG = -1e30
FLOOR = -1e29
INT_MIN = -(2 ** 31)
HALF = 2 ** 15
LANES = 128
TQ = 256
TK = 256
VMEM_LIMIT = 56 * 1024 * 1024
LOG2E = math.log2(math.e)
FAR_CHAINS = 16
MATMULS_AHEAD = 6

R_AK, R_CK, R_DK, R_BKS, R_BKW, R_AKI, R_BKVC = 0, 256, 512, 768, 896, 1024, 1152
NR = 1280
T_AQ, T_AV, T_AQI, T_BQ, T_CQ, T_CV, T_DQ, T_DV, T_BVS, T_BVW = (
    0, 256, 512, 1024, 1280, 1536, 1792, 2048, 2304, 2368)
NT = 2432
S_AW, S_BG, NS = 0, 8, 32


def _cparams(*sem):
    return pltpu.CompilerParams(dimension_semantics=sem, vmem_limit_bytes=VMEM_LIMIT)


def _rel_bucket(dist):
    n = jnp.maximum(dist, 0)
    max_exact = REL_BUCKETS // 2
    nf = jnp.maximum(n, max_exact).astype(f32)
    large = max_exact + (jnp.log(nf / max_exact) / math.log(REL_MAX_DIST / max_exact)
                         * (REL_BUCKETS - max_exact)).astype(i32)
    large = jnp.minimum(large, REL_BUCKETS - 1)
    return jnp.where(n < max_exact, n, large)


def _bias_tiles(rel_bias):
    assert 2 * TQ - TK + 1 >= REL_MAX_DIST
    d = jnp.arange(TQ, dtype=i32)[None, :] - jnp.arange(TK, dtype=i32)[:, None]
    bucket = jnp.stack([_rel_bucket(d + off) for off in (0, TQ)])
    far = rel_bias[_rel_bucket(jnp.int32(2 * TQ))]
    table = ((rel_bias - far) * LOG2E).astype(f32)
    onehot = (bucket[None] == jnp.arange(REL_BUCKETS, dtype=i32)[:, None, None, None]).astype(f32)
    return jnp.einsum('bntq,bh->hntq', onehot, table, precision=lax.Precision.HIGHEST)


def _row_iota(shape):
    return lax.broadcasted_iota(i32, shape, 0)


def _col_iota(shape):
    return lax.broadcasted_iota(i32, shape, 1)


def _flash_init(m_ref, l_ref, acc_ref):
    m_ref[...] = jnp.full(m_ref.shape, NEG, f32)
    l_ref[...] = jnp.zeros(l_ref.shape, f32)
    acc_ref[...] = jnp.zeros(acc_ref.shape, f32)


def _fold(x, op, rows):
    while x.shape[0] > rows:
        half = x.shape[0] // 2
        x = op(x[:half], x[half:])
    return x


def _flash_step(c, s, vT, m_ref, l_ref, acc_ref, pen=None):
    m_prev = m_ref[c]
    m_chunk = jnp.max(_fold(s, jnp.maximum, 8), axis=0, keepdims=True)
    if pen is not None:
        m_chunk = m_chunk + pen
    m_new = jnp.maximum(m_prev, m_chunk)
    m_use = jnp.maximum(m_new, FLOOR)
    alpha = jnp.exp2(jnp.maximum(m_prev, FLOOR) - m_use)
    p = jnp.exp2(s - (m_use if pen is None else m_use - pen))
    l_ref[c] = alpha * l_ref[c] + jnp.sum(_fold(p, jnp.add, 8), axis=0, keepdims=True)
    acc_ref[c] = alpha * acc_ref[c] + jnp.dot(vT, p.astype(bf16), preferred_element_type=f32)
    m_ref[c] = m_new


def _flash_result(c, l_ref, acc_ref):
    l = l_ref[c]
    return jnp.where(l > 0.0, acc_ref[c] / jnp.where(l > 0.0, l, 1.0), 0.0)


def _attend(qi, n, raw, finish, vT, m_ref, l_ref, acc_ref, first=0, shared=None):
    def run(chunks):
        items = [(kc, kind, c) for kc, kind in chunks for c in range(n)]
        ctx, logits = {}, {}

        def issue(j):
            kc, kind, c = items[j]
            if c == 0 and shared is not None:
                ctx[j // n] = shared(kc, kind)
            logits[j] = raw(kc, c)

        for j in range(min(MATMULS_AHEAD, len(items))):
            issue(j)
        for j, (kc, kind, c) in enumerate(items):
            if j + MATMULS_AHEAD < len(items):
                issue(j + MATMULS_AHEAD)
            s = finish(kind, c, logits.pop(j), ctx.get(j // n))
            s, pen = s if isinstance(s, tuple) else (s, None)
            _flash_step(c, s, vT(kc, c), m_ref, l_ref, acc_ref, pen)

    _flash_init(m_ref, l_ref, acc_ref)
    n_far = jnp.maximum(qi - 1 - first, 0)
    group = max(1, FAR_CHAINS // n)

    def far_group(j, carry):
        run([(first + group * j + t, 2) for t in range(group)])
        return carry

    lax.fori_loop(0, n_far // group, far_group, 0)
    size = group // 2
    while size >= 1:
        done = first + (n_far // (2 * size)) * (2 * size)

        @pl.when((n_far // size) % 2 == 1)
        def _(done=done, size=size):
            run([(done + t, 2) for t in range(size)])

        size //= 2

    @pl.when(qi >= 1)
    def _():
        run([(qi - 1, 1), (qi, 0)])

    @pl.when(qi == 0)
    def _():
        run([(qi, 0)])


def _pair_rows(qT_ref, h, width=HEAD_DIM, offset=0):
    blk = qT_ref[(h // 2) * LANES:(h // 2 + 1) * LANES, :]
    r = _row_iota((LANES, 1))
    lo = (h % 2) * HEAD_DIM + offset
    return jnp.where((r >= lo) & (r < lo + width), blk, jnp.zeros_like(blk))


def _layer_norm(y, g, b):
    mu = jnp.mean(y, axis=-1, keepdims=True)
    yc = y - mu
    var = jnp.mean(yc * yc, axis=-1, keepdims=True)
    return yc * lax.rsqrt(var + 1e-5) * g + b


def _proj_kernel(x_ref, wr_ref, wt_ref, ws_ref, hr_ref, ht_ref, hs_ref):
    xb = x_ref[...].astype(bf16)
    nt = (((1,), (1,)), ((), ()))
    hr_ref[...] = jnp.dot(xb, wr_ref[...], preferred_element_type=f32).astype(bf16)
    ht_ref[...] = lax.dot_general(wt_ref[...], xb, nt, preferred_element_type=f32).astype(bf16)
    hs_ref[...] = lax.dot_general(ws_ref[...], xb, nt, preferred_element_type=f32)


def _project(x, wr, wt, ws):
    B, L, D = x.shape
    nch = L // TK
    return pl.pallas_call(
        _proj_kernel,
        grid=(B, nch),
        in_specs=[
            pl.BlockSpec((None, TK, D), lambda b, i: (b, i, 0)),
            pl.BlockSpec((D, NR), lambda b, i: (0, 0)),
            pl.BlockSpec((NT, D), lambda b, i: (0, 0)),
            pl.BlockSpec((NS, D), lambda b, i: (0, 0)),
        ],
        out_specs=[
            pl.BlockSpec((None, TK, NR), lambda b, i: (b, i, 0)),
            pl.BlockSpec((None, None, NT, TK), lambda b, i: (b, i, 0, 0)),
            pl.BlockSpec((None, None, NS, TK), lambda b, i: (b, i, 0, 0)),
        ],
        out_shape=[
            jax.ShapeDtypeStruct((B, L, NR), bf16),
            jax.ShapeDtypeStruct((B, nch, NT, TK), bf16),
            jax.ShapeDtypeStruct((B, nch, NS, TK), f32),
        ],
        compiler_params=_cparams("parallel", "parallel"),
        name="proj",
    )(x, wr, wt, ws)


def _q_spec(off, rows=GW):
    return pl.BlockSpec((None, None, rows, TQ), lambda b, i: (b, i, off // rows, 0))


def _vT_spec(nch, off, rows=GW):
    return pl.BlockSpec((None, nch, rows, TK), lambda b, i: (b, 0, off // rows, 0))


def _k_spec(L, off, cols=GW):
    return pl.BlockSpec((None, L, cols), lambda b, i: (b, 0, off // cols))


_BIAS_SPEC = pl.BlockSpec((HG, 2, TK, TQ), lambda b, i: (0, 0, 0, 0))
_OUT_SPEC = pl.BlockSpec((None, TQ, GW), lambda b, i: (b, i, 0))


def _flash_scratch(chains, dv=HEAD_DIM):
    return [pltpu.VMEM((chains, 1, TQ), f32), pltpu.VMEM((chains, 1, TQ), f32),
            pltpu.VMEM((chains, dv, TQ), f32)]


def _k_chunk(k_ref, kc, pair=None):
    rows = pl.ds(pl.multiple_of(kc * TK, TK), TK)
    if pair is None:
        return k_ref[rows, :]
    return k_ref[rows, pair * LANES:(pair + 1) * LANES]


def _dsa_kernel(qT_ref, qiT_ref, wT_ref, k_ref, ki_ref, vT_ref, bias_ref, o_ref,
                key_ref, hi_ref, lo_ref, m_ref, l_ref, acc_ref, *, topk):
    qi = pl.program_id(1)
    nch = qi + 1
    qpos = qi * TQ + _col_iota((1, TQ))
    krow = _row_iota((TK, 1))
    wT = wT_ref[...] * (IDX_DIM ** -0.5 * IDX_HEADS ** -0.5)
    qidx = [_pair_rows(qiT_ref, h) for h in range(IDX_HEADS)]

    def score_chunk(kc, carry):
        ki2 = _k_chunk(ki_ref, kc)
        s = jnp.zeros((TK, TQ), f32)
        for h in range(IDX_HEADS):
            r = jnp.dot(ki2, qidx[h], preferred_element_type=f32)
            s = s + wT[S_AW + h:S_AW + h + 1, :] * jnp.maximum(r, 0.0)
        s = jnp.where(s == 0.0, 0.0, s)
        bits = lax.bitcast_convert_type(s, i32)
        key = bits ^ ((bits >> 31) & 0x7FFFFFFF)
        key = jnp.where(kc * TK + krow <= qpos, key, INT_MIN)
        key_ref[kc] = key
        hi_ref[kc] = (key >> 16).astype(i16)
        lo_ref[kc] = ((key & 0xFFFF) - HALF).astype(i16)
        return carry

    lax.fori_loop(0, nch, score_chunk, 0)

    def count(ref, pred):
        one = jnp.ones((), ref.dtype)
        zero = jnp.zeros((), ref.dtype)
        rows = 8 * 4 // ref.dtype.itemsize

        def body(kc, acc):
            hit = jnp.where(pred(ref[kc]), one, zero)
            return acc + _fold(hit, jnp.add, rows).astype(f32)
        acc = lax.fori_loop(0, nch, body, jnp.zeros((rows, TQ), f32))
        return jnp.sum(acc, axis=0, keepdims=True)

    def kth_largest_i16(ref, want):
        def bisect(i, t_u):
            cand_u = t_u | jnp.left_shift(jnp.int32(1), 15 - i)
            cand = (cand_u - HALF).astype(i16)
            return jnp.where(count(ref, lambda k: k >= cand) >= want, cand_u, t_u)
        return lax.fori_loop(0, 16, bisect, jnp.zeros((1, TQ), i32)) - HALF

    hi_t = kth_largest_i16(hi_ref, topk)
    hi_t16 = hi_t.astype(i16)
    above = count(hi_ref, lambda k: k > hi_t16)

    def keep_low(kc, carry):
        lo_ref[kc] = jnp.where(hi_ref[kc] == hi_t16, lo_ref[kc], jnp.full((), -HALF, i16))
        return carry

    lax.fori_loop(0, nch, keep_low, 0)
    lo_t = kth_largest_i16(lo_ref, topk - above)
    thr = jnp.maximum((hi_t << 16) | (lo_t + HALF), INT_MIN + 1)
    cnt_ge = count(key_ref, lambda k: k >= thr)

    @pl.when(jnp.max(cnt_ge) > topk)
    def _():
        need = topk - (cnt_ge - count(key_ref, lambda k: k == thr))
        tri = (_col_iota((TK, TK)) <= _row_iota((TK, TK))).astype(bf16)

        def body(kc, seen):
            k = key_ref[kc]
            tie = k == thr
            tief = jnp.where(tie, 1.0, 0.0)
            pref = jnp.dot(tri, tief.astype(bf16), preferred_element_type=f32) + seen
            key_ref[kc] = jnp.where(tie & (pref > need), INT_MIN, k)
            return seen + jnp.sum(tief, axis=0, keepdims=True)

        lax.fori_loop(0, nch, body, jnp.zeros((1, TQ), f32))

    qs = [_pair_rows(qT_ref, h) for h in range(HG)]

    def finish(kind, h, s, keep):
        s = s * (HEAD_DIM ** -0.5 * LOG2E)
        if kind < 2:
            s = s + bias_ref[h, kind]
        return jnp.where(keep, s, NEG)

    _attend(qi, HG, lambda kc, h: jnp.dot(_k_chunk(k_ref, kc, h // 2), qs[h], preferred_element_type=f32),
            finish, lambda kc, h: vT_ref[kc, h * HEAD_DIM:(h + 1) * HEAD_DIM, :],
            m_ref, l_ref, acc_ref, shared=lambda kc, kind: key_ref[kc] >= thr)
    outT = jnp.concatenate([_flash_result(h, l_ref, acc_ref) for h in range(HG)], axis=0)
    o_ref[...] = outT.T.astype(o_ref.dtype)


def _dsa(hr, ht, hs, bias):
    B, L, _ = hr.shape
    nch = L // TK
    topk = min(DSA_TOPK, L // 4)
    return pl.pallas_call(
        functools.partial(_dsa_kernel, topk=topk),
        grid=(B, L // TQ),
        in_specs=[
            _q_spec(T_AQ), _q_spec(T_AQI, 2 * GW), _q_spec(0, NS),
            _k_spec(L, R_AK), _k_spec(L, R_AKI, LANES), _vT_spec(nch, T_AV), _BIAS_SPEC,
        ],
        out_specs=_OUT_SPEC,
        out_shape=jax.ShapeDtypeStruct((B, L, GW), bf16),
        scratch_shapes=[pltpu.VMEM((nch, TK, TQ), i32), pltpu.VMEM((nch, TK, TQ), i16),
                        pltpu.VMEM((nch, TK, TQ), i16)] + _flash_scratch(HG),
        compiler_params=_cparams("parallel", "arbitrary"),
        name="dsa",
    )(ht, ht, hs, hr, hr, ht, bias)


def _moba_kernel(qT_ref, k_ref, vT_ref, bias_ref, o_ref, km_ref, sel_ref, m_ref, l_ref, acc_ref, *, nch, topk):
    qi = pl.program_id(1)
    nb = km_ref.shape[0]

    @pl.when(qi == 0)
    def _():
        km_ref[...] = jnp.zeros(km_ref.shape, f32)
        for n in range(nch):
            blk = k_ref[n * MOBA_BLOCK:(n + 1) * MOBA_BLOCK, :].astype(f32)
            km_ref[n:n + 1, :] = jnp.sum(blk, axis=0, keepdims=True) * (1.0 / MOBA_BLOCK)

    km = km_ref[...]
    km_hi = km.astype(bf16)
    km_lo = (km - km_hi.astype(f32)).astype(bf16)
    blk_id = _row_iota((nb, 1))
    blk_f = blk_id.astype(f32)
    for h in range(HG):
        qh = qT_ref[...]
        r = _row_iota((GW, 1))
        qh = jnp.where((r >= h * HEAD_DIM) & (r < (h + 1) * HEAD_DIM), qh, jnp.zeros_like(qh))
        gate = (jnp.dot(km_hi, qh, preferred_element_type=f32)
                + jnp.dot(km_lo, qh, preferred_element_type=f32))
        gate = jnp.where(blk_id < qi, gate, -jnp.inf)
        sel = jnp.zeros((nb, TQ), f32)
        for _ in range(topk):
            best = jnp.max(gate, axis=0, keepdims=True)
            first = jnp.min(jnp.where(gate == best, blk_f, float(nb)), axis=0, keepdims=True)
            hit = blk_f == first
            sel = jnp.where(hit & (best > -jnp.inf), 1.0, sel)
            gate = jnp.where(hit, -jnp.inf, gate)
        sel_ref[h] = sel

    qs = [_pair_rows(qT_ref, h) for h in range(HG)]
    causal = _row_iota((TK, 1)) <= _col_iota((1, TQ))

    def finish(kind, h, s, kc):
        s = s * (HEAD_DIM ** -0.5 * LOG2E)
        if kind < 2:
            s = s + bias_ref[h, kind]
        if kind == 0:
            return jnp.where(causal, s, NEG)
        return s, (1.0 - sel_ref[h, pl.ds(kc, 1), :]) * NEG

    _attend(qi, HG, lambda kc, h: jnp.dot(_k_chunk(k_ref, kc, h // 2), qs[h], preferred_element_type=f32),
            finish, lambda kc, h: vT_ref[kc, h * HEAD_DIM:(h + 1) * HEAD_DIM, :],
            m_ref, l_ref, acc_ref, shared=lambda kc, kind: kc)
    outT = jnp.concatenate([_flash_result(h, l_ref, acc_ref) for h in range(HG)], axis=0)
    o_ref[...] = outT.T.astype(o_ref.dtype)


def _moba(hr, ht, bias):
    B, L, _ = hr.shape
    nch = L // TK
    nb = -(-nch // 8) * 8
    topk = min(MOBA_TOPK, nch - 1)
    return pl.pallas_call(
        functools.partial(_moba_kernel, nch=nch, topk=topk),
        grid=(B, L // TQ),
        in_specs=[_q_spec(T_CQ), _k_spec(L, R_CK), _vT_spec(nch, T_CV), _BIAS_SPEC],
        out_specs=_OUT_SPEC,
        out_shape=jax.ShapeDtypeStruct((B, L, GW), bf16),
        scratch_shapes=[pltpu.VMEM((nb, GW), f32), pltpu.VMEM((HG, nb, TQ), f32)] + _flash_scratch(HG),
        compiler_params=_cparams("parallel", "arbitrary"),
        name="moba",
    )(ht, hr, ht, bias)


def _diff_kernel(qT_ref, k_ref, vT_ref, bias_ref, lq1_ref, lk1_ref, lq2_ref, lk2_ref, li_ref, g_ref,
                 o_ref, m_ref, l_ref, acc_ref):
    qi = pl.program_id(1)
    lam_init = li_ref[...]
    lam = (jnp.exp(jnp.sum(lq1_ref[...] * lk1_ref[...], axis=1, keepdims=True))
           - jnp.exp(jnp.sum(lq2_ref[...] * lk2_ref[...], axis=1, keepdims=True)) + lam_init)
    qs = [_pair_rows(qT_ref, h, DIFF_HALF, c * DIFF_HALF) for h in range(HG) for c in range(2)]
    causal = _row_iota((TK, 1)) <= _col_iota((1, TQ))

    def finish(kind, c, s, _):
        s = s * (DIFF_HALF ** -0.5 * LOG2E)
        if kind < 2:
            s = s + bias_ref[c // 2, kind]
        return jnp.where(causal, s, NEG) if kind == 0 else s

    _attend(qi, 2 * HG, lambda kc, c: jnp.dot(_k_chunk(k_ref, kc, c // 4), qs[c], preferred_element_type=f32),
            finish, lambda kc, c: vT_ref[kc, (c // 2) * HEAD_DIM:(c // 2 + 1) * HEAD_DIM, :],
            m_ref, l_ref, acc_ref)
    heads = []
    for h in range(HG):
        o = _flash_result(2 * h, l_ref, acc_ref) - lam * _flash_result(2 * h + 1, l_ref, acc_ref)
        ms = jnp.mean(o * o, axis=0, keepdims=True)
        heads.append(o * lax.rsqrt(ms + 1e-6) * g_ref[...] * (1.0 - lam_init))
    o_ref[...] = jnp.concatenate(heads, axis=0).T.astype(o_ref.dtype)


def _diff(hr, ht, bias, lq1, lk1, lq2, lk2, lam_init, g_col):
    B, L, _ = hr.shape
    nch = L // TK
    vec = pl.BlockSpec((1, DIFF_HALF), lambda b, i: (0, 0))
    return pl.pallas_call(
        _diff_kernel,
        grid=(B, L // TQ),
        in_specs=[
            _q_spec(T_DQ), _k_spec(L, R_DK), _vT_spec(nch, T_DV), _BIAS_SPEC,
            vec, vec, vec, vec,
            pl.BlockSpec((1, 1), lambda b, i: (0, 0)),
            pl.BlockSpec((HEAD_DIM, 1), lambda b, i: (0, 0)),
        ],
        out_specs=_OUT_SPEC,
        out_shape=jax.ShapeDtypeStruct((B, L, GW), bf16),
        scratch_shapes=_flash_scratch(2 * HG),
        compiler_params=_cparams("parallel", "arbitrary"),
        name="diff",
    )(ht, hr, ht, bias, lq1, lk1, lq2, lk2, lam_init, g_col)


def _gelu_tanh(x):
    return 0.5 * x * (1.0 + jnp.tanh(math.sqrt(2.0 / math.pi) * (x + 0.044715 * (x * x * x))))


def _compress_kernel(xk_ref, xv_ref, pk_ref, pv_ref, w1k_ref, w1v_ref, w2k_ref, w2vT_ref, kc_ref, vcT_ref):
    def pre_act(x_ref, p_ref, w1_ref):
        x = x_ref[...].astype(f32)
        first = jnp.dot((x + p_ref[0:1, :]).astype(bf16), w1_ref[0], preferred_element_type=f32)
        second = jnp.dot((x + p_ref[1:2, :]).astype(bf16), w1_ref[1], preferred_element_type=f32)
        n = first.shape[0]
        return _gelu_tanh(first + pltpu.roll(second, n - 1, 0)).astype(bf16)

    gk = pre_act(xk_ref, pk_ref, w1k_ref)
    kc_ref[...] = jnp.dot(gk, w2k_ref[...], preferred_element_type=f32).astype(bf16)
    gv = pre_act(xv_ref, pv_ref, w1v_ref)
    vcT_ref[...] = lax.dot_general(w2vT_ref[...], gv, (((1,), (1,)), ((), ())),
                                   preferred_element_type=f32).astype(bf16)


def _compress(xk, xv, pk, pv, w1k, w1v, w2k, w2vT):
    B, n, W = xk.shape
    xspec = pl.BlockSpec((None, n, W), lambda b: (b, 0, 0))
    full = lambda a: pl.BlockSpec(a.shape, lambda b: (0,) * a.ndim)
    return pl.pallas_call(
        _compress_kernel,
        grid=(B,),
        in_specs=[xspec, xspec, full(pk), full(pv), full(w1k), full(w1v), full(w2k), full(w2vT)],
        out_specs=[
            pl.BlockSpec((None, n, LANES), lambda b: (b, 0, 0)),
            pl.BlockSpec((None, HEAD_DIM, n), lambda b: (b, 0, 0)),
        ],
        out_shape=[
            jax.ShapeDtypeStruct((B, n, LANES), bf16),
            jax.ShapeDtypeStruct((B, HEAD_DIM, n), bf16),
        ],
        compiler_params=_cparams("parallel"),
        name="nsa_compress",
    )(xk, xv, pk, pv, w1k, w1v, w2k, w2vT)


def _split3(x):
    hi = x.astype(bf16)
    r = x - hi.astype(f32)
    mid = r.astype(bf16)
    lo = (r - mid.astype(f32)).astype(bf16)
    return hi, mid, lo


def _nsa_kernel(qT_ref, gT_ref, kc_ref, vcT_ref, ks_ref, vsT_ref, kw_ref, vwT_ref, ovlT_ref, exp_ref,
                bias_ref, o_ref, m_ref, l_ref, acc_ref, *, n_slc, topn):
    qi = pl.program_id(1)
    ncmp = kc_ref.shape[0]
    qpos = qi * TQ + _col_iota((1, TQ))
    pad = jnp.zeros((LANES - HEAD_DIM, TQ), bf16)
    qs = [jnp.concatenate([qT_ref[h * HEAD_DIM:(h + 1) * HEAD_DIM, :], pad], axis=0)
          for h in range(HG)]

    cmp_ok = _row_iota((ncmp, 1)) * CMP_STRIDE + (CMP_LEN - 1) <= qpos
    o_cmp = []
    pc_sum = jnp.zeros((ncmp, TQ), f32)
    for h in range(HG):
        s = jnp.dot(kc_ref[...], qs[h], preferred_element_type=f32) * (HEAD_DIM ** -0.5 * LOG2E)
        s = jnp.where(cmp_ok, s, NEG)
        e = jnp.exp2(s - jnp.maximum(jnp.max(s, axis=0, keepdims=True), FLOOR))
        den = jnp.sum(e, axis=0, keepdims=True)
        pc = e / jnp.where(den > 0.0, den, 1.0)
        pc_sum = pc_sum + pc
        o_cmp.append(jnp.dot(vcT_ref[...], pc.astype(bf16), preferred_element_type=f32))
    nb = ovlT_ref.shape[0]
    imp = jnp.zeros((nb, TQ), f32)
    for part in _split3(pc_sum):
        imp = imp + jnp.dot(ovlT_ref[...], part, preferred_element_type=f32)
    blk = _row_iota((nb, 1))
    blk_f = blk.astype(f32)
    cur = qpos // SLC_LEN
    forced = (blk == 0) | (blk == cur) | (blk == cur - 1)
    imp = jnp.where(forced, jnp.inf, imp)
    imp = jnp.where((blk * SLC_LEN <= qpos) & (blk < n_slc), imp, -jnp.inf)

    def pick(_, st):
        imp, sel = st
        best = jnp.max(imp, axis=0, keepdims=True)
        first = jnp.min(jnp.where(imp == best, blk_f, float(nb)), axis=0, keepdims=True)
        hit = blk_f == first
        return jnp.where(hit, -jnp.inf, imp), jnp.where(hit & (best > -jnp.inf), 1.0, sel)

    _, sel = lax.fori_loop(0, topn, pick, (imp, jnp.zeros((nb, TQ), f32)))

    krow = _row_iota((TK, 1))
    qcol = _col_iota((1, TQ))
    causal = krow <= qcol

    selb = sel.astype(bf16)

    def slc_keep(kc, kind):
        keep = jnp.dot(exp_ref[kc], selb, preferred_element_type=f32) > 0.5
        return keep & causal if kind == 0 else keep

    def slc_finish(kind, h, s, keep):
        s = s * (HEAD_DIM ** -0.5 * LOG2E)
        if kind < 2:
            s = s + bias_ref[h, kind]
        return jnp.where(keep, s, NEG)

    _attend(qi, HG, lambda kc, h: jnp.dot(_k_chunk(ks_ref, kc), qs[h], preferred_element_type=f32),
            slc_finish, lambda kc, h: vsT_ref[kc], m_ref, l_ref, acc_ref, shared=slc_keep)
    o_slc = [_flash_result(h, l_ref, acc_ref) for h in range(HG)]

    def win_finish(kind, h, s, _):
        s = s * (HEAD_DIM ** -0.5 * LOG2E)
        if kind < 2:
            s = s + bias_ref[h, kind]
        if kind == 0:
            return jnp.where(causal, s, NEG)
        return jnp.where(krow > qcol, s, NEG) if kind == 2 else s

    _attend(qi, HG, lambda kc, h: jnp.dot(_k_chunk(kw_ref, kc), qs[h], preferred_element_type=f32),
            win_finish, lambda kc, h: vwT_ref[kc], m_ref, l_ref, acc_ref,
            first=jnp.maximum(qi - WIN // TK, 0))
    o_win = [_flash_result(h, l_ref, acc_ref) for h in range(HG)]

    gates = jax.nn.sigmoid(gT_ref[...])
    heads = []
    for h in range(HG):
        c = S_BG + 3 * h
        heads.append(gates[c:c + 1, :] * o_cmp[h] + gates[c + 1:c + 2, :] * o_slc[h]
                     + gates[c + 2:c + 3, :] * o_win[h])
    o_ref[...] = jnp.concatenate(heads, axis=0).T.astype(o_ref.dtype)


def _nsa(hr, ht, hs, kc, vcT, ovlT, expand, bias):
    B, L, _ = hr.shape
    nch = L // TK
    n = kc.shape[1]
    n_slc = L // SLC_LEN
    topn = min(SLC_TOPN, n_slc)
    return pl.pallas_call(
        functools.partial(_nsa_kernel, n_slc=n_slc, topn=topn),
        grid=(B, L // TQ),
        in_specs=[
            _q_spec(T_BQ), _q_spec(0, NS),
            pl.BlockSpec((None, n, LANES), lambda b, i: (b, 0, 0)),
            pl.BlockSpec((None, HEAD_DIM, n), lambda b, i: (b, 0, 0)),
            _k_spec(L, R_BKS, LANES), _vT_spec(nch, T_BVS, HEAD_DIM),
            _k_spec(L, R_BKW, LANES), _vT_spec(nch, T_BVW, HEAD_DIM),
            pl.BlockSpec(ovlT.shape, lambda b, i: (0, 0)),
            pl.BlockSpec(expand.shape, lambda b, i: (0, 0, 0)),
            _BIAS_SPEC,
        ],
        out_specs=_OUT_SPEC,
        out_shape=jax.ShapeDtypeStruct((B, L, GW), bf16),
        scratch_shapes=_flash_scratch(HG),
        compiler_params=_cparams("parallel", "arbitrary"),
        name="nsa",
    )(ht, hs, kc, vcT, hr, ht, hr, ht, ovlT, expand, bias)


def _nsa_tables(L):
    n = L // CMP_STRIDE
    n_slc = L // SLC_LEN
    nb = -(-n_slc // 16) * 16
    c0 = jnp.arange(n, dtype=i32)[None, :] * CMP_STRIDE
    s0 = jnp.arange(nb, dtype=i32)[:, None] * SLC_LEN
    ovlT = (c0 <= s0 + SLC_LEN - 1) & (c0 + CMP_LEN - 1 >= s0)
    ovlT = ovlT & (jnp.arange(n)[None, :] < n - 1) & (jnp.arange(nb)[:, None] < n_slc)
    tokblk = jnp.arange(L, dtype=i32) // SLC_LEN
    expand = tokblk[:, None] == jnp.arange(nb, dtype=i32)[None, :]
    return ovlT.astype(bf16), expand.reshape(L // TK, TK, nb).astype(bf16)


def _outproj_kernel(oa_ref, ob_ref, oc_ref, od_ref, w_ref, x_ref, g_ref, b_ref, y_ref):
    mix = jnp.zeros(x_ref.shape, f32)
    for n, o_ref in enumerate((oa_ref, ob_ref, oc_ref, od_ref)):
        mix = mix + jnp.dot(o_ref[...], w_ref[n * GW:(n + 1) * GW, :], preferred_element_type=f32)
    y_ref[...] = _layer_norm(ALPHA * x_ref[...] + mix, g_ref[...], b_ref[...])


def _outproj(oa, ob, oc, od, w, x, g, b, tm=512):
    M, D = x.shape
    ospec = pl.BlockSpec((tm, GW), lambda i: (i, 0))
    vec = pl.BlockSpec((1, D), lambda i: (0, 0))
    return pl.pallas_call(
        _outproj_kernel,
        grid=(M // tm,),
        in_specs=[ospec, ospec, ospec, ospec, pl.BlockSpec((D, D), lambda i: (0, 0)),
                  pl.BlockSpec((tm, D), lambda i: (i, 0)), vec, vec],
        out_specs=pl.BlockSpec((tm, D), lambda i: (i, 0)),
        out_shape=jax.ShapeDtypeStruct((M, D), f32),
        compiler_params=_cparams("parallel"),
        name="outproj_ln",
    )(oa, ob, oc, od, w, x, g, b)


def _memkv_kernel(mem_ref, wkT_ref, wv_ref, kT_ref, v_ref):
    mb = mem_ref[...].astype(bf16)
    kT_ref[...] = lax.dot_general(wkT_ref[...], mb, (((1,), (1,)), ((), ())),
                                  preferred_element_type=f32).astype(bf16)
    v_ref[...] = jnp.dot(mb, wv_ref[...], preferred_element_type=f32).astype(bf16)


def _memkv(mem, wkT, wv):
    B, N, D = mem.shape
    wspec = pl.BlockSpec((D, D), lambda b: (0, 0))
    return pl.pallas_call(
        _memkv_kernel,
        grid=(B,),
        in_specs=[pl.BlockSpec((None, N, D), lambda b: (b, 0, 0)), wspec, wspec],
        out_specs=[pl.BlockSpec((None, D, N), lambda b: (b, 0, 0)),
                   pl.BlockSpec((None, N, D), lambda b: (b, 0, 0))],
        out_shape=[jax.ShapeDtypeStruct((B, D, N), bf16), jax.ShapeDtypeStruct((B, N, D), bf16)],
        compiler_params=_cparams("parallel"),
        name="cross_kv",
    )(mem, wkT, wv)


def _cross_kernel(x_ref, wq_ref, kT_ref, v_ref, wo_ref, g_ref, b_ref, y_ref):
    x = x_ref[...]
    q = jnp.dot(x.astype(bf16), wq_ref[...], preferred_element_type=f32).astype(bf16)
    outs = []
    for h in range(CROSS_HEADS):
        sl = slice(h * CROSS_DIM, (h + 1) * CROSS_DIM)
        s = jnp.dot(q[:, sl], kT_ref[sl, :], preferred_element_type=f32) * CROSS_DIM ** -0.5
        e = jnp.exp(s - jnp.max(s, axis=1, keepdims=True))
        p = e / jnp.sum(e, axis=1, keepdims=True)
        outs.append(jnp.dot(p.astype(bf16), v_ref[:, sl], preferred_element_type=f32).astype(bf16))
    o = jnp.concatenate(outs, axis=1)
    y = ALPHA * x + jnp.dot(o, wo_ref[...], preferred_element_type=f32)
    y_ref[...] = _layer_norm(y, g_ref[...], b_ref[...])


def _cross(x, wq, kT, v, wo, g, b, tm=512):
    B, L, D = x.shape
    N = v.shape[1]
    wspec = pl.BlockSpec((D, D), lambda bb, i: (0, 0))
    vec = pl.BlockSpec((1, D), lambda bb, i: (0, 0))
    return pl.pallas_call(
        _cross_kernel,
        grid=(B, L // tm),
        in_specs=[pl.BlockSpec((None, tm, D), lambda bb, i: (bb, i, 0)), wspec,
                  pl.BlockSpec((None, D, N), lambda bb, i: (bb, 0, 0)),
                  pl.BlockSpec((None, N, D), lambda bb, i: (bb, 0, 0)), wspec, vec, vec],
        out_specs=pl.BlockSpec((None, tm, D), lambda bb, i: (bb, i, 0)),
        out_shape=jax.ShapeDtypeStruct((B, L, D), f32),
        compiler_params=_cparams("parallel", "parallel"),
        name="cross_ln",
    )(x, wq, kT, v, wo, g, b)


def _mlp_kernel(x_ref, w1_ref, w2_ref, g_ref, b_ref, y_ref, acc_ref):
    j = pl.program_id(1)

    @pl.when(j == 0)
    def _():
        acc_ref[...] = jnp.zeros(acc_ref.shape, f32)

    hdn = jnp.dot(x_ref[...].astype(bf16), w1_ref[...], preferred_element_type=f32)
    hdn = jnp.square(jnp.maximum(hdn, 0.0)).astype(bf16)
    acc_ref[...] += jnp.dot(hdn, w2_ref[...], preferred_element_type=f32)

    @pl.when(j == pl.num_programs(1) - 1)
    def _():
        y_ref[...] = _layer_norm(ALPHA * x_ref[...] + acc_ref[...], g_ref[...], b_ref[...])


def _mlp(x, w1, w2, g, b, tm=1024, tf=1024):
    M, D = x.shape
    F = w1.shape[1]
    vec = pl.BlockSpec((1, D), lambda i, j: (0, 0))
    return pl.pallas_call(
        _mlp_kernel,
        grid=(M // tm, F // tf),
        in_specs=[pl.BlockSpec((tm, D), lambda i, j: (i, 0)),
                  pl.BlockSpec((D, tf), lambda i, j: (0, j)),
                  pl.BlockSpec((tf, D), lambda i, j: (j, 0)), vec, vec],
        out_specs=pl.BlockSpec((tm, D), lambda i, j: (i, 0)),
        out_shape=jax.ShapeDtypeStruct((M, D), f32),
        scratch_shapes=[pltpu.VMEM((tm, D), f32)],
        compiler_params=_cparams("parallel", "arbitrary"),
        name="mlp_ln",
    )(x, w1, w2, g, b)


def _split_w_in(w):
    sizes = (GW, GW, GW, IDX_HEADS * IDX_DIM, IDX_DIM, IDX_HEADS,
             GW, HEAD_DIM, HEAD_DIM, HEAD_DIM, HEAD_DIM, HEAD_DIM, HEAD_DIM, 3 * HG,
             GW, GW, GW, GW, GW, GW)
    offs = [0]
    for s in sizes:
        offs.append(offs[-1] + s)
    return [w[:, offs[n]:offs[n + 1]] for n in range(len(sizes))]


def _layout_w_in(w):
    (a_q, a_k, a_v, a_qi, a_ki, a_w, b_q, b_kc, b_vc, b_ks, b_vs, b_kw, b_vw, b_g,
     c_q, c_k, c_v, d_q, d_k, d_v) = _split_w_in(w)
    twice = lambda t: jnp.concatenate([t, t], axis=1)
    wr = jnp.concatenate([a_k, c_k, d_k, twice(b_ks), twice(b_kw), twice(a_ki), b_kc, b_vc], axis=1)
    wt = jnp.concatenate([a_q, a_v, a_qi, b_q, c_q, c_v, d_q, d_v, b_vs, b_vw], axis=1).T
    ws = jnp.concatenate([a_w, b_g, jnp.zeros((w.shape[0], NS - IDX_HEADS - 3 * HG), w.dtype)], axis=1).T
    assert wr.shape[1] == NR and wt.shape[0] == NT
    return wr.astype(bf16), wt.astype(bf16), ws.astype(bf16)


def _mixers(x, bias, w_in_l, pos_k, pos_v, w1_k, w2_k, w1_v, w2_v, lq1, lk1, lq2, lk2, diff_g, lam_init):
    B, L, _ = x.shape
    wr, wt, ws = _layout_w_in(w_in_l)
    hr, ht, hs = _project(x, wr, wt, ws)
    o_a = _dsa(hr, ht, hs, bias[0:HG])

    n = L // CMP_STRIDE
    half = CMP_STRIDE * HEAD_DIM
    xk = hr[:, :, R_BKVC:R_BKVC + HEAD_DIM].reshape(B, n, half)
    xv = hr[:, :, R_BKVC + HEAD_DIM:R_BKVC + 2 * HEAD_DIM].reshape(B, n, half)
    kc, vcT = _compress(
        xk, xv, pos_k.reshape(2, half), pos_v.reshape(2, half),
        w1_k.reshape(2, half, HEAD_DIM).astype(bf16), w1_v.reshape(2, half, HEAD_DIM).astype(bf16),
        jnp.concatenate([w2_k, w2_k], axis=1).astype(bf16), w2_v.T.astype(bf16))
    ovlT, expand = _nsa_tables(L)
    o_b = _nsa(hr, ht, hs, kc, vcT, ovlT, expand, bias[HG:2 * HG])
    o_c = _moba(hr, ht, bias[2 * HG:3 * HG])
    o_d = _diff(hr, ht, bias[3 * HG:4 * HG], lq1.reshape(1, -1), lk1.reshape(1, -1),
                lq2.reshape(1, -1), lk2.reshape(1, -1), jnp.full((1, 1), lam_init, f32),
                diff_g.reshape(-1, 1))
    return o_a, o_b, o_c, o_d


def kernel(x, mem, rel_bias, w_in, w_out, nsa_pos_k, nsa_pos_v, nsa_w1_k, nsa_w2_k, nsa_w1_v, nsa_w2_v, diff_lq1, diff_lk1, diff_lq2, diff_lk2, diff_g, ln1_g, ln1_b, xq, xk, xv, xo, ln2_g, ln2_b, mlp_w1, mlp_w2, ln3_g, ln3_b):
    B, L, D = x.shape
    bias = _bias_tiles(rel_bias)
    for l in range(DEPTH):
        lam_init = 0.8 - 0.6 * math.exp(-0.3 * l)
        o_a, o_b, o_c, o_d = _mixers(
            x, bias, w_in[l], nsa_pos_k[l], nsa_pos_v[l], nsa_w1_k[l], nsa_w2_k[l],
            nsa_w1_v[l], nsa_w2_v[l], diff_lq1[l], diff_lk1[l], diff_lq2[l], diff_lk2[l],
            diff_g[l], lam_init)
        flat = lambda t: t.reshape(B * L, -1)
        row = lambda t: t.reshape(1, D)
        x2 = _outproj(flat(o_a), flat(o_b), flat(o_c), flat(o_d), w_out[l].astype(bf16),
                      flat(x), row(ln1_g[l]), row(ln1_b[l]))
        kT, v = _memkv(mem, xk[l].T.astype(bf16), xv[l].astype(bf16))
        x3 = _cross(x2.reshape(B, L, D), xq[l].astype(bf16), kT, v, xo[l].astype(bf16),
                    row(ln2_g[l]), row(ln2_b[l]))
        x = _mlp(flat(x3), mlp_w1[l].astype(bf16), mlp_w2[l].astype(bf16),
                 row(ln3_g[l]), row(ln3_b[l])).reshape(B, L, D)
    return x
```

```python
import functools
import math

import jax
import jax.numpy as jnp
from jax import lax
from jax.experimental import pallas as pl
from jax.experimental.pallas import tpu as pltpu

f32 = jnp.float32
bf16 = jnp.bfloat16
i32 = jnp.int32
i16 = jnp.int16

D_MODEL = 1024
DEPTH = 4
HEAD_DIM = 64
HG = 4
GW = HG * HEAD_DIM
REL_BUCKETS = 32
REL_MAX_DIST = 128
IDX_HEADS = 8
IDX_DIM = 64
DSA_TOPK = 256
CMP_LEN = 32
CMP_STRIDE = 16
SLC_LEN = 64
SLC_TOPN = 16
WIN = 512
MOBA_BLOCK = 256
MOBA_TOPK = 3
DIFF_HALF = HEAD_DIM // 2
CROSS_HEADS = 4
CROSS_DIM = D_MODEL // CROSS_HEADS
D_FF = 4 * D_MODEL
ALPHA = (2 * DEPTH) ** 0.25

NEG = -1e30
FLOOR = -1e29
INT_MIN = -(2 ** 31)
HALF = 2 ** 15
LANES = 128
TQ = 256
TK = 256
VMEM_LIMIT = 56 * 1024 * 1024
LOG2E = math.log2(math.e)
FAR_CHAINS = 16
MATMULS_AHEAD = 6

R_AK, R_CK, R_DK, R_BKS, R_BKW, R_AKI, R_BKVC = 0, 256, 512, 768, 896, 1024, 1152
NR = 1280
T_AQ, T_AV, T_AQI, T_BQ, T_CQ, T_CV, T_DQ, T_DV, T_BVS, T_BVW = (
    0, 256, 512, 1024, 1280, 1536, 1792, 2048, 2304, 2368)
NT = 2432
S_AW, S_BG, NS = 0, 8, 32


def _cparams(*sem):
    return pltpu.CompilerParams(dimension_semantics=sem, vmem_limit_bytes=VMEM_LIMIT)


def _rel_bucket(dist):
    n = jnp.maximum(dist, 0)
    max_exact = REL_BUCKETS // 2
    nf = jnp.maximum(n, max_exact).astype(f32)
    large = max_exact + (jnp.log(nf / max_exact) / math.log(REL_MAX_DIST / max_exact)
                         * (REL_BUCKETS - max_exact)).astype(i32)
    large = jnp.minimum(large, REL_BUCKETS - 1)
    return jnp.where(n < max_exact, n, large)


def _bias_tiles(rel_bias):
    assert 2 * TQ - TK + 1 >= REL_MAX_DIST
    d = jnp.arange(TQ, dtype=i32)[None, :] - jnp.arange(TK, dtype=i32)[:, None]
    bucket = jnp.stack([_rel_bucket(d + off) for off in (0, TQ)])
    far = rel_bias[_rel_bucket(jnp.int32(2 * TQ))]
    table = ((rel_bias - far) * LOG2E).astype(f32)
    onehot = (bucket[None] == jnp.arange(REL_BUCKETS, dtype=i32)[:, None, None, None]).astype(f32)
    return jnp.einsum('bntq,bh->hntq', onehot, table, precision=lax.Precision.HIGHEST)


def _row_iota(shape):
    return lax.broadcasted_iota(i32, shape, 0)


def _col_iota(shape):
    return lax.broadcasted_iota(i32, shape, 1)


def _flash_init(m_ref, l_ref, acc_ref):
    m_ref[...] = jnp.full(m_ref.shape, NEG, f32)
    l_ref[...] = jnp.zeros(l_ref.shape, f32)
    acc_ref[...] = jnp.zeros(acc_ref.shape, f32)


def _fold(x, op, rows):
    while x.shape[0] > rows:
        half = x.shape[0] // 2
        x = op(x[:half], x[half:])
    return x


def _flash_step(c, s, vT, m_ref, l_ref, acc_ref, pen=None):
    m_prev = m_ref[c]
    m_chunk = jnp.max(_fold(s, jnp.maximum, 8), axis=0, keepdims=True)
    if pen is not None:
        m_chunk = m_chunk + pen
    m_new = jnp.maximum(m_prev, m_chunk)
    m_use = jnp.maximum(m_new, FLOOR)
    alpha = jnp.exp2(jnp.maximum(m_prev, FLOOR) - m_use)
    p = jnp.exp2(s - (m_use if pen is None else m_use - pen))
    l_ref[c] = alpha * l_ref[c] + jnp.sum(_fold(p, jnp.add, 8), axis=0, keepdims=True)
    acc_ref[c] = alpha * acc_ref[c] + jnp.dot(vT, p.astype(bf16), preferred_element_type=f32)
    m_ref[c] = m_new


def _flash_result(c, l_ref, acc_ref):
    l = l_ref[c]
    return jnp.where(l > 0.0, acc_ref[c] / jnp.where(l > 0.0, l, 1.0), 0.0)


def _attend(qi, n, raw, finish, vT, m_ref, l_ref, acc_ref, first=0, shared=None):
    def run(chunks):
        items = [(kc, kind, c) for kc, kind in chunks for c in range(n)]
        ctx, logits = {}, {}

        def issue(j):
            kc, kind, c = items[j]
            if c == 0 and shared is not None:
                ctx[j // n] = shared(kc, kind)
            logits[j] = raw(kc, c)

        for j in range(min(MATMULS_AHEAD, len(items))):
            issue(j)
        for j, (kc, kind, c) in enumerate(items):
            if j + MATMULS_AHEAD < len(items):
                issue(j + MATMULS_AHEAD)
            s = finish(kind, c, logits.pop(j), ctx.get(j // n))
            s, pen = s if isinstance(s, tuple) else (s, None)
            _flash_step(c, s, vT(kc, c), m_ref, l_ref, acc_ref, pen)

    _flash_init(m_ref, l_ref, acc_ref)
    n_far = jnp.maximum(qi - 1 - first, 0)
    group = max(1, FAR_CHAINS // n)

    def far_group(j, carry):
        run([(first + group * j + t, 2) for t in range(group)])
        return carry

    lax.fori_loop(0, n_far // group, far_group, 0)
    size = group // 2
    while size >= 1:
        done = first + (n_far // (2 * size)) * (2 * size)

        @pl.when((n_far // size) % 2 == 1)
        def _(done=done, size=size):
            run([(done + t, 2) for t in range(size)])

        size //= 2

    @pl.when(qi >= 1)
    def _():
        run([(qi - 1, 1), (qi, 0)])

    @pl.when(qi == 0)
    def _():
        run([(qi, 0)])


def _pair_rows(qT_ref, h, width=HEAD_DIM, offset=0):
    blk = qT_ref[(h // 2) * LANES:(h // 2 + 1) * LANES, :]
    r = _row_iota((LANES, 1))
    lo = (h % 2) * HEAD_DIM + offset
    return jnp.where((r >= lo) & (r < lo + width), blk, jnp.zeros_like(blk))


def _layer_norm(y, g, b):
    mu = jnp.mean(y, axis=-1, keepdims=True)
    yc = y - mu
    var = jnp.mean(yc * yc, axis=-1, keepdims=True)
    return yc * lax.rsqrt(var + 1e-5) * g + b


def _proj_kernel(x_ref, wr_ref, wt_ref, ws_ref, hr_ref, ht_ref, hs_ref):
    xb = x_ref[...].astype(bf16)
    nt = (((1,), (1,)), ((), ()))
    hr_ref[...] = jnp.dot(xb, wr_ref[...], preferred_element_type=f32).astype(bf16)
    ht_ref[...] = lax.dot_general(wt_ref[...], xb, nt, preferred_element_type=f32).astype(bf16)
    hs_ref[...] = lax.dot_general(ws_ref[...], xb, nt, preferred_element_type=f32)


def _project(x, wr, wt, ws):
    B, L, D = x.shape
    nch = L // TK
    return pl.pallas_call(
        _proj_kernel,
        grid=(B, nch),
        in_specs=[
            pl.BlockSpec((None, TK, D), lambda b, i: (b, i, 0)),
            pl.BlockSpec((D, NR), lambda b, i: (0, 0)),
            pl.BlockSpec((NT, D), lambda b, i: (0, 0)),
            pl.BlockSpec((NS, D), lambda b, i: (0, 0)),
        ],
        out_specs=[
            pl.BlockSpec((None, TK, NR), lambda b, i: (b, i, 0)),
            pl.BlockSpec((None, None, NT, TK), lambda b, i: (b, i, 0, 0)),
            pl.BlockSpec((None, None, NS, TK), lambda b, i: (b, i, 0, 0)),
        ],
        out_shape=[
            jax.ShapeDtypeStruct((B, L, NR), bf16),
            jax.ShapeDtypeStruct((B, nch, NT, TK), bf16),
            jax.ShapeDtypeStruct((B, nch, NS, TK), f32),
        ],
        compiler_params=_cparams("parallel", "parallel"),
        name="proj",
    )(x, wr, wt, ws)


def _q_spec(off, rows=GW):
    return pl.BlockSpec((None, None, rows, TQ), lambda b, i: (b, i, off // rows, 0))


def _vT_spec(nch, off, rows=GW):
    return pl.BlockSpec((None, nch, rows, TK), lambda b, i: (b, 0, off // rows, 0))


def _k_spec(L, off, cols=GW):
    return pl.BlockSpec((None, L, cols), lambda b, i: (b, 0, off // cols))


_BIAS_SPEC = pl.BlockSpec((HG, 2, TK, TQ), lambda b, i: (0, 0, 0, 0))
_OUT_SPEC = pl.BlockSpec((None, TQ, GW), lambda b, i: (b, i, 0))


def _flash_scratch(chains, dv=HEAD_DIM):
    return [pltpu.VMEM((chains, 1, TQ), f32), pltpu.VMEM((chains, 1, TQ), f32),
            pltpu.VMEM((chains, dv, TQ), f32)]


def _k_chunk(k_ref, kc, pair=None):
    rows = pl.ds(pl.multiple_of(kc * TK, TK), TK)
    if pair is None:
        return k_ref[rows, :]
    return k_ref[rows, pair * LANES:(pair + 1) * LANES]


def _dsa_kernel(qT_ref, qiT_ref, wT_ref, k_ref, ki_ref, vT_ref, bias_ref, o_ref,
                key_ref, hi_ref, lo_ref, m_ref, l_ref, acc_ref, *, topk):
    qi = pl.program_id(1)
    nch = qi + 1
    qpos = qi * TQ + _col_iota((1, TQ))
    krow = _row_iota((TK, 1))
    wT = wT_ref[...] * (IDX_DIM ** -0.5 * IDX_HEADS ** -0.5)
    qidx = [_pair_rows(qiT_ref, h) for h in range(IDX_HEADS)]

    def score_chunk(kc, carry):
        ki2 = _k_chunk(ki_ref, kc)
        s = jnp.zeros((TK, TQ), f32)
        for h in range(IDX_HEADS):
            r = jnp.dot(ki2, qidx[h], preferred_element_type=f32)
            s = s + wT[S_AW + h:S_AW + h + 1, :] * jnp.maximum(r, 0.0)
        s = jnp.where(s == 0.0, 0.0, s)
        bits = lax.bitcast_convert_type(s, i32)
        key = bits ^ ((bits >> 31) & 0x7FFFFFFF)
        key = jnp.where(kc * TK + krow <= qpos, key, INT_MIN)
        key_ref[kc] = key
        hi_ref[kc] = (key >> 16).astype(i16)
        lo_ref[kc] = ((key & 0xFFFF) - HALF).astype(i16)
        return carry

    lax.fori_loop(0, nch, score_chunk, 0)

    def count(ref, pred):
        one = jnp.ones((), ref.dtype)
        zero = jnp.zeros((), ref.dtype)
        rows = 8 * 4 // ref.dtype.itemsize

        def body(kc, acc):
            hit = jnp.where(pred(ref[kc]), one, zero)
            return acc + _fold(hit, jnp.add, rows).astype(f32)
        acc = lax.fori_loop(0, nch, body, jnp.zeros((rows, TQ), f32))
        return jnp.sum(acc, axis=0, keepdims=True)

    def kth_largest_i16(ref, want):
        def bisect(i, t_u):
            cand_u = t_u | jnp.left_shift(jnp.int32(1), 15 - i)
            cand = (cand_u - HALF).astype(i16)
            return jnp.where(count(ref, lambda k: k >= cand) >= want, cand_u, t_u)
        return lax.fori_loop(0, 16, bisect, jnp.zeros((1, TQ), i32)) - HALF

    hi_t = kth_largest_i16(hi_ref, topk)
    hi_t16 = hi_t.astype(i16)
    above = count(hi_ref, lambda k: k > hi_t16)

    def keep_low(kc, carry):
        lo_ref[kc] = jnp.where(hi_ref[kc] == hi_t16, lo_ref[kc], jnp.full((), -HALF, i16))
        return carry

    lax.fori_loop(0, nch, keep_low, 0)
    lo_t = kth_largest_i16(lo_ref, topk - above)
    thr = jnp.maximum((hi_t << 16) | (lo_t + HALF), INT_MIN + 1)
    cnt_ge = count(key_ref, lambda k: k >= thr)

    @pl.when(jnp.max(cnt_ge) > topk)
    def _():
        need = topk - (cnt_ge - count(key_ref, lambda k: k == thr))
        tri = (_col_iota((TK, TK)) <= _row_iota((TK, TK))).astype(bf16)

        def body(kc, seen):
            k = key_ref[kc]
            tie = k == thr
            tief = jnp.where(tie, 1.0, 0.0)
            pref = jnp.dot(tri, tief.astype(bf16), preferred_element_type=f32) + seen
            key_ref[kc] = jnp.where(tie & (pref > need), INT_MIN, k)
            return seen + jnp.sum(tief, axis=0, keepdims=True)

        lax.fori_loop(0, nch, body, jnp.zeros((1, TQ), f32))

    qs = [_pair_rows(qT_ref, h) for h in range(HG)]

    def finish(kind, h, s, keep):
        s = s * (HEAD_DIM ** -0.5 * LOG2E)
        if kind < 2:
            s = s + bias_ref[h, kind]
        return jnp.where(keep, s, NEG)

    _attend(qi, HG, lambda kc, h: jnp.dot(_k_chunk(k_ref, kc, h // 2), qs[h], preferred_element_type=f32),
            finish, lambda kc, h: vT_ref[kc, h * HEAD_DIM:(h + 1) * HEAD_DIM, :],
            m_ref, l_ref, acc_ref, shared=lambda kc, kind: key_ref[kc] >= thr)
    outT = jnp.concatenate([_flash_result(h, l_ref, acc_ref) for h in range(HG)], axis=0)
    o_ref[...] = outT.T.astype(o_ref.dtype)


def _dsa(hr, ht, hs, bias):
    B, L, _ = hr.shape
    nch = L // TK
    topk = min(DSA_TOPK, L // 4)
    return pl.pallas_call(
        functools.partial(_dsa_kernel, topk=topk),
        grid=(B, L // TQ),
        in_specs=[
            _q_spec(T_AQ), _q_spec(T_AQI, 2 * GW), _q_spec(0, NS),
            _k_spec(L, R_AK), _k_spec(L, R_AKI, LANES), _vT_spec(nch, T_AV), _BIAS_SPEC,
        ],
        out_specs=_OUT_SPEC,
        out_shape=jax.ShapeDtypeStruct((B, L, GW), bf16),
        scratch_shapes=[pltpu.VMEM((nch, TK, TQ), i32), pltpu.VMEM((nch, TK, TQ), i16),
                        pltpu.VMEM((nch, TK, TQ), i16)] + _flash_scratch(HG),
        compiler_params=_cparams("parallel", "arbitrary"),
        name="dsa",
    )(ht, ht, hs, hr, hr, ht, bias)


def _moba_kernel(qT_ref, k_ref, vT_ref, bias_ref, o_ref, km_ref, sel_ref, m_ref, l_ref, acc_ref, *, nch, topk):
    qi = pl.program_id(1)
    nb = km_ref.shape[0]

    @pl.when(qi == 0)
    def _():
        km_ref[...] = jnp.zeros(km_ref.shape, f32)
        for n in range(nch):
            blk = k_ref[n * MOBA_BLOCK:(n + 1) * MOBA_BLOCK, :].astype(f32)
            km_ref[n:n + 1, :] = jnp.sum(blk, axis=0, keepdims=True) * (1.0 / MOBA_BLOCK)

    km = km_ref[...]
    km_hi = km.astype(bf16)
    km_lo = (km - km_hi.astype(f32)).astype(bf16)
    blk_id = _row_iota((nb, 1))
    blk_f = blk_id.astype(f32)
    for h in range(HG):
        qh = qT_ref[...]
        r = _row_iota((GW, 1))
        qh = jnp.where((r >= h * HEAD_DIM) & (r < (h + 1) * HEAD_DIM), qh, jnp.zeros_like(qh))
        gate = (jnp.dot(km_hi, qh, preferred_element_type=f32)
                + jnp.dot(km_lo, qh, preferred_element_type=f32))
        gate = jnp.where(blk_id < qi, gate, -jnp.inf)
        sel = jnp.zeros((nb, TQ), f32)
        for _ in range(topk):
            best = jnp.max(gate, axis=0, keepdims=True)
            first = jnp.min(jnp.where(gate == best, blk_f, float(nb)), axis=0, keepdims=True)
            hit = blk_f == first
            sel = jnp.where(hit & (best > -jnp.inf), 1.0, sel)
            gate = jnp.where(hit, -jnp.inf, gate)
        sel_ref[h] = sel

    qs = [_pair_rows(qT_ref, h) for h in range(HG)]
    causal = _row_iota((TK, 1)) <= _col_iota((1, TQ))

    def finish(kind, h, s, kc):
        s = s * (HEAD_DIM ** -0.5 * LOG2E)
        if kind < 2:
            s = s + bias_ref[h, kind]
        if kind == 0:
            return jnp.where(causal, s, NEG)
        return s, (1.0 - sel_ref[h, pl.ds(kc, 1), :]) * NEG

    _attend(qi, HG, lambda kc, h: jnp.dot(_k_chunk(k_ref, kc, h // 2), qs[h], preferred_element_type=f32),
            finish, lambda kc, h: vT_ref[kc, h * HEAD_DIM:(h + 1) * HEAD_DIM, :],
            m_ref, l_ref, acc_ref, shared=lambda kc, kind: kc)
    outT = jnp.concatenate([_flash_result(h, l_ref, acc_ref) for h in range(HG)], axis=0)
    o_ref[...] = outT.T.astype(o_ref.dtype)


def _moba(hr, ht, bias):
    B, L, _ = hr.shape
    nch = L // TK
    nb = -(-nch // 8) * 8
    topk = min(MOBA_TOPK, nch - 1)
    return pl.pallas_call(
        functools.partial(_moba_kernel, nch=nch, topk=topk),
        grid=(B, L // TQ),
        in_specs=[_q_spec(T_CQ), _k_spec(L, R_CK), _vT_spec(nch, T_CV), _BIAS_SPEC],
        out_specs=_OUT_SPEC,
        out_shape=jax.ShapeDtypeStruct((B, L, GW), bf16),
        scratch_shapes=[pltpu.VMEM((nb, GW), f32), pltpu.VMEM((HG, nb, TQ), f32)] + _flash_scratch(HG),
        compiler_params=_cparams("parallel", "arbitrary"),
        name="moba",
    )(ht, hr, ht, bias)


def _diff_kernel(qT_ref, k_ref, vT_ref, bias_ref, lq1_ref, lk1_ref, lq2_ref, lk2_ref, li_ref, g_ref,
                 o_ref, m_ref, l_ref, acc_ref):
    qi = pl.program_id(1)
    lam_init = li_ref[...]
    lam = (jnp.exp(jnp.sum(lq1_ref[...] * lk1_ref[...], axis=1, keepdims=True))
           - jnp.exp(jnp.sum(lq2_ref[...] * lk2_ref[...], axis=1, keepdims=True)) + lam_init)
    qs = [_pair_rows(qT_ref, h, DIFF_HALF, c * DIFF_HALF) for h in range(HG) for c in range(2)]
    causal = _row_iota((TK, 1)) <= _col_iota((1, TQ))

    def finish(kind, c, s, _):
        s = s * (DIFF_HALF ** -0.5 * LOG2E)
        if kind < 2:
            s = s + bias_ref[c // 2, kind]
        return jnp.where(causal, s, NEG) if kind == 0 else s

    _attend(qi, 2 * HG, lambda kc, c: jnp.dot(_k_chunk(k_ref, kc, c // 4), qs[c], preferred_element_type=f32),
            finish, lambda kc, c: vT_ref[kc, (c // 2) * HEAD_DIM:(c // 2 + 1) * HEAD_DIM, :],
            m_ref, l_ref, acc_ref)
    heads = []
    for h in range(HG):
        o = _flash_result(2 * h, l_ref, acc_ref) - lam * _flash_result(2 * h + 1, l_ref, acc_ref)
        ms = jnp.mean(o * o, axis=0, keepdims=True)
        heads.append(o * lax.rsqrt(ms + 1e-6) * g_ref[...] * (1.0 - lam_init))
    o_ref[...] = jnp.concatenate(heads, axis=0).T.astype(o_ref.dtype)


def _diff(hr, ht, bias, lq1, lk1, lq2, lk2, lam_init, g_col):
    B, L, _ = hr.shape
    nch = L // TK
    vec = pl.BlockSpec((1, DIFF_HALF), lambda b, i: (0, 0))
    return pl.pallas_call(
        _diff_kernel,
        grid=(B, L // TQ),
        in_specs=[
            _q_spec(T_DQ), _k_spec(L, R_DK), _vT_spec(nch, T_DV), _BIAS_SPEC,
            vec, vec, vec, vec,
            pl.BlockSpec((1, 1), lambda b, i: (0, 0)),
            pl.BlockSpec((HEAD_DIM, 1), lambda b, i: (0, 0)),
        ],
        out_specs=_OUT_SPEC,
        out_shape=jax.ShapeDtypeStruct((B, L, GW), bf16),
        scratch_shapes=_flash_scratch(2 * HG),
        compiler_params=_cparams("parallel", "arbitrary"),
        name="diff",
    )(ht, hr, ht, bias, lq1, lk1, lq2, lk2, lam_init, g_col)


def _gelu_tanh(x):
    return 0.5 * x * (1.0 + jnp.tanh(math.sqrt(2.0 / math.pi) * (x + 0.044715 * (x * x * x))))


def _compress_kernel(xk_ref, xv_ref, pk_ref, pv_ref, w1k_ref, w1v_ref, w2k_ref, w2vT_ref, kc_ref, vcT_ref):
    def pre_act(x_ref, p_ref, w1_ref):
        x = x_ref[...].astype(f32)
        first = jnp.dot((x + p_ref[0:1, :]).astype(bf16), w1_ref[0], preferred_element_type=f32)
        second = jnp.dot((x + p_ref[1:2, :]).astype(bf16), w1_ref[1], preferred_element_type=f32)
        n = first.shape[0]
        return _gelu_tanh(first + pltpu.roll(second, n - 1, 0)).astype(bf16)

    gk = pre_act(xk_ref, pk_ref, w1k_ref)
    kc_ref[...] = jnp.dot(gk, w2k_ref[...], preferred_element_type=f32).astype(bf16)
    gv = pre_act(xv_ref, pv_ref, w1v_ref)
    vcT_ref[...] = lax.dot_general(w2vT_ref[...], gv, (((1,), (1,)), ((), ())),
                                   preferred_element_type=f32).astype(bf16)


def _compress(xk, xv, pk, pv, w1k, w1v, w2k, w2vT):
    B, n, W = xk.shape
    xspec = pl.BlockSpec((None, n, W), lambda b: (b, 0, 0))
    full = lambda a: pl.BlockSpec(a.shape, lambda b: (0,) * a.ndim)
    return pl.pallas_call(
        _compress_kernel,
        grid=(B,),
        in_specs=[xspec, xspec, full(pk), full(pv), full(w1k), full(w1v), full(w2k), full(w2vT)],
        out_specs=[
            pl.BlockSpec((None, n, LANES), lambda b: (b, 0, 0)),
            pl.BlockSpec((None, HEAD_DIM, n), lambda b: (b, 0, 0)),
        ],
        out_shape=[
            jax.ShapeDtypeStruct((B, n, LANES), bf16),
            jax.ShapeDtypeStruct((B, HEAD_DIM, n), bf16),
        ],
        compiler_params=_cparams("parallel"),
        name="nsa_compress",
    )(xk, xv, pk, pv, w1k, w1v, w2k, w2vT)


def _split3(x):
    hi = x.astype(bf16)
    r = x - hi.astype(f32)
    mid = r.astype(bf16)
    lo = (r - mid.astype(f32)).astype(bf16)
    return hi, mid, lo


def _nsa_kernel(qT_ref, gT_ref, kc_ref, vcT_ref, ks_ref, vsT_ref, kw_ref, vwT_ref, ovlT_ref, exp_ref,
                bias_ref, o_ref, m_ref, l_ref, acc_ref, *, n_slc, topn):
    qi = pl.program_id(1)
    ncmp = kc_ref.shape[0]
    qpos = qi * TQ + _col_iota((1, TQ))
    pad = jnp.zeros((LANES - HEAD_DIM, TQ), bf16)
    qs = [jnp.concatenate([qT_ref[h * HEAD_DIM:(h + 1) * HEAD_DIM, :], pad], axis=0)
          for h in range(HG)]

    cmp_ok = _row_iota((ncmp, 1)) * CMP_STRIDE + (CMP_LEN - 1) <= qpos
    o_cmp = []
    pc_sum = jnp.zeros((ncmp, TQ), f32)
    for h in range(HG):
        s = jnp.dot(kc_ref[...], qs[h], preferred_element_type=f32) * (HEAD_DIM ** -0.5 * LOG2E)
        s = jnp.where(cmp_ok, s, NEG)
        e = jnp.exp2(s - jnp.maximum(jnp.max(s, axis=0, keepdims=True), FLOOR))
        den = jnp.sum(e, axis=0, keepdims=True)
        pc = e / jnp.where(den > 0.0, den, 1.0)
        pc_sum = pc_sum + pc
        o_cmp.append(jnp.dot(vcT_ref[...], pc.astype(bf16), preferred_element_type=f32))
    nb = ovlT_ref.shape[0]
    imp = jnp.zeros((nb, TQ), f32)
    for part in _split3(pc_sum):
        imp = imp + jnp.dot(ovlT_ref[...], part, preferred_element_type=f32)
    blk = _row_iota((nb, 1))
    blk_f = blk.astype(f32)
    cur = qpos // SLC_LEN
    forced = (blk == 0) | (blk == cur) | (blk == cur - 1)
    imp = jnp.where(forced, jnp.inf, imp)
    imp = jnp.where((blk * SLC_LEN <= qpos) & (blk < n_slc), imp, -jnp.inf)

    def pick(_, st):
        imp, sel = st
        best = jnp.max(imp, axis=0, keepdims=True)
        first = jnp.min(jnp.where(imp == best, blk_f, float(nb)), axis=0, keepdims=True)
        hit = blk_f == first
        return jnp.where(hit, -jnp.inf, imp), jnp.where(hit & (best > -jnp.inf), 1.0, sel)

    _, sel = lax.fori_loop(0, topn, pick, (imp, jnp.zeros((nb, TQ), f32)))

    krow = _row_iota((TK, 1))
    qcol = _col_iota((1, TQ))
    causal = krow <= qcol

    selb = sel.astype(bf16)

    def slc_keep(kc, kind):
        keep = jnp.dot(exp_ref[kc], selb, preferred_element_type=f32) > 0.5
        return keep & causal if kind == 0 else keep

    def slc_finish(kind, h, s, keep):
        s = s * (HEAD_DIM ** -0.5 * LOG2E)
        if kind < 2:
            s = s + bias_ref[h, kind]
        return jnp.where(keep, s, NEG)

    _attend(qi, HG, lambda kc, h: jnp.dot(_k_chunk(ks_ref, kc), qs[h], preferred_element_type=f32),
            slc_finish, lambda kc, h: vsT_ref[kc], m_ref, l_ref, acc_ref, shared=slc_keep)
    o_slc = [_flash_result(h, l_ref, acc_ref) for h in range(HG)]

    def win_finish(kind, h, s, _):
        s = s * (HEAD_DIM ** -0.5 * LOG2E)
        if kind < 2:
            s = s + bias_ref[h, kind]
        if kind == 0:
            return jnp.where(causal, s, NEG)
        return jnp.where(krow > qcol, s, NEG) if kind == 2 else s

    _attend(qi, HG, lambda kc, h: jnp.dot(_k_chunk(kw_ref, kc), qs[h], preferred_element_type=f32),
            win_finish, lambda kc, h: vwT_ref[kc], m_ref, l_ref, acc_ref,
            first=jnp.maximum(qi - WIN // TK, 0))
    o_win = [_flash_result(h, l_ref, acc_ref) for h in range(HG)]

    gates = jax.nn.sigmoid(gT_ref[...])
    heads = []
    for h in range(HG):
        c = S_BG + 3 * h
        heads.append(gates[c:c + 1, :] * o_cmp[h] + gates[c + 1:c + 2, :] * o_slc[h]
                     + gates[c + 2:c + 3, :] * o_win[h])
    o_ref[...] = jnp.concatenate(heads, axis=0).T.astype(o_ref.dtype)


def _nsa(hr, ht, hs, kc, vcT, ovlT, expand, bias):
    B, L, _ = hr.shape
    nch = L // TK
    n = kc.shape[1]
    n_slc = L // SLC_LEN
    topn = min(SLC_TOPN, n_slc)
    return pl.pallas_call(
        functools.partial(_nsa_kernel, n_slc=n_slc, topn=topn),
        grid=(B, L // TQ),
        in_specs=[
            _q_spec(T_BQ), _q_spec(0, NS),
            pl.BlockSpec((None, n, LANES), lambda b, i: (b, 0, 0)),
            pl.BlockSpec((None, HEAD_DIM, n), lambda b, i: (b, 0, 0)),
            _k_spec(L, R_BKS, LANES), _vT_spec(nch, T_BVS, HEAD_DIM),
            _k_spec(L, R_BKW, LANES), _vT_spec(nch, T_BVW, HEAD_DIM),
            pl.BlockSpec(ovlT.shape, lambda b, i: (0, 0)),
            pl.BlockSpec(expand.shape, lambda b, i: (0, 0, 0)),
            _BIAS_SPEC,
        ],
        out_specs=_OUT_SPEC,
        out_shape=jax.ShapeDtypeStruct((B, L, GW), bf16),
        scratch_shapes=_flash_scratch(HG),
        compiler_params=_cparams("parallel", "arbitrary"),
        name="nsa",
    )(ht, hs, kc, vcT, hr, ht, hr, ht, ovlT, expand, bias)


def _nsa_tables(L):
    n = L // CMP_STRIDE
    n_slc = L // SLC_LEN
    nb = -(-n_slc // 16) * 16
    c0 = jnp.arange(n, dtype=i32)[None, :] * CMP_STRIDE
    s0 = jnp.arange(nb, dtype=i32)[:, None] * SLC_LEN
    ovlT = (c0 <= s0 + SLC_LEN - 1) & (c0 + CMP_LEN - 1 >= s0)
    ovlT = ovlT & (jnp.arange(n)[None, :] < n - 1) & (jnp.arange(nb)[:, None] < n_slc)
    tokblk = jnp.arange(L, dtype=i32) // SLC_LEN
    expand = tokblk[:, None] == jnp.arange(nb, dtype=i32)[None, :]
    return ovlT.astype(bf16), expand.reshape(L // TK, TK, nb).astype(bf16)


def _memkv_kernel(mem_ref, wkT_ref, wv_ref, kT_ref, v_ref):
    mb = mem_ref[...].astype(bf16)
    kT_ref[...] = lax.dot_general(wkT_ref[...], mb, (((1,), (1,)), ((), ())),
                                  preferred_element_type=f32).astype(bf16)
    v_ref[...] = jnp.dot(mb, wv_ref[...], preferred_element_type=f32).astype(bf16)


def _memkv(mem, wkT, wv):
    B, N, D = mem.shape
    wspec = pl.BlockSpec((D, D), lambda b: (0, 0))
    return pl.pallas_call(
        _memkv_kernel,
        grid=(B,),
        in_specs=[pl.BlockSpec((None, N, D), lambda b: (b, 0, 0)), wspec, wspec],
        out_specs=[pl.BlockSpec((None, D, N), lambda b: (b, 0, 0)),
                   pl.BlockSpec((None, N, D), lambda b: (b, 0, 0))],
        out_shape=[jax.ShapeDtypeStruct((B, D, N), bf16), jax.ShapeDtypeStruct((B, N, D), bf16)],
        compiler_params=_cparams("parallel"),
        name="cross_kv",
    )(mem, wkT, wv)


def _mix_cross_kernel(oa_ref, ob_ref, oc_ref, od_ref, w_ref, x_ref, g1_ref, b1_ref,
                      wq_ref, kT_ref, v_ref, wo_ref, g_ref, b_ref, y_ref):
    mix = jnp.zeros(x_ref.shape, f32)
    for n, o_ref in enumerate((oa_ref, ob_ref, oc_ref, od_ref)):
        mix = mix + jnp.dot(o_ref[...], w_ref[n * GW:(n + 1) * GW, :], preferred_element_type=f32)
    x = _layer_norm(ALPHA * x_ref[...] + mix, g1_ref[...], b1_ref[...])
    q = jnp.dot(x.astype(bf16), wq_ref[...], preferred_element_type=f32).astype(bf16)
    outs = []
    for h in range(CROSS_HEADS):
        sl = slice(h * CROSS_DIM, (h + 1) * CROSS_DIM)
        s = jnp.dot(q[:, sl], kT_ref[sl, :], preferred_element_type=f32) * CROSS_DIM ** -0.5
        e = jnp.exp(s - jnp.max(s, axis=1, keepdims=True))
        p = e / jnp.sum(e, axis=1, keepdims=True)
        outs.append(jnp.dot(p.astype(bf16), v_ref[:, sl], preferred_element_type=f32).astype(bf16))
    o = jnp.concatenate(outs, axis=1)
    y = ALPHA * x + jnp.dot(o, wo_ref[...], preferred_element_type=f32)
    y_ref[...] = _layer_norm(y, g_ref[...], b_ref[...])


def _mix_cross(oa, ob, oc, od, w, x, g1, b1, wq, kT, v, wo, g, b, tm=512):
    B, L, D = x.shape
    N = v.shape[1]
    ospec = pl.BlockSpec((None, tm, GW), lambda bb, i: (bb, i, 0))
    xspec = pl.BlockSpec((None, tm, D), lambda bb, i: (bb, i, 0))
    wspec = pl.BlockSpec((D, D), lambda bb, i: (0, 0))
    vec = pl.BlockSpec((1, D), lambda bb, i: (0, 0))
    return pl.pallas_call(
        _mix_cross_kernel,
        grid=(B, L // tm),
        in_specs=[ospec, ospec, ospec, ospec, wspec, xspec, vec, vec, wspec,
                  pl.BlockSpec((None, D, N), lambda bb, i: (bb, 0, 0)),
                  pl.BlockSpec((None, N, D), lambda bb, i: (bb, 0, 0)), wspec, vec, vec],
        out_specs=xspec,
        out_shape=jax.ShapeDtypeStruct((B, L, D), f32),
        compiler_params=_cparams("parallel", "parallel"),
        name="mix_cross_ln",
    )(oa, ob, oc, od, w, x, g1, b1, wq, kT, v, wo, g, b)


def _mlp_kernel(x_ref, w1_ref, w2_ref, g_ref, b_ref, y_ref, acc_ref):
    j = pl.program_id(1)

    @pl.when(j == 0)
    def _():
        acc_ref[...] = jnp.zeros(acc_ref.shape, f32)

    hdn = jnp.dot(x_ref[...].astype(bf16), w1_ref[...], preferred_element_type=f32)
    hdn = jnp.square(jnp.maximum(hdn, 0.0)).astype(bf16)
    acc_ref[...] += jnp.dot(hdn, w2_ref[...], preferred_element_type=f32)

    @pl.when(j == pl.num_programs(1) - 1)
    def _():
        y_ref[...] = _layer_norm(ALPHA * x_ref[...] + acc_ref[...], g_ref[...], b_ref[...])


def _mlp(x, w1, w2, g, b, tm=1024, tf=1024):
    M, D = x.shape
    F = w1.shape[1]
    vec = pl.BlockSpec((1, D), lambda i, j: (0, 0))
    return pl.pallas_call(
        _mlp_kernel,
        grid=(M // tm, F // tf),
        in_specs=[pl.BlockSpec((tm, D), lambda i, j: (i, 0)),
                  pl.BlockSpec((D, tf), lambda i, j: (0, j)),
                  pl.BlockSpec((tf, D), lambda i, j: (j, 0)), vec, vec],
        out_specs=pl.BlockSpec((tm, D), lambda i, j: (i, 0)),
        out_shape=jax.ShapeDtypeStruct((M, D), f32),
        scratch_shapes=[pltpu.VMEM((tm, D), f32)],
        compiler_params=_cparams("parallel", "arbitrary"),
        name="mlp_ln",
    )(x, w1, w2, g, b)


def _split_w_in(w):
    sizes = (GW, GW, GW, IDX_HEADS * IDX_DIM, IDX_DIM, IDX_HEADS,
             GW, HEAD_DIM, HEAD_DIM, HEAD_DIM, HEAD_DIM, HEAD_DIM, HEAD_DIM, 3 * HG,
             GW, GW, GW, GW, GW, GW)
    offs = [0]
    for s in sizes:
        offs.append(offs[-1] + s)
    return [w[:, offs[n]:offs[n + 1]] for n in range(len(sizes))]


def _layout_w_in(w):
    (a_q, a_k, a_v, a_qi, a_ki, a_w, b_q, b_kc, b_vc, b_ks, b_vs, b_kw, b_vw, b_g,
     c_q, c_k, c_v, d_q, d_k, d_v) = _split_w_in(w)
    twice = lambda t: jnp.concatenate([t, t], axis=1)
    wr = jnp.concatenate([a_k, c_k, d_k, twice(b_ks), twice(b_kw), twice(a_ki), b_kc, b_vc], axis=1)
    wt = jnp.concatenate([a_q, a_v, a_qi, b_q, c_q, c_v, d_q, d_v, b_vs, b_vw], axis=1).T
    ws = jnp.concatenate([a_w, b_g, jnp.zeros((w.shape[0], NS - IDX_HEADS - 3 * HG), w.dtype)], axis=1).T
    assert wr.shape[1] == NR and wt.shape[0] == NT
    return wr.astype(bf16), wt.astype(bf16), ws.astype(bf16)


def _mixers(x, bias, w_in_l, pos_k, pos_v, w1_k, w2_k, w1_v, w2_v, lq1, lk1, lq2, lk2, diff_g, lam_init):
    B, L, _ = x.shape
    wr, wt, ws = _layout_w_in(w_in_l)
    hr, ht, hs = _project(x, wr, wt, ws)
    o_a = _dsa(hr, ht, hs, bias[0:HG])

    n = L // CMP_STRIDE
    half = CMP_STRIDE * HEAD_DIM
    xk = hr[:, :, R_BKVC:R_BKVC + HEAD_DIM].reshape(B, n, half)
    xv = hr[:, :, R_BKVC + HEAD_DIM:R_BKVC + 2 * HEAD_DIM].reshape(B, n, half)
    kc, vcT = _compress(
        xk, xv, pos_k.reshape(2, half), pos_v.reshape(2, half),
        w1_k.reshape(2, half, HEAD_DIM).astype(bf16), w1_v.reshape(2, half, HEAD_DIM).astype(bf16),
        jnp.concatenate([w2_k, w2_k], axis=1).astype(bf16), w2_v.T.astype(bf16))
    ovlT, expand = _nsa_tables(L)
    o_b = _nsa(hr, ht, hs, kc, vcT, ovlT, expand, bias[HG:2 * HG])
    o_c = _moba(hr, ht, bias[2 * HG:3 * HG])
    o_d = _diff(hr, ht, bias[3 * HG:4 * HG], lq1.reshape(1, -1), lk1.reshape(1, -1),
                lq2.reshape(1, -1), lk2.reshape(1, -1), jnp.full((1, 1), lam_init, f32),
                diff_g.reshape(-1, 1))
    return o_a, o_b, o_c, o_d


def kernel(x, mem, rel_bias, w_in, w_out, nsa_pos_k, nsa_pos_v, nsa_w1_k, nsa_w2_k, nsa_w1_v, nsa_w2_v, diff_lq1, diff_lk1, diff_lq2, diff_lk2, diff_g, ln1_g, ln1_b, xq, xk, xv, xo, ln2_g, ln2_b, mlp_w1, mlp_w2, ln3_g, ln3_b):
    B, L, D = x.shape
    bias = _bias_tiles(rel_bias)
    for l in range(DEPTH):
        lam_init = 0.8 - 0.6 * math.exp(-0.3 * l)
        o_a, o_b, o_c, o_d = _mixers(
            x, bias, w_in[l], nsa_pos_k[l], nsa_pos_v[l], nsa_w1_k[l], nsa_w2_k[l],
            nsa_w1_v[l], nsa_w2_v[l], diff_lq1[l], diff_lk1[l], diff_lq2[l], diff_lk2[l],
            diff_g[l], lam_init)
        flat = lambda t: t.reshape(B * L, -1)
        row = lambda t: t.reshape(1, D)
        kT, v = _memkv(mem, xk[l].T.astype(bf16), xv[l].astype(bf16))
        x3 = _mix_cross(o_a, o_b, o_c, o_d, w_out[l].astype(bf16), x, row(ln1_g[l]), row(ln1_b[l]),
                        xq[l].astype(bf16), kT, v, xo[l].astype(bf16), row(ln2_g[l]), row(ln2_b[l]))
        x = _mlp(flat(x3), mlp_w1[l].astype(bf16), mlp_w2[l].astype(bf16),
                 row(ln3_g[l]), row(ln3_b[l])).reshape(B, L, D)
    return x
```

```python
import functools
import math

import jax
import jax.numpy as jnp
from jax import lax
from jax.experimental import pallas as pl
from jax.experimental.pallas import tpu as pltpu

f32 = jnp.float32
bf16 = jnp.bfloat16
i32 = jnp.int32
i16 = jnp.int16

D_MODEL = 1024
DEPTH = 4
HEAD_DIM = 64
HG = 4
GW = HG * HEAD_DIM
REL_BUCKETS = 32
REL_MAX_DIST = 128
IDX_HEADS = 8
IDX_DIM = 64
DSA_TOPK = 256
CMP_LEN = 32
CMP_STRIDE = 16
SLC_LEN = 64
SLC_TOPN = 16
WIN = 512
MOBA_BLOCK = 256
MOBA_TOPK = 3
DIFF_HALF = HEAD_DIM // 2
CROSS_HEADS = 4
CROSS_DIM = D_MODEL // CROSS_HEADS
D_FF = 4 * D_MODEL
ALPHA = (2 * DEPTH) ** 0.25

NEG = -1e30
FLOOR = -1e29
INT_MIN = -(2 ** 31)
HALF = 2 ** 15
LANES = 128
TQ = 256
TK = 256
VMEM_LIMIT = 56 * 1024 * 1024
LOG2E = math.log2(math.e)
FAR_CHAINS = 16
MATMULS_AHEAD = 6

R_AK, R_CK, R_DK, R_BKS, R_BKW, R_AKI, R_BKVC = 0, 256, 512, 768, 896, 1024, 1152
NR = 1280
T_AQ, T_AV, T_AQI, T_BQ, T_CQ, T_CV, T_DQ, T_DV, T_BVS, T_BVW = (
    0, 256, 512, 1024, 1280, 1536, 1792, 2048, 2304, 2368)
NT = 2432
S_AW, S_BG, NS = 0, 8, 32


def _cparams(*sem):
    return pltpu.CompilerParams(dimension_semantics=sem, vmem_limit_bytes=VMEM_LIMIT)


def _rel_bucket(dist):
    n = jnp.maximum(dist, 0)
    max_exact = REL_BUCKETS // 2
    nf = jnp.maximum(n, max_exact).astype(f32)
    large = max_exact + (jnp.log(nf / max_exact) / math.log(REL_MAX_DIST / max_exact)
                         * (REL_BUCKETS - max_exact)).astype(i32)
    large = jnp.minimum(large, REL_BUCKETS - 1)
    return jnp.where(n < max_exact, n, large)


def _bias_tiles(rel_bias):
    assert 2 * TQ - TK + 1 >= REL_MAX_DIST
    d = jnp.arange(TQ, dtype=i32)[None, :] - jnp.arange(TK, dtype=i32)[:, None]
    bucket = jnp.stack([_rel_bucket(d + off) for off in (0, TQ)])
    far = rel_bias[_rel_bucket(jnp.int32(2 * TQ))]
    table = ((rel_bias - far) * LOG2E).astype(f32)
    onehot = (bucket[None] == jnp.arange(REL_BUCKETS, dtype=i32)[:, None, None, None]).astype(f32)
    return jnp.einsum('bntq,bh->hntq', onehot, table, precision=lax.Precision.HIGHEST)


def _row_iota(shape):
    return lax.broadcasted_iota(i32, shape, 0)


def _col_iota(shape):
    return lax.broadcasted_iota(i32, shape, 1)


def _flash_init(m_ref, l_ref, acc_ref):
    m_ref[...] = jnp.full(m_ref.shape, NEG, f32)
    l_ref[...] = jnp.zeros(l_ref.shape, f32)
    acc_ref[...] = jnp.zeros(acc_ref.shape, f32)


def _fold(x, op, rows):
    while x.shape[0] > rows:
        half = x.shape[0] // 2
        x = op(x[:half], x[half:])
    return x


def _flash_step(c, s, vT, m_ref, l_ref, acc_ref, pen=None):
    m_prev = m_ref[c]
    m_chunk = jnp.max(_fold(s, jnp.maximum, 8), axis=0, keepdims=True)
    if pen is not None:
        m_chunk = m_chunk + pen
    m_new = jnp.maximum(m_prev, m_chunk)
    m_use = jnp.maximum(m_new, FLOOR)
    alpha = jnp.exp2(jnp.maximum(m_prev, FLOOR) - m_use)
    p = jnp.exp2(s - (m_use if pen is None else m_use - pen))
    l_ref[c] = alpha * l_ref[c] + jnp.sum(_fold(p, jnp.add, 8), axis=0, keepdims=True)
    acc_ref[c] = alpha * acc_ref[c] + jnp.dot(vT, p.astype(bf16), preferred_element_type=f32)
    m_ref[c] = m_new


def _flash_result(c, l_ref, acc_ref):
    l = l_ref[c]
    return jnp.where(l > 0.0, acc_ref[c] / jnp.where(l > 0.0, l, 1.0), 0.0)


def _attend(qi, n, raw, finish, vT, m_ref, l_ref, acc_ref, first=0, shared=None, max_far=None):
    def run(chunks):
        items = [(kc, kind, c) for kc, kind in chunks for c in range(n)]
        ctx, logits = {}, {}

        def issue(j):
            kc, kind, c = items[j]
            if c == 0 and shared is not None:
                ctx[j // n] = shared(kc, kind)
            logits[j] = raw(kc, c)

        for j in range(min(MATMULS_AHEAD, len(items))):
            issue(j)
        for j, (kc, kind, c) in enumerate(items):
            if j + MATMULS_AHEAD < len(items):
                issue(j + MATMULS_AHEAD)
            s = finish(kind, c, logits.pop(j), ctx.get(j // n))
            s, pen = s if isinstance(s, tuple) else (s, None)
            _flash_step(c, s, vT(kc, c), m_ref, l_ref, acc_ref, pen)

    _flash_init(m_ref, l_ref, acc_ref)
    n_far = jnp.maximum(qi - 1 - first, 0)
    group = max(1, FAR_CHAINS // n)
    if max_far is not None and max_far < group:
        group = max_far + 1

    def far_group(j, carry):
        run([(first + group * j + t, 2) for t in range(group)])
        return carry

    lax.fori_loop(0, n_far // group, far_group, 0)
    done = first + (n_far // group) * group
    for rem in range(group):
        @pl.when((qi >= 1) & (n_far % group == rem))
        def _(rem=rem):
            run([(done + t, 2) for t in range(rem)] + [(qi - 1, 1), (qi, 0)])

    @pl.when(qi == 0)
    def _():
        run([(qi, 0)])


def _pair_rows(qT_ref, h, width=HEAD_DIM, offset=0):
    blk = qT_ref[(h // 2) * LANES:(h // 2 + 1) * LANES, :]
    r = _row_iota((LANES, 1))
    lo = (h % 2) * HEAD_DIM + offset
    return jnp.where((r >= lo) & (r < lo + width), blk, jnp.zeros_like(blk))


def _layer_norm(y, g, b):
    mu = jnp.mean(y, axis=-1, keepdims=True)
    yc = y - mu
    var = jnp.mean(yc * yc, axis=-1, keepdims=True)
    return yc * lax.rsqrt(var + 1e-5) * g + b


def _proj_kernel(x_ref, wr_ref, wt_ref, ws_ref, hr_ref, ht_ref, hs_ref):
    xb = x_ref[...].astype(bf16)
    nt = (((1,), (1,)), ((), ()))
    hr_ref[...] = jnp.dot(xb, wr_ref[...], preferred_element_type=f32).astype(bf16)
    ht_ref[...] = lax.dot_general(wt_ref[...], xb, nt, preferred_element_type=f32).astype(bf16)
    hs_ref[...] = lax.dot_general(ws_ref[...], xb, nt, preferred_element_type=f32)


def _project(x, wr, wt, ws):
    B, L, D = x.shape
    nch = L // TK
    return pl.pallas_call(
        _proj_kernel,
        grid=(B, nch),
        in_specs=[
            pl.BlockSpec((None, TK, D), lambda b, i: (b, i, 0)),
            pl.BlockSpec((D, NR), lambda b, i: (0, 0)),
            pl.BlockSpec((NT, D), lambda b, i: (0, 0)),
            pl.BlockSpec((NS, D), lambda b, i: (0, 0)),
        ],
        out_specs=[
            pl.BlockSpec((None, TK, NR), lambda b, i: (b, i, 0)),
            pl.BlockSpec((None, None, NT, TK), lambda b, i: (b, i, 0, 0)),
            pl.BlockSpec((None, None, NS, TK), lambda b, i: (b, i, 0, 0)),
        ],
        out_shape=[
            jax.ShapeDtypeStruct((B, L, NR), bf16),
            jax.ShapeDtypeStruct((B, nch, NT, TK), bf16),
            jax.ShapeDtypeStruct((B, nch, NS, TK), f32),
        ],
        compiler_params=_cparams("parallel", "parallel"),
        name="proj",
    )(x, wr, wt, ws)


def _q_spec(off, rows=GW):
    return pl.BlockSpec((None, None, rows, TQ), lambda b, i: (b, i, off // rows, 0))


def _vT_spec(nch, off, rows=GW):
    return pl.BlockSpec((None, nch, rows, TK), lambda b, i: (b, 0, off // rows, 0))


def _k_spec(L, off, cols=GW):
    return pl.BlockSpec((None, L, cols), lambda b, i: (b, 0, off // cols))


_BIAS_SPEC = pl.BlockSpec((HG, 2, TK, TQ), lambda b, i: (0, 0, 0, 0))
_OUT_SPEC = pl.BlockSpec((None, TQ, GW), lambda b, i: (b, i, 0))


def _flash_scratch(chains, dv=HEAD_DIM):
    return [pltpu.VMEM((chains, 1, TQ), f32), pltpu.VMEM((chains, 1, TQ), f32),
            pltpu.VMEM((chains, dv, TQ), f32)]


def _k_chunk(k_ref, kc, pair=None):
    rows = pl.ds(pl.multiple_of(kc * TK, TK), TK)
    if pair is None:
        return k_ref[rows, :]
    return k_ref[rows, pair * LANES:(pair + 1) * LANES]


def _dsa_kernel(qT_ref, qiT_ref, wT_ref, k_ref, ki_ref, vT_ref, bias_ref, o_ref,
                key_ref, hi_ref, lo_ref, m_ref, l_ref, acc_ref, *, topk):
    qi = pl.program_id(1)
    nch = qi + 1
    qpos = qi * TQ + _col_iota((1, TQ))
    krow = _row_iota((TK, 1))
    wT = wT_ref[...] * (IDX_DIM ** -0.5 * IDX_HEADS ** -0.5)
    qidx = [_pair_rows(qiT_ref, h) for h in range(IDX_HEADS)]

    def score_chunk(kc, carry):
        ki2 = _k_chunk(ki_ref, kc)
        s = jnp.zeros((TK, TQ), f32)
        for h in range(IDX_HEADS):
            r = jnp.dot(ki2, qidx[h], preferred_element_type=f32)
            s = s + wT[S_AW + h:S_AW + h + 1, :] * jnp.maximum(r, 0.0)
        s = jnp.where(s == 0.0, 0.0, s)
        bits = lax.bitcast_convert_type(s, i32)
        key = bits ^ ((bits >> 31) & 0x7FFFFFFF)
        key = jnp.where(kc * TK + krow <= qpos, key, INT_MIN)
        key_ref[kc] = key
        hi_ref[kc] = (key >> 16).astype(i16)
        lo_ref[kc] = ((key & 0xFFFF) - HALF).astype(i16)
        return carry

    lax.fori_loop(0, nch, score_chunk, 0)

    def count(ref, pred):
        one = jnp.ones((), ref.dtype)
        zero = jnp.zeros((), ref.dtype)
        rows = 8 * 4 // ref.dtype.itemsize

        def body(kc, acc):
            hit = jnp.where(pred(ref[kc]), one, zero)
            return acc + _fold(hit, jnp.add, rows).astype(f32)
        acc = lax.fori_loop(0, nch, body, jnp.zeros((rows, TQ), f32))
        return jnp.sum(acc, axis=0, keepdims=True)

    def kth_largest_i16(ref, want):
        def bisect(i, t_u):
            cand_u = t_u | jnp.left_shift(jnp.int32(1), 15 - i)
            cand = (cand_u - HALF).astype(i16)
            return jnp.where(count(ref, lambda k: k >= cand) >= want, cand_u, t_u)
        return lax.fori_loop(0, 16, bisect, jnp.zeros((1, TQ), i32)) - HALF

    hi_t = kth_largest_i16(hi_ref, topk)
    hi_t16 = hi_t.astype(i16)
    above = count(hi_ref, lambda k: k > hi_t16)

    def keep_low(kc, carry):
        lo_ref[kc] = jnp.where(hi_ref[kc] == hi_t16, lo_ref[kc], jnp.full((), -HALF, i16))
        return carry

    lax.fori_loop(0, nch, keep_low, 0)
    lo_t = kth_largest_i16(lo_ref, topk - above)
    thr = jnp.maximum((hi_t << 16) | (lo_t + HALF), INT_MIN + 1)
    cnt_ge = count(key_ref, lambda k: k >= thr)

    @pl.when(jnp.max(cnt_ge) > topk)
    def _():
        need = topk - (cnt_ge - count(key_ref, lambda k: k == thr))
        tri = (_col_iota((TK, TK)) <= _row_iota((TK, TK))).astype(bf16)

        def body(kc, seen):
            k = key_ref[kc]
            tie = k == thr
            tief = jnp.where(tie, 1.0, 0.0)
            pref = jnp.dot(tri, tief.astype(bf16), preferred_element_type=f32) + seen
            key_ref[kc] = jnp.where(tie & (pref > need), INT_MIN, k)
            return seen + jnp.sum(tief, axis=0, keepdims=True)

        lax.fori_loop(0, nch, body, jnp.zeros((1, TQ), f32))

    qs = [_pair_rows(qT_ref, h) for h in range(HG)]

    def finish(kind, h, s, keep):
        s = s * (HEAD_DIM ** -0.5 * LOG2E)
        if kind < 2:
            s = s + bias_ref[h, kind]
        return jnp.where(keep, s, NEG)

    _attend(qi, HG, lambda kc, h: jnp.dot(_k_chunk(k_ref, kc, h // 2), qs[h], preferred_element_type=f32),
            finish, lambda kc, h: vT_ref[kc, h * HEAD_DIM:(h + 1) * HEAD_DIM, :],
            m_ref, l_ref, acc_ref, shared=lambda kc, kind: key_ref[kc] >= thr)
    outT = jnp.concatenate([_flash_result(h, l_ref, acc_ref) for h in range(HG)], axis=0)
    o_ref[...] = outT.T.astype(o_ref.dtype)


def _dsa(hr, ht, hs, bias):
    B, L, _ = hr.shape
    nch = L // TK
    topk = min(DSA_TOPK, L // 4)
    return pl.pallas_call(
        functools.partial(_dsa_kernel, topk=topk),
        grid=(B, L // TQ),
        in_specs=[
            _q_spec(T_AQ), _q_spec(T_AQI, 2 * GW), _q_spec(0, NS),
            _k_spec(L, R_AK), _k_spec(L, R_AKI, LANES), _vT_spec(nch, T_AV), _BIAS_SPEC,
        ],
        out_specs=_OUT_SPEC,
        out_shape=jax.ShapeDtypeStruct((B, L, GW), bf16),
        scratch_shapes=[pltpu.VMEM((nch, TK, TQ), i32), pltpu.VMEM((nch, TK, TQ), i16),
                        pltpu.VMEM((nch, TK, TQ), i16)] + _flash_scratch(HG),
        compiler_params=_cparams("parallel", "arbitrary"),
        name="dsa",
    )(ht, ht, hs, hr, hr, ht, bias)


def _moba_kernel(qT_ref, k_ref, vT_ref, bias_ref, o_ref, km_ref, sel_ref, m_ref, l_ref, acc_ref, *, nch, topk):
    qi = pl.program_id(1)
    nb = km_ref.shape[0]

    @pl.when(qi == 0)
    def _():
        km_ref[...] = jnp.zeros(km_ref.shape, f32)
        for n in range(nch):
            blk = k_ref[n * MOBA_BLOCK:(n + 1) * MOBA_BLOCK, :].astype(f32)
            km_ref[n:n + 1, :] = jnp.sum(blk, axis=0, keepdims=True) * (1.0 / MOBA_BLOCK)

    km = km_ref[...]
    km_hi = km.astype(bf16)
    km_lo = (km - km_hi.astype(f32)).astype(bf16)
    blk_id = _row_iota((nb, 1))
    blk_f = blk_id.astype(f32)
    for h in range(HG):
        qh = qT_ref[...]
        r = _row_iota((GW, 1))
        qh = jnp.where((r >= h * HEAD_DIM) & (r < (h + 1) * HEAD_DIM), qh, jnp.zeros_like(qh))
        gate = (jnp.dot(km_hi, qh, preferred_element_type=f32)
                + jnp.dot(km_lo, qh, preferred_element_type=f32))
        gate = jnp.where(blk_id < qi, gate, -jnp.inf)
        sel = jnp.zeros((nb, TQ), f32)
        for _ in range(topk):
            best = jnp.max(gate, axis=0, keepdims=True)
            first = jnp.min(jnp.where(gate == best, blk_f, float(nb)), axis=0, keepdims=True)
            hit = blk_f == first
            sel = jnp.where(hit & (best > -jnp.inf), 1.0, sel)
            gate = jnp.where(hit, -jnp.inf, gate)
        sel_ref[h] = sel

    qs = [_pair_rows(qT_ref, h) for h in range(HG)]
    causal = _row_iota((TK, 1)) <= _col_iota((1, TQ))

    def finish(kind, h, s, kc):
        s = s * (HEAD_DIM ** -0.5 * LOG2E)
        if kind < 2:
            s = s + bias_ref[h, kind]
        if kind == 0:
            return jnp.where(causal, s, NEG)
        return s, (1.0 - sel_ref[h, pl.ds(kc, 1), :]) * NEG

    _attend(qi, HG, lambda kc, h: jnp.dot(_k_chunk(k_ref, kc, h // 2), qs[h], preferred_element_type=f32),
            finish, lambda kc, h: vT_ref[kc, h * HEAD_DIM:(h + 1) * HEAD_DIM, :],
            m_ref, l_ref, acc_ref, shared=lambda kc, kind: kc)
    outT = jnp.concatenate([_flash_result(h, l_ref, acc_ref) for h in range(HG)], axis=0)
    o_ref[...] = outT.T.astype(o_ref.dtype)


def _moba(hr, ht, bias):
    B, L, _ = hr.shape
    nch = L // TK
    nb = -(-nch // 8) * 8
    topk = min(MOBA_TOPK, nch - 1)
    return pl.pallas_call(
        functools.partial(_moba_kernel, nch=nch, topk=topk),
        grid=(B, L // TQ),
        in_specs=[_q_spec(T_CQ), _k_spec(L, R_CK), _vT_spec(nch, T_CV), _BIAS_SPEC],
        out_specs=_OUT_SPEC,
        out_shape=jax.ShapeDtypeStruct((B, L, GW), bf16),
        scratch_shapes=[pltpu.VMEM((nb, GW), f32), pltpu.VMEM((HG, nb, TQ), f32)] + _flash_scratch(HG),
        compiler_params=_cparams("parallel", "arbitrary"),
        name="moba",
    )(ht, hr, ht, bias)


def _diff_kernel(qT_ref, k_ref, vT_ref, bias_ref, lq1_ref, lk1_ref, lq2_ref, lk2_ref, li_ref, g_ref,
                 o_ref, m_ref, l_ref, acc_ref):
    qi = pl.program_id(1)
    lam_init = li_ref[...]
    lam = (jnp.exp(jnp.sum(lq1_ref[...] * lk1_ref[...], axis=1, keepdims=True))
           - jnp.exp(jnp.sum(lq2_ref[...] * lk2_ref[...], axis=1, keepdims=True)) + lam_init)
    qs = [_pair_rows(qT_ref, h, DIFF_HALF, c * DIFF_HALF) for h in range(HG) for c in range(2)]
    causal = _row_iota((TK, 1)) <= _col_iota((1, TQ))

    def finish(kind, c, s, _):
        s = s * (DIFF_HALF ** -0.5 * LOG2E)
        if kind < 2:
            s = s + bias_ref[c // 2, kind]
        return jnp.where(causal, s, NEG) if kind == 0 else s

    _attend(qi, 2 * HG, lambda kc, c: jnp.dot(_k_chunk(k_ref, kc, c // 4), qs[c], preferred_element_type=f32),
            finish, lambda kc, c: vT_ref[kc, (c // 2) * HEAD_DIM:(c // 2 + 1) * HEAD_DIM, :],
            m_ref, l_ref, acc_ref)
    heads = []
    for h in range(HG):
        o = _flash_result(2 * h, l_ref, acc_ref) - lam * _flash_result(2 * h + 1, l_ref, acc_ref)
        ms = jnp.mean(o * o, axis=0, keepdims=True)
        heads.append(o * lax.rsqrt(ms + 1e-6) * g_ref[...] * (1.0 - lam_init))
    o_ref[...] = jnp.concatenate(heads, axis=0).T.astype(o_ref.dtype)


def _diff(hr, ht, bias, lq1, lk1, lq2, lk2, lam_init, g_col):
    B, L, _ = hr.shape
    nch = L // TK
    vec = pl.BlockSpec((1, DIFF_HALF), lambda b, i: (0, 0))
    return pl.pallas_call(
        _diff_kernel,
        grid=(B, L // TQ),
        in_specs=[
            _q_spec(T_DQ), _k_spec(L, R_DK), _vT_spec(nch, T_DV), _BIAS_SPEC,
            vec, vec, vec, vec,
            pl.BlockSpec((1, 1), lambda b, i: (0, 0)),
            pl.BlockSpec((HEAD_DIM, 1), lambda b, i: (0, 0)),
        ],
        out_specs=_OUT_SPEC,
        out_shape=jax.ShapeDtypeStruct((B, L, GW), bf16),
        scratch_shapes=_flash_scratch(2 * HG),
        compiler_params=_cparams("parallel", "arbitrary"),
        name="diff",
    )(ht, hr, ht, bias, lq1, lk1, lq2, lk2, lam_init, g_col)


def _gelu_tanh(x):
    return 0.5 * x * (1.0 + jnp.tanh(math.sqrt(2.0 / math.pi) * (x + 0.044715 * (x * x * x))))


def _compress_kernel(xk_ref, xv_ref, pk_ref, pv_ref, w1k_ref, w1v_ref, w2k_ref, w2vT_ref, kc_ref, vcT_ref):
    def pre_act(x_ref, p_ref, w1_ref):
        x = x_ref[...].astype(f32)
        first = jnp.dot((x + p_ref[0:1, :]).astype(bf16), w1_ref[0], preferred_element_type=f32)
        second = jnp.dot((x + p_ref[1:2, :]).astype(bf16), w1_ref[1], preferred_element_type=f32)
        n = first.shape[0]
        return _gelu_tanh(first + pltpu.roll(second, n - 1, 0)).astype(bf16)

    gk = pre_act(xk_ref, pk_ref, w1k_ref)
    kc_ref[...] = jnp.dot(gk, w2k_ref[...], preferred_element_type=f32).astype(bf16)
    gv = pre_act(xv_ref, pv_ref, w1v_ref)
    vcT_ref[...] = lax.dot_general(w2vT_ref[...], gv, (((1,), (1,)), ((), ())),
                                   preferred_element_type=f32).astype(bf16)


def _compress(xk, xv, pk, pv, w1k, w1v, w2k, w2vT):
    B, n, W = xk.shape
    xspec = pl.BlockSpec((None, n, W), lambda b: (b, 0, 0))
    full = lambda a: pl.BlockSpec(a.shape, lambda b: (0,) * a.ndim)
    return pl.pallas_call(
        _compress_kernel,
        grid=(B,),
        in_specs=[xspec, xspec, full(pk), full(pv), full(w1k), full(w1v), full(w2k), full(w2vT)],
        out_specs=[
            pl.BlockSpec((None, n, LANES), lambda b: (b, 0, 0)),
            pl.BlockSpec((None, HEAD_DIM, n), lambda b: (b, 0, 0)),
        ],
        out_shape=[
            jax.ShapeDtypeStruct((B, n, LANES), bf16),
            jax.ShapeDtypeStruct((B, HEAD_DIM, n), bf16),
        ],
        compiler_params=_cparams("parallel"),
        name="nsa_compress",
    )(xk, xv, pk, pv, w1k, w1v, w2k, w2vT)


def _split3(x):
    hi = x.astype(bf16)
    r = x - hi.astype(f32)
    mid = r.astype(bf16)
    lo = (r - mid.astype(f32)).astype(bf16)
    return hi, mid, lo


def _nsa_kernel(qT_ref, gT_ref, kc_ref, vcT_ref, ks_ref, vsT_ref, kw_ref, vwT_ref, ovlT_ref, exp_ref,
                bias_ref, o_ref, m_ref, l_ref, acc_ref, *, n_slc, topn):
    qi = pl.program_id(1)
    ncmp = kc_ref.shape[0]
    qpos = qi * TQ + _col_iota((1, TQ))
    pad = jnp.zeros((LANES - HEAD_DIM, TQ), bf16)
    qs = [jnp.concatenate([qT_ref[h * HEAD_DIM:(h + 1) * HEAD_DIM, :], pad], axis=0)
          for h in range(HG)]

    cmp_ok = _row_iota((ncmp, 1)) * CMP_STRIDE + (CMP_LEN - 1) <= qpos
    o_cmp = []
    pc_sum = jnp.zeros((ncmp, TQ), f32)
    for h in range(HG):
        s = jnp.dot(kc_ref[...], qs[h], preferred_element_type=f32) * (HEAD_DIM ** -0.5 * LOG2E)
        s = jnp.where(cmp_ok, s, NEG)
        e = jnp.exp2(s - jnp.maximum(jnp.max(s, axis=0, keepdims=True), FLOOR))
        den = jnp.sum(e, axis=0, keepdims=True)
        pc = e / jnp.where(den > 0.0, den, 1.0)
        pc_sum = pc_sum + pc
        o_cmp.append(jnp.dot(vcT_ref[...], pc.astype(bf16), preferred_element_type=f32))
    nb = ovlT_ref.shape[0]
    imp = jnp.zeros((nb, TQ), f32)
    for part in _split3(pc_sum):
        imp = imp + jnp.dot(ovlT_ref[...], part, preferred_element_type=f32)
    blk = _row_iota((nb, 1))
    blk_f = blk.astype(f32)
    cur = qpos // SLC_LEN
    forced = (blk == 0) | (blk == cur) | (blk == cur - 1)
    imp = jnp.where(forced, jnp.inf, imp)
    imp = jnp.where((blk * SLC_LEN <= qpos) & (blk < n_slc), imp, -jnp.inf)

    def pick(_, st):
        imp, sel = st
        best = jnp.max(imp, axis=0, keepdims=True)
        first = jnp.min(jnp.where(imp == best, blk_f, float(nb)), axis=0, keepdims=True)
        hit = blk_f == first
        return jnp.where(hit, -jnp.inf, imp), jnp.where(hit & (best > -jnp.inf), 1.0, sel)

    _, sel = lax.fori_loop(0, topn, pick, (imp, jnp.zeros((nb, TQ), f32)))

    krow = _row_iota((TK, 1))
    qcol = _col_iota((1, TQ))
    causal = krow <= qcol

    selb = sel.astype(bf16)

    def slc_keep(kc, kind):
        keep = jnp.dot(exp_ref[kc], selb, preferred_element_type=f32) > 0.5
        return keep & causal if kind == 0 else keep

    def slc_finish(kind, h, s, keep):
        s = s * (HEAD_DIM ** -0.5 * LOG2E)
        if kind < 2:
            s = s + bias_ref[h, kind]
        return jnp.where(keep, s, NEG)

    _attend(qi, HG, lambda kc, h: jnp.dot(_k_chunk(ks_ref, kc), qs[h], preferred_element_type=f32),
            slc_finish, lambda kc, h: vsT_ref[kc], m_ref, l_ref, acc_ref, shared=slc_keep)
    o_slc = [_flash_result(h, l_ref, acc_ref) for h in range(HG)]

    def win_finish(kind, h, s, _):
        s = s * (HEAD_DIM ** -0.5 * LOG2E)
        if kind < 2:
            s = s + bias_ref[h, kind]
        if kind == 0:
            return jnp.where(causal, s, NEG)
        return jnp.where(krow > qcol, s, NEG) if kind == 2 else s

    _attend(qi, HG, lambda kc, h: jnp.dot(_k_chunk(kw_ref, kc), qs[h], preferred_element_type=f32),
            win_finish, lambda kc, h: vwT_ref[kc], m_ref, l_ref, acc_ref,
            first=jnp.maximum(qi - WIN // TK, 0), max_far=WIN // TK - 1)
    o_win = [_flash_result(h, l_ref, acc_ref) for h in range(HG)]

    gates = jax.nn.sigmoid(gT_ref[...])
    heads = []
    for h in range(HG):
        c = S_BG + 3 * h
        heads.append(gates[c:c + 1, :] * o_cmp[h] + gates[c + 1:c + 2, :] * o_slc[h]
                     + gates[c + 2:c + 3, :] * o_win[h])
    o_ref[...] = jnp.concatenate(heads, axis=0).T.astype(o_ref.dtype)


def _nsa(hr, ht, hs, kc, vcT, ovlT, expand, bias):
    B, L, _ = hr.shape
    nch = L // TK
    n = kc.shape[1]
    n_slc = L // SLC_LEN
    topn = min(SLC_TOPN, n_slc)
    return pl.pallas_call(
        functools.partial(_nsa_kernel, n_slc=n_slc, topn=topn),
        grid=(B, L // TQ),
        in_specs=[
            _q_spec(T_BQ), _q_spec(0, NS),
            pl.BlockSpec((None, n, LANES), lambda b, i: (b, 0, 0)),
            pl.BlockSpec((None, HEAD_DIM, n), lambda b, i: (b, 0, 0)),
            _k_spec(L, R_BKS, LANES), _vT_spec(nch, T_BVS, HEAD_DIM),
            _k_spec(L, R_BKW, LANES), _vT_spec(nch, T_BVW, HEAD_DIM),
            pl.BlockSpec(ovlT.shape, lambda b, i: (0, 0)),
            pl.BlockSpec(expand.shape, lambda b, i: (0, 0, 0)),
            _BIAS_SPEC,
        ],
        out_specs=_OUT_SPEC,
        out_shape=jax.ShapeDtypeStruct((B, L, GW), bf16),
        scratch_shapes=_flash_scratch(HG),
        compiler_params=_cparams("parallel", "arbitrary"),
        name="nsa",
    )(ht, hs, kc, vcT, hr, ht, hr, ht, ovlT, expand, bias)


def _nsa_tables(L):
    n = L // CMP_STRIDE
    n_slc = L // SLC_LEN
    nb = -(-n_slc // 16) * 16
    c0 = jnp.arange(n, dtype=i32)[None, :] * CMP_STRIDE
    s0 = jnp.arange(nb, dtype=i32)[:, None] * SLC_LEN
    ovlT = (c0 <= s0 + SLC_LEN - 1) & (c0 + CMP_LEN - 1 >= s0)
    ovlT = ovlT & (jnp.arange(n)[None, :] < n - 1) & (jnp.arange(nb)[:, None] < n_slc)
    tokblk = jnp.arange(L, dtype=i32) // SLC_LEN
    expand = tokblk[:, None] == jnp.arange(nb, dtype=i32)[None, :]
    return ovlT.astype(bf16), expand.reshape(L // TK, TK, nb).astype(bf16)


def _memkv_kernel(mem_ref, wkT_ref, wv_ref, kT_ref, v_ref):
    mb = mem_ref[...].astype(bf16)
    kT_ref[...] = lax.dot_general(wkT_ref[...], mb, (((1,), (1,)), ((), ())),
                                  preferred_element_type=f32).astype(bf16)
    v_ref[...] = jnp.dot(mb, wv_ref[...], preferred_element_type=f32).astype(bf16)


def _memkv(mem, wkT, wv):
    B, N, D = mem.shape
    wspec = pl.BlockSpec((D, D), lambda b: (0, 0))
    return pl.pallas_call(
        _memkv_kernel,
        grid=(B,),
        in_specs=[pl.BlockSpec((None, N, D), lambda b: (b, 0, 0)), wspec, wspec],
        out_specs=[pl.BlockSpec((None, D, N), lambda b: (b, 0, 0)),
                   pl.BlockSpec((None, N, D), lambda b: (b, 0, 0))],
        out_shape=[jax.ShapeDtypeStruct((B, D, N), bf16), jax.ShapeDtypeStruct((B, N, D), bf16)],
        compiler_params=_cparams("parallel"),
        name="cross_kv",
    )(mem, wkT, wv)


def _mix_cross_kernel(oa_ref, ob_ref, oc_ref, od_ref, w_ref, x_ref, g1_ref, b1_ref,
                      wq_ref, kT_ref, v_ref, wo_ref, g_ref, b_ref, y_ref):
    mix = jnp.zeros(x_ref.shape, f32)
    for n, o_ref in enumerate((oa_ref, ob_ref, oc_ref, od_ref)):
        mix = mix + jnp.dot(o_ref[...], w_ref[n * GW:(n + 1) * GW, :], preferred_element_type=f32)
    x = _layer_norm(ALPHA * x_ref[...] + mix, g1_ref[...], b1_ref[...])
    q = jnp.dot(x.astype(bf16), wq_ref[...], preferred_element_type=f32).astype(bf16)
    outs = []
    for h in range(CROSS_HEADS):
        sl = slice(h * CROSS_DIM, (h + 1) * CROSS_DIM)
        s = jnp.dot(q[:, sl], kT_ref[sl, :], preferred_element_type=f32) * CROSS_DIM ** -0.5
        e = jnp.exp(s - jnp.max(s, axis=1, keepdims=True))
        p = e / jnp.sum(e, axis=1, keepdims=True)
        outs.append(jnp.dot(p.astype(bf16), v_ref[:, sl], preferred_element_type=f32).astype(bf16))
    o = jnp.concatenate(outs, axis=1)
    y = ALPHA * x + jnp.dot(o, wo_ref[...], preferred_element_type=f32)
    y_ref[...] = _layer_norm(y, g_ref[...], b_ref[...])


def _mix_cross(oa, ob, oc, od, w, x, g1, b1, wq, kT, v, wo, g, b, tm=512):
    B, L, D = x.shape
    N = v.shape[1]
    ospec = pl.BlockSpec((None, tm, GW), lambda bb, i: (bb, i, 0))
    xspec = pl.BlockSpec((None, tm, D), lambda bb, i: (bb, i, 0))
    wspec = pl.BlockSpec((D, D), lambda bb, i: (0, 0))
    vec = pl.BlockSpec((1, D), lambda bb, i: (0, 0))
    return pl.pallas_call(
        _mix_cross_kernel,
        grid=(B, L // tm),
        in_specs=[ospec, ospec, ospec, ospec, wspec, xspec, vec, vec, wspec,
                  pl.BlockSpec((None, D, N), lambda bb, i: (bb, 0, 0)),
                  pl.BlockSpec((None, N, D), lambda bb, i: (bb, 0, 0)), wspec, vec, vec],
        out_specs=xspec,
        out_shape=jax.ShapeDtypeStruct((B, L, D), f32),
        compiler_params=_cparams("parallel", "parallel"),
        name="mix_cross_ln",
    )(oa, ob, oc, od, w, x, g1, b1, wq, kT, v, wo, g, b)


def _mlp_kernel(x_ref, w1_ref, w2_ref, g_ref, b_ref, y_ref, acc_ref):
    j = pl.program_id(1)

    @pl.when(j == 0)
    def _():
        acc_ref[...] = jnp.zeros(acc_ref.shape, f32)

    hdn = jnp.dot(x_ref[...].astype(bf16), w1_ref[...], preferred_element_type=f32)
    hdn = jnp.square(jnp.maximum(hdn, 0.0)).astype(bf16)
    acc_ref[...] += jnp.dot(hdn, w2_ref[...], preferred_element_type=f32)

    @pl.when(j == pl.num_programs(1) - 1)
    def _():
        y_ref[...] = _layer_norm(ALPHA * x_ref[...] + acc_ref[...], g_ref[...], b_ref[...])


def _mlp(x, w1, w2, g, b, tm=1024, tf=1024):
    M, D = x.shape
    F = w1.shape[1]
    vec = pl.BlockSpec((1, D), lambda i, j: (0, 0))
    return pl.pallas_call(
        _mlp_kernel,
        grid=(M // tm, F // tf),
        in_specs=[pl.BlockSpec((tm, D), lambda i, j: (i, 0)),
                  pl.BlockSpec((D, tf), lambda i, j: (0, j)),
                  pl.BlockSpec((tf, D), lambda i, j: (j, 0)), vec, vec],
        out_specs=pl.BlockSpec((tm, D), lambda i, j: (i, 0)),
        out_shape=jax.ShapeDtypeStruct((M, D), f32),
        scratch_shapes=[pltpu.VMEM((tm, D), f32)],
        compiler_params=_cparams("parallel", "arbitrary"),
        name="mlp_ln",
    )(x, w1, w2, g, b)


def _split_w_in(w):
    sizes = (GW, GW, GW, IDX_HEADS * IDX_DIM, IDX_DIM, IDX_HEADS,
             GW, HEAD_DIM, HEAD_DIM, HEAD_DIM, HEAD_DIM, HEAD_DIM, HEAD_DIM, 3 * HG,
             GW, GW, GW, GW, GW, GW)
    offs = [0]
    for s in sizes:
        offs.append(offs[-1] + s)
    return [w[:, offs[n]:offs[n + 1]] for n in range(len(sizes))]


def _layout_w_in(w):
    (a_q, a_k, a_v, a_qi, a_ki, a_w, b_q, b_kc, b_vc, b_ks, b_vs, b_kw, b_vw, b_g,
     c_q, c_k, c_v, d_q, d_k, d_v) = _split_w_in(w)
    twice = lambda t: jnp.concatenate([t, t], axis=1)
    wr = jnp.concatenate([a_k, c_k, d_k, twice(b_ks), twice(b_kw), twice(a_ki), b_kc, b_vc], axis=1)
    wt = jnp.concatenate([a_q, a_v, a_qi, b_q, c_q, c_v, d_q, d_v, b_vs, b_vw], axis=1).T
    ws = jnp.concatenate([a_w, b_g, jnp.zeros((w.shape[0], NS - IDX_HEADS - 3 * HG), w.dtype)], axis=1).T
    assert wr.shape[1] == NR and wt.shape[0] == NT
    return wr.astype(bf16), wt.astype(bf16), ws.astype(bf16)


def _mixers(x, bias, w_in_l, pos_k, pos_v, w1_k, w2_k, w1_v, w2_v, lq1, lk1, lq2, lk2, diff_g, lam_init):
    B, L, _ = x.shape
    wr, wt, ws = _layout_w_in(w_in_l)
    hr, ht, hs = _project(x, wr, wt, ws)
    o_a = _dsa(hr, ht, hs, bias[0:HG])

    n = L // CMP_STRIDE
    half = CMP_STRIDE * HEAD_DIM
    xk = hr[:, :, R_BKVC:R_BKVC + HEAD_DIM].reshape(B, n, half)
    xv = hr[:, :, R_BKVC + HEAD_DIM:R_BKVC + 2 * HEAD_DIM].reshape(B, n, half)
    kc, vcT = _compress(
        xk, xv, pos_k.reshape(2, half), pos_v.reshape(2, half),
        w1_k.reshape(2, half, HEAD_DIM).astype(bf16), w1_v.reshape(2, half, HEAD_DIM).astype(bf16),
        jnp.concatenate([w2_k, w2_k], axis=1).astype(bf16), w2_v.T.astype(bf16))
    ovlT, expand = _nsa_tables(L)
    o_b = _nsa(hr, ht, hs, kc, vcT, ovlT, expand, bias[HG:2 * HG])
    o_c = _moba(hr, ht, bias[2 * HG:3 * HG])
    o_d = _diff(hr, ht, bias[3 * HG:4 * HG], lq1.reshape(1, -1), lk1.reshape(1, -1),
                lq2.reshape(1, -1), lk2.reshape(1, -1), jnp.full((1, 1), lam_init, f32),
                diff_g.reshape(-1, 1))
    return o_a, o_b, o_c, o_d


def kernel(x, mem, rel_bias, w_in, w_out, nsa_pos_k, nsa_pos_v, nsa_w1_k, nsa_w2_k, nsa_w1_v, nsa_w2_v, diff_lq1, diff_lk1, diff_lq2, diff_lk2, diff_g, ln1_g, ln1_b, xq, xk, xv, xo, ln2_g, ln2_b, mlp_w1, mlp_w2, ln3_g, ln3_b):
    B, L, D = x.shape
    bias = _bias_tiles(rel_bias)
    for l in range(DEPTH):
        lam_init = 0.8 - 0.6 * math.exp(-0.3 * l)
        o_a, o_b, o_c, o_d = _mixers(
            x, bias, w_in[l], nsa_pos_k[l], nsa_pos_v[l], nsa_w1_k[l], nsa_w2_k[l],
            nsa_w1_v[l], nsa_w2_v[l], diff_lq1[l], diff_lk1[l], diff_lq2[l], diff_lk2[l],
            diff_g[l], lam_init)
        flat = lambda t: t.reshape(B * L, -1)
        row = lambda t: t.reshape(1, D)
        kT, v = _memkv(mem, xk[l].T.astype(bf16), xv[l].astype(bf16))
        x3 = _mix_cross(o_a, o_b, o_c, o_d, w_out[l].astype(bf16), x, row(ln1_g[l]), row(ln1_b[l]),
                        xq[l].astype(bf16), kT, v, xo[l].astype(bf16), row(ln2_g[l]), row(ln2_b[l]))
        x = _mlp(flat(x3), mlp_w1[l].astype(bf16), mlp_w2[l].astype(bf16),
                 row(ln3_g[l]), row(ln3_b[l])).reshape(B, L, D)
    return x
```

```python
import functools
import math

import jax
import jax.numpy as jnp
from jax import lax
from jax.experimental import pallas as pl
from jax.experimental.pallas import tpu as pltpu

f32 = jnp.float32
bf16 = jnp.bfloat16
i32 = jnp.int32
i16 = jnp.int16

D_MODEL = 1024
DEPTH = 4
HEAD_DIM = 64
HG = 4
GW = HG * HEAD_DIM
REL_BUCKETS = 32
REL_MAX_DIST = 128
IDX_HEADS = 8
IDX_DIM = 64
DSA_TOPK = 256
CMP_LEN = 32
CMP_STRIDE = 16
SLC_LEN = 64
SLC_TOPN = 16
WIN = 512
MOBA_BLOCK = 256
MOBA_TOPK = 3
DIFF_HALF = HEAD_DIM // 2
CROSS_HEADS = 4
CROSS_DIM = D_MODEL // CROSS_HEADS
D_FF = 4 * D_MODEL
ALPHA = (2 * DEPTH) ** 0.25

NEG = -1e30
FLOOR = -1e29
INT_MIN = -(2 ** 31)
HALF = 2 ** 15
LANES = 128
TQ = 256
TK = 256
VMEM_LIMIT = 56 * 1024 * 1024
LOG2E = math.log2(math.e)
FAR_CHAINS = 16
MATMULS_AHEAD = 6

R_AK, R_CK, R_DK, R_BKS, R_BKW, R_AKI, R_BKVC = 0, 256, 512, 768, 896, 1024, 1152
NR = 1280
T_AQ, T_AV, T_AQI, T_BQ, T_CQ, T_CV, T_DQ, T_DV, T_BVS, T_BVW = (
    0, 256, 512, 1024, 1280, 1536, 1792, 2048, 2304, 2368)
NT = 2432
S_AW, S_BG, NS = 0, 8, 32


def _cparams(*sem):
    return pltpu.CompilerParams(dimension_semantics=sem, vmem_limit_bytes=VMEM_LIMIT)


def _rel_bucket(dist):
    n = jnp.maximum(dist, 0)
    max_exact = REL_BUCKETS // 2
    nf = jnp.maximum(n, max_exact).astype(f32)
    large = max_exact + (jnp.log(nf / max_exact) / math.log(REL_MAX_DIST / max_exact)
                         * (REL_BUCKETS - max_exact)).astype(i32)
    large = jnp.minimum(large, REL_BUCKETS - 1)
    return jnp.where(n < max_exact, n, large)


def _bias_tiles(rel_bias):
    assert 2 * TQ - TK + 1 >= REL_MAX_DIST
    d = jnp.arange(TQ, dtype=i32)[None, :] - jnp.arange(TK, dtype=i32)[:, None]
    bucket = jnp.stack([_rel_bucket(d + off) for off in (0, TQ)])
    far = rel_bias[_rel_bucket(jnp.int32(2 * TQ))]
    table = ((rel_bias - far) * LOG2E).astype(f32)
    onehot = (bucket[None] == jnp.arange(REL_BUCKETS, dtype=i32)[:, None, None, None]).astype(f32)
    return jnp.einsum('bntq,bh->hntq', onehot, table, precision=lax.Precision.HIGHEST)


def _row_iota(shape):
    return lax.broadcasted_iota(i32, shape, 0)


def _col_iota(shape):
    return lax.broadcasted_iota(i32, shape, 1)


def _flash_init(m_ref, l_ref, acc_ref):
    m_ref[...] = jnp.full(m_ref.shape, NEG, f32)
    l_ref[...] = jnp.zeros(l_ref.shape, f32)
    acc_ref[...] = jnp.zeros(acc_ref.shape, f32)


def _fold(x, op, rows):
    while x.shape[0] > rows:
        half = x.shape[0] // 2
        x = op(x[:half], x[half:])
    return x


def _flash_step(c, s, vT, m_ref, l_ref, acc_ref, pen=None):
    m_prev = m_ref[c]
    m_chunk = jnp.max(_fold(s, jnp.maximum, 8), axis=0, keepdims=True)
    if pen is not None:
        m_chunk = m_chunk + pen
    m_new = jnp.maximum(m_prev, m_chunk)
    m_use = jnp.maximum(m_new, FLOOR)
    alpha = jnp.exp2(jnp.maximum(m_prev, FLOOR) - m_use)
    p = jnp.exp2(s - (m_use if pen is None else m_use - pen))
    l_ref[c] = alpha * l_ref[c] + jnp.sum(_fold(p, jnp.add, 8), axis=0, keepdims=True)
    acc_ref[c] = alpha * acc_ref[c] + jnp.dot(vT, p.astype(bf16), preferred_element_type=f32)
    m_ref[c] = m_new


def _flash_result(c, l_ref, acc_ref):
    l = l_ref[c]
    return jnp.where(l > 0.0, acc_ref[c] / jnp.where(l > 0.0, l, 1.0), 0.0)


def _attend(qi, n, raw, finish, vT, m_ref, l_ref, acc_ref, first=0, shared=None, max_far=None):
    def run(chunks):
        items = [(kc, kind, c) for kc, kind in chunks for c in range(n)]
        ctx, logits = {}, {}

        def issue(j):
            kc, kind, c = items[j]
            if c == 0 and shared is not None:
                ctx[j // n] = shared(kc, kind)
            logits[j] = raw(kc, c)

        for j in range(min(MATMULS_AHEAD, len(items))):
            issue(j)
        for j, (kc, kind, c) in enumerate(items):
            if j + MATMULS_AHEAD < len(items):
                issue(j + MATMULS_AHEAD)
            s = finish(kind, c, logits.pop(j), ctx.get(j // n))
            s, pen = s if isinstance(s, tuple) else (s, None)
            _flash_step(c, s, vT(kc, c), m_ref, l_ref, acc_ref, pen)

    _flash_init(m_ref, l_ref, acc_ref)
    n_far = jnp.maximum(qi - 1 - first, 0)
    group = max(1, FAR_CHAINS // n)
    if max_far is not None and max_far < group:
        group = max_far + 1

    def far_group(j, carry):
        run([(first + group * j + t, 2) for t in range(group)])
        return carry

    lax.fori_loop(0, n_far // group, far_group, 0)
    done = first + (n_far // group) * group
    for rem in range(group):
        @pl.when((qi >= 1) & (n_far % group == rem))
        def _(rem=rem):
            run([(done + t, 2) for t in range(rem)] + [(qi - 1, 1), (qi, 0)])

    @pl.when(qi == 0)
    def _():
        run([(qi, 0)])


def _pair_rows(qT_ref, h, width=HEAD_DIM, offset=0):
    blk = qT_ref[(h // 2) * LANES:(h // 2 + 1) * LANES, :]
    r = _row_iota((LANES, 1))
    lo = (h % 2) * HEAD_DIM + offset
    return jnp.where((r >= lo) & (r < lo + width), blk, jnp.zeros_like(blk))


def _layer_norm(y, g, b):
    mu = jnp.mean(y, axis=-1, keepdims=True)
    yc = y - mu
    var = jnp.mean(yc * yc, axis=-1, keepdims=True)
    return yc * lax.rsqrt(var + 1e-5) * g + b


def _proj_kernel(x_ref, wr_ref, wt_ref, ws_ref, hr_ref, ht_ref, hs_ref):
    xb = x_ref[...].astype(bf16)
    nt = (((1,), (1,)), ((), ()))
    hr_ref[...] = jnp.dot(xb, wr_ref[...], preferred_element_type=f32).astype(bf16)
    ht_ref[...] = lax.dot_general(wt_ref[...], xb, nt, preferred_element_type=f32).astype(bf16)
    hs_ref[...] = lax.dot_general(ws_ref[...], xb, nt, preferred_element_type=f32)


def _project(x, wr, wt, ws):
    B, L, D = x.shape
    nch = L // TK
    return pl.pallas_call(
        _proj_kernel,
        grid=(B, nch),
        in_specs=[
            pl.BlockSpec((None, TK, D), lambda b, i: (b, i, 0)),
            pl.BlockSpec((D, NR), lambda b, i: (0, 0)),
            pl.BlockSpec((NT, D), lambda b, i: (0, 0)),
            pl.BlockSpec((NS, D), lambda b, i: (0, 0)),
        ],
        out_specs=[
            pl.BlockSpec((None, TK, NR), lambda b, i: (b, i, 0)),
            pl.BlockSpec((None, None, NT, TK), lambda b, i: (b, i, 0, 0)),
            pl.BlockSpec((None, None, NS, TK), lambda b, i: (b, i, 0, 0)),
        ],
        out_shape=[
            jax.ShapeDtypeStruct((B, L, NR), bf16),
            jax.ShapeDtypeStruct((B, nch, NT, TK), bf16),
            jax.ShapeDtypeStruct((B, nch, NS, TK), f32),
        ],
        compiler_params=_cparams("parallel", "parallel"),
        name="proj",
    )(x, wr, wt, ws)


def _q_spec(off, rows=GW):
    return pl.BlockSpec((None, None, rows, TQ), lambda b, i: (b, i, off // rows, 0))


def _vT_spec(nch, off, rows=GW):
    return pl.BlockSpec((None, nch, rows, TK), lambda b, i: (b, 0, off // rows, 0))


def _k_spec(L, off, cols=GW):
    return pl.BlockSpec((None, L, cols), lambda b, i: (b, 0, off // cols))


_BIAS_SPEC = pl.BlockSpec((HG, 2, TK, TQ), lambda b, i: (0, 0, 0, 0))
_OUT_SPEC = pl.BlockSpec((None, TQ, GW), lambda b, i: (b, i, 0))


def _flash_scratch(chains, dv=HEAD_DIM):
    return [pltpu.VMEM((chains, 1, TQ), f32), pltpu.VMEM((chains, 1, TQ), f32),
            pltpu.VMEM((chains, dv, TQ), f32)]


def _k_chunk(k_ref, kc, pair=None):
    rows = pl.ds(pl.multiple_of(kc * TK, TK), TK)
    if pair is None:
        return k_ref[rows, :]
    return k_ref[rows, pair * LANES:(pair + 1) * LANES]


def _dsa_kernel(qT_ref, qiT_ref, wT_ref, k_ref, ki_ref, vT_ref, bias_ref, o_ref,
                key_ref, hi_ref, lo_ref, m_ref, l_ref, acc_ref, *, topk):
    qi = pl.program_id(1)
    nch = qi + 1
    qpos = qi * TQ + _col_iota((1, TQ))
    krow = _row_iota((TK, 1))
    wT = wT_ref[...] * (IDX_DIM ** -0.5 * IDX_HEADS ** -0.5)
    qidx = [_pair_rows(qiT_ref, h) for h in range(IDX_HEADS)]

    def score_chunk(kc):
        ki2 = _k_chunk(ki_ref, kc)
        s = jnp.zeros((TK, TQ), f32)
        for h in range(IDX_HEADS):
            r = jnp.dot(ki2, qidx[h], preferred_element_type=f32)
            s = s + wT[S_AW + h:S_AW + h + 1, :] * jnp.maximum(r, 0.0)
        s = jnp.where(s == 0.0, 0.0, s)
        bits = lax.bitcast_convert_type(s, i32)
        key = bits ^ ((bits >> 31) & 0x7FFFFFFF)
        key = jnp.where(kc * TK + krow <= qpos, key, INT_MIN)
        key_ref[kc] = key
        hi_ref[kc] = (key >> 16).astype(i16)
        lo_ref[kc] = ((key & 0xFFFF) - HALF).astype(i16)

    def score_pair(j, carry):
        score_chunk(2 * j)
        score_chunk(2 * j + 1)
        return carry

    lax.fori_loop(0, nch // 2, score_pair, 0)

    @pl.when(nch % 2 == 1)
    def _():
        score_chunk(nch - 1)
        key_ref[nch] = jnp.full((TK, TQ), INT_MIN, i32)
        hi_ref[nch] = jnp.full((TK, TQ), -HALF, i16)
        lo_ref[nch] = jnp.full((TK, TQ), -HALF, i16)

    npair = (nch + 1) // 2

    def count(ref, pred):
        one = jnp.ones((), ref.dtype)
        zero = jnp.zeros((), ref.dtype)
        rows = 8 * 4 // ref.dtype.itemsize

        def body(j, acc):
            hit = jnp.where(pred(ref[2 * j]), one, zero) + jnp.where(pred(ref[2 * j + 1]), one, zero)
            return acc + _fold(hit, jnp.add, rows).astype(f32)
        acc = lax.fori_loop(0, npair, body, jnp.zeros((rows, TQ), f32))
        return jnp.sum(acc, axis=0, keepdims=True)

    def kth_largest_i16(ref, want):
        def bisect(i, t_u):
            cand_u = t_u | jnp.left_shift(jnp.int32(1), 15 - i)
            cand = (cand_u - HALF).astype(i16)
            return jnp.where(count(ref, lambda k: k >= cand) >= want, cand_u, t_u)
        return lax.fori_loop(0, 16, bisect, jnp.zeros((1, TQ), i32)) - HALF

    hi_t = kth_largest_i16(hi_ref, topk)
    hi_t16 = hi_t.astype(i16)
    above = count(hi_ref, lambda k: k > hi_t16)

    def keep_low(kc, carry):
        lo_ref[kc] = jnp.where(hi_ref[kc] == hi_t16, lo_ref[kc], jnp.full((), -HALF, i16))
        return carry

    lax.fori_loop(0, 2 * npair, keep_low, 0)
    lo_t = kth_largest_i16(lo_ref, topk - above)
    thr = jnp.maximum((hi_t << 16) | (lo_t + HALF), INT_MIN + 1)
    cnt_ge = count(key_ref, lambda k: k >= thr)

    @pl.when(jnp.max(cnt_ge) > topk)
    def _():
        need = topk - (cnt_ge - count(key_ref, lambda k: k == thr))
        tri = (_col_iota((TK, TK)) <= _row_iota((TK, TK))).astype(bf16)

        def body(kc, seen):
            k = key_ref[kc]
            tie = k == thr
            tief = jnp.where(tie, 1.0, 0.0)
            pref = jnp.dot(tri, tief.astype(bf16), preferred_element_type=f32) + seen
            key_ref[kc] = jnp.where(tie & (pref > need), INT_MIN, k)
            return seen + jnp.sum(tief, axis=0, keepdims=True)

        lax.fori_loop(0, nch, body, jnp.zeros((1, TQ), f32))

    qs = [_pair_rows(qT_ref, h) for h in range(HG)]

    def finish(kind, h, s, keep):
        s = s * (HEAD_DIM ** -0.5 * LOG2E)
        if kind < 2:
            s = s + bias_ref[h, kind]
        return jnp.where(keep, s, NEG)

    _attend(qi, HG, lambda kc, h: jnp.dot(_k_chunk(k_ref, kc, h // 2), qs[h], preferred_element_type=f32),
            finish, lambda kc, h: vT_ref[kc, h * HEAD_DIM:(h + 1) * HEAD_DIM, :],
            m_ref, l_ref, acc_ref, shared=lambda kc, kind: key_ref[kc] >= thr)
    outT = jnp.concatenate([_flash_result(h, l_ref, acc_ref) for h in range(HG)], axis=0)
    o_ref[...] = outT.T.astype(o_ref.dtype)


def _dsa(hr, ht, hs, bias):
    B, L, _ = hr.shape
    nch = L // TK
    topk = min(DSA_TOPK, L // 4)
    return pl.pallas_call(
        functools.partial(_dsa_kernel, topk=topk),
        grid=(B, L // TQ),
        in_specs=[
            _q_spec(T_AQ), _q_spec(T_AQI, 2 * GW), _q_spec(0, NS),
            _k_spec(L, R_AK), _k_spec(L, R_AKI, LANES), _vT_spec(nch, T_AV), _BIAS_SPEC,
        ],
        out_specs=_OUT_SPEC,
        out_shape=jax.ShapeDtypeStruct((B, L, GW), bf16),
        scratch_shapes=[pltpu.VMEM((nch, TK, TQ), i32), pltpu.VMEM((nch, TK, TQ), i16),
                        pltpu.VMEM((nch, TK, TQ), i16)] + _flash_scratch(HG),
        compiler_params=_cparams("parallel", "arbitrary"),
        name="dsa",
    )(ht, ht, hs, hr, hr, ht, bias)


def _moba_kernel(qT_ref, k_ref, vT_ref, bias_ref, o_ref, km_ref, sel_ref, m_ref, l_ref, acc_ref, *, nch, topk):
    qi = pl.program_id(1)
    nb = km_ref.shape[0]

    @pl.when(qi == 0)
    def _():
        km_ref[...] = jnp.zeros(km_ref.shape, f32)
        for n in range(nch):
            blk = k_ref[n * MOBA_BLOCK:(n + 1) * MOBA_BLOCK, :].astype(f32)
            km_ref[n:n + 1, :] = jnp.sum(blk, axis=0, keepdims=True) * (1.0 / MOBA_BLOCK)

    km = km_ref[...]
    km_hi = km.astype(bf16)
    km_lo = (km - km_hi.astype(f32)).astype(bf16)
    blk_id = _row_iota((nb, 1))
    blk_f = blk_id.astype(f32)
    for h in range(HG):
        qh = qT_ref[...]
        r = _row_iota((GW, 1))
        qh = jnp.where((r >= h * HEAD_DIM) & (r < (h + 1) * HEAD_DIM), qh, jnp.zeros_like(qh))
        gate = (jnp.dot(km_hi, qh, preferred_element_type=f32)
                + jnp.dot(km_lo, qh, preferred_element_type=f32))
        gate = jnp.where(blk_id < qi, gate, -jnp.inf)
        sel = jnp.zeros((nb, TQ), f32)
        for _ in range(topk):
            best = jnp.max(gate, axis=0, keepdims=True)
            first = jnp.min(jnp.where(gate == best, blk_f, float(nb)), axis=0, keepdims=True)
            hit = blk_f == first
            sel = jnp.where(hit & (best > -jnp.inf), 1.0, sel)
            gate = jnp.where(hit, -jnp.inf, gate)
        sel_ref[h] = sel

    qs = [_pair_rows(qT_ref, h) for h in range(HG)]
    causal = _row_iota((TK, 1)) <= _col_iota((1, TQ))

    def finish(kind, h, s, kc):
        s = s * (HEAD_DIM ** -0.5 * LOG2E)
        if kind < 2:
            s = s + bias_ref[h, kind]
        if kind == 0:
            return jnp.where(causal, s, NEG)
        return s, (1.0 - sel_ref[h, pl.ds(kc, 1), :]) * NEG

    _attend(qi, HG, lambda kc, h: jnp.dot(_k_chunk(k_ref, kc, h // 2), qs[h], preferred_element_type=f32),
            finish, lambda kc, h: vT_ref[kc, h * HEAD_DIM:(h + 1) * HEAD_DIM, :],
            m_ref, l_ref, acc_ref, shared=lambda kc, kind: kc)
    outT = jnp.concatenate([_flash_result(h, l_ref, acc_ref) for h in range(HG)], axis=0)
    o_ref[...] = outT.T.astype(o_ref.dtype)


def _moba(hr, ht, bias):
    B, L, _ = hr.shape
    nch = L // TK
    nb = -(-nch // 8) * 8
    topk = min(MOBA_TOPK, nch - 1)
    return pl.pallas_call(
        functools.partial(_moba_kernel, nch=nch, topk=topk),
        grid=(B, L // TQ),
        in_specs=[_q_spec(T_CQ), _k_spec(L, R_CK), _vT_spec(nch, T_CV), _BIAS_SPEC],
        out_specs=_OUT_SPEC,
        out_shape=jax.ShapeDtypeStruct((B, L, GW), bf16),
        scratch_shapes=[pltpu.VMEM((nb, GW), f32), pltpu.VMEM((HG, nb, TQ), f32)] + _flash_scratch(HG),
        compiler_params=_cparams("parallel", "arbitrary"),
        name="moba",
    )(ht, hr, ht, bias)


def _diff_kernel(qT_ref, k_ref, vT_ref, bias_ref, lq1_ref, lk1_ref, lq2_ref, lk2_ref, li_ref, g_ref,
                 o_ref, m_ref, l_ref, acc_ref):
    qi = pl.program_id(1)
    lam_init = li_ref[...]
    lam = (jnp.exp(jnp.sum(lq1_ref[...] * lk1_ref[...], axis=1, keepdims=True))
           - jnp.exp(jnp.sum(lq2_ref[...] * lk2_ref[...], axis=1, keepdims=True)) + lam_init)
    qs = [_pair_rows(qT_ref, h, DIFF_HALF, c * DIFF_HALF) for h in range(HG) for c in range(2)]
    causal = _row_iota((TK, 1)) <= _col_iota((1, TQ))

    def finish(kind, c, s, _):
        s = s * (DIFF_HALF ** -0.5 * LOG2E)
        if kind < 2:
            s = s + bias_ref[c // 2, kind]
        return jnp.where(causal, s, NEG) if kind == 0 else s

    _attend(qi, 2 * HG, lambda kc, c: jnp.dot(_k_chunk(k_ref, kc, c // 4), qs[c], preferred_element_type=f32),
            finish, lambda kc, c: vT_ref[kc, (c // 2) * HEAD_DIM:(c // 2 + 1) * HEAD_DIM, :],
            m_ref, l_ref, acc_ref)
    heads = []
    for h in range(HG):
        o = _flash_result(2 * h, l_ref, acc_ref) - lam * _flash_result(2 * h + 1, l_ref, acc_ref)
        ms = jnp.mean(o * o, axis=0, keepdims=True)
        heads.append(o * lax.rsqrt(ms + 1e-6) * g_ref[...] * (1.0 - lam_init))
    o_ref[...] = jnp.concatenate(heads, axis=0).T.astype(o_ref.dtype)


def _diff(hr, ht, bias, lq1, lk1, lq2, lk2, lam_init, g_col):
    B, L, _ = hr.shape
    nch = L // TK
    vec = pl.BlockSpec((1, DIFF_HALF), lambda b, i: (0, 0))
    return pl.pallas_call(
        _diff_kernel,
        grid=(B, L // TQ),
        in_specs=[
            _q_spec(T_DQ), _k_spec(L, R_DK), _vT_spec(nch, T_DV), _BIAS_SPEC,
            vec, vec, vec, vec,
            pl.BlockSpec((1, 1), lambda b, i: (0, 0)),
            pl.BlockSpec((HEAD_DIM, 1), lambda b, i: (0, 0)),
        ],
        out_specs=_OUT_SPEC,
        out_shape=jax.ShapeDtypeStruct((B, L, GW), bf16),
        scratch_shapes=_flash_scratch(2 * HG),
        compiler_params=_cparams("parallel", "arbitrary"),
        name="diff",
    )(ht, hr, ht, bias, lq1, lk1, lq2, lk2, lam_init, g_col)


def _gelu_tanh(x):
    return 0.5 * x * (1.0 + jnp.tanh(math.sqrt(2.0 / math.pi) * (x + 0.044715 * (x * x * x))))


def _compress_kernel(xk_ref, xv_ref, pk_ref, pv_ref, w1k_ref, w1v_ref, w2k_ref, w2vT_ref, kc_ref, vcT_ref):
    def pre_act(x_ref, p_ref, w1_ref):
        x = x_ref[...].astype(f32)
        first = jnp.dot((x + p_ref[0:1, :]).astype(bf16), w1_ref[0], preferred_element_type=f32)
        second = jnp.dot((x + p_ref[1:2, :]).astype(bf16), w1_ref[1], preferred_element_type=f32)
        n = first.shape[0]
        return _gelu_tanh(first + pltpu.roll(second, n - 1, 0)).astype(bf16)

    gk = pre_act(xk_ref, pk_ref, w1k_ref)
    kc_ref[...] = jnp.dot(gk, w2k_ref[...], preferred_element_type=f32).astype(bf16)
    gv = pre_act(xv_ref, pv_ref, w1v_ref)
    vcT_ref[...] = lax.dot_general(w2vT_ref[...], gv, (((1,), (1,)), ((), ())),
                                   preferred_element_type=f32).astype(bf16)


def _compress(xk, xv, pk, pv, w1k, w1v, w2k, w2vT):
    B, n, W = xk.shape
    xspec = pl.BlockSpec((None, n, W), lambda b: (b, 0, 0))
    full = lambda a: pl.BlockSpec(a.shape, lambda b: (0,) * a.ndim)
    return pl.pallas_call(
        _compress_kernel,
        grid=(B,),
        in_specs=[xspec, xspec, full(pk), full(pv), full(w1k), full(w1v), full(w2k), full(w2vT)],
        out_specs=[
            pl.BlockSpec((None, n, LANES), lambda b: (b, 0, 0)),
            pl.BlockSpec((None, HEAD_DIM, n), lambda b: (b, 0, 0)),
        ],
        out_shape=[
            jax.ShapeDtypeStruct((B, n, LANES), bf16),
            jax.ShapeDtypeStruct((B, HEAD_DIM, n), bf16),
        ],
        compiler_params=_cparams("parallel"),
        name="nsa_compress",
    )(xk, xv, pk, pv, w1k, w1v, w2k, w2vT)


def _split3(x):
    hi = x.astype(bf16)
    r = x - hi.astype(f32)
    mid = r.astype(bf16)
    lo = (r - mid.astype(f32)).astype(bf16)
    return hi, mid, lo


def _nsa_kernel(qT_ref, gT_ref, kc_ref, vcT_ref, ks_ref, vsT_ref, kw_ref, vwT_ref, ovlT_ref, exp_ref,
                bias_ref, o_ref, m_ref, l_ref, acc_ref, *, n_slc, topn):
    qi = pl.program_id(1)
    ncmp = kc_ref.shape[0]
    qpos = qi * TQ + _col_iota((1, TQ))
    pad = jnp.zeros((LANES - HEAD_DIM, TQ), bf16)
    qs = [jnp.concatenate([qT_ref[h * HEAD_DIM:(h + 1) * HEAD_DIM, :], pad], axis=0)
          for h in range(HG)]

    cmp_ok = _row_iota((ncmp, 1)) * CMP_STRIDE + (CMP_LEN - 1) <= qpos
    o_cmp = []
    pc_sum = jnp.zeros((ncmp, TQ), f32)
    for h in range(HG):
        s = jnp.dot(kc_ref[...], qs[h], preferred_element_type=f32) * (HEAD_DIM ** -0.5 * LOG2E)
        s = jnp.where(cmp_ok, s, NEG)
        e = jnp.exp2(s - jnp.maximum(jnp.max(s, axis=0, keepdims=True), FLOOR))
        den = jnp.sum(e, axis=0, keepdims=True)
        pc = e / jnp.where(den > 0.0, den, 1.0)
        pc_sum = pc_sum + pc
        o_cmp.append(jnp.dot(vcT_ref[...], pc.astype(bf16), preferred_element_type=f32))
    nb = ovlT_ref.shape[0]
    imp = jnp.zeros((nb, TQ), f32)
    for part in _split3(pc_sum):
        imp = imp + jnp.dot(ovlT_ref[...], part, preferred_element_type=f32)
    blk = _row_iota((nb, 1))
    blk_f = blk.astype(f32)
    cur = qpos // SLC_LEN
    forced = (blk == 0) | (blk == cur) | (blk == cur - 1)
    imp = jnp.where(forced, jnp.inf, imp)
    imp = jnp.where((blk * SLC_LEN <= qpos) & (blk < n_slc), imp, -jnp.inf)

    def pick(_, st):
        imp, sel = st
        best = jnp.max(imp, axis=0, keepdims=True)
        first = jnp.min(jnp.where(imp == best, blk_f, float(nb)), axis=0, keepdims=True)
        hit = blk_f == first
        return jnp.where(hit, -jnp.inf, imp), jnp.where(hit & (best > -jnp.inf), 1.0, sel)

    _, sel = lax.fori_loop(0, topn, pick, (imp, jnp.zeros((nb, TQ), f32)))

    krow = _row_iota((TK, 1))
    qcol = _col_iota((1, TQ))
    causal = krow <= qcol

    selb = sel.astype(bf16)

    def slc_keep(kc, kind):
        keep = jnp.dot(exp_ref[kc], selb, preferred_element_type=f32) > 0.5
        return keep & causal if kind == 0 else keep

    def slc_finish(kind, h, s, keep):
        s = s * (HEAD_DIM ** -0.5 * LOG2E)
        if kind < 2:
            s = s + bias_ref[h, kind]
        return jnp.where(keep, s, NEG)

    _attend(qi, HG, lambda kc, h: jnp.dot(_k_chunk(ks_ref, kc), qs[h], preferred_element_type=f32),
            slc_finish, lambda kc, h: vsT_ref[kc], m_ref, l_ref, acc_ref, shared=slc_keep)
    o_slc = [_flash_result(h, l_ref, acc_ref) for h in range(HG)]

    def win_finish(kind, h, s, _):
        s = s * (HEAD_DIM ** -0.5 * LOG2E)
        if kind < 2:
            s = s + bias_ref[h, kind]
        if kind == 0:
            return jnp.where(causal, s, NEG)
        return jnp.where(krow > qcol, s, NEG) if kind == 2 else s

    _attend(qi, HG, lambda kc, h: jnp.dot(_k_chunk(kw_ref, kc), qs[h], preferred_element_type=f32),
            win_finish, lambda kc, h: vwT_ref[kc], m_ref, l_ref, acc_ref,
            first=jnp.maximum(qi - WIN // TK, 0), max_far=WIN // TK - 1)
    o_win = [_flash_result(h, l_ref, acc_ref) for h in range(HG)]

    gates = jax.nn.sigmoid(gT_ref[...])
    heads = []
    for h in range(HG):
        c = S_BG + 3 * h
        heads.append(gates[c:c + 1, :] * o_cmp[h] + gates[c + 1:c + 2, :] * o_slc[h]
                     + gates[c + 2:c + 3, :] * o_win[h])
    o_ref[...] = jnp.concatenate(heads, axis=0).T.astype(o_ref.dtype)


def _nsa(hr, ht, hs, kc, vcT, ovlT, expand, bias):
    B, L, _ = hr.shape
    nch = L // TK
    n = kc.shape[1]
    n_slc = L // SLC_LEN
    topn = min(SLC_TOPN, n_slc)
    return pl.pallas_call(
        functools.partial(_nsa_kernel, n_slc=n_slc, topn=topn),
        grid=(B, L // TQ),
        in_specs=[
            _q_spec(T_BQ), _q_spec(0, NS),
            pl.BlockSpec((None, n, LANES), lambda b, i: (b, 0, 0)),
            pl.BlockSpec((None, HEAD_DIM, n), lambda b, i: (b, 0, 0)),
            _k_spec(L, R_BKS, LANES), _vT_spec(nch, T_BVS, HEAD_DIM),
            _k_spec(L, R_BKW, LANES), _vT_spec(nch, T_BVW, HEAD_DIM),
            pl.BlockSpec(ovlT.shape, lambda b, i: (0, 0)),
            pl.BlockSpec(expand.shape, lambda b, i: (0, 0, 0)),
            _BIAS_SPEC,
        ],
        out_specs=_OUT_SPEC,
        out_shape=jax.ShapeDtypeStruct((B, L, GW), bf16),
        scratch_shapes=_flash_scratch(HG),
        compiler_params=_cparams("parallel", "arbitrary"),
        name="nsa",
    )(ht, hs, kc, vcT, hr, ht, hr, ht, ovlT, expand, bias)


def _nsa_tables(L):
    n = L // CMP_STRIDE
    n_slc = L // SLC_LEN
    nb = -(-n_slc // 16) * 16
    c0 = jnp.arange(n, dtype=i32)[None, :] * CMP_STRIDE
    s0 = jnp.arange(nb, dtype=i32)[:, None] * SLC_LEN
    ovlT = (c0 <= s0 + SLC_LEN - 1) & (c0 + CMP_LEN - 1 >= s0)
    ovlT = ovlT & (jnp.arange(n)[None, :] < n - 1) & (jnp.arange(nb)[:, None] < n_slc)
    tokblk = jnp.arange(L, dtype=i32) // SLC_LEN
    expand = tokblk[:, None] == jnp.arange(nb, dtype=i32)[None, :]
    return ovlT.astype(bf16), expand.reshape(L // TK, TK, nb).astype(bf16)


def _memkv_kernel(mem_ref, wkT_ref, wv_ref, kT_ref, v_ref):
    mb = mem_ref[...].astype(bf16)
    kT_ref[...] = lax.dot_general(wkT_ref[...], mb, (((1,), (1,)), ((), ())),
                                  preferred_element_type=f32).astype(bf16)
    v_ref[...] = jnp.dot(mb, wv_ref[...], preferred_element_type=f32).astype(bf16)


def _memkv(mem, wkT, wv):
    B, N, D = mem.shape
    wspec = pl.BlockSpec((D, D), lambda b: (0, 0))
    return pl.pallas_call(
        _memkv_kernel,
        grid=(B,),
        in_specs=[pl.BlockSpec((None, N, D), lambda b: (b, 0, 0)), wspec, wspec],
        out_specs=[pl.BlockSpec((None, D, N), lambda b: (b, 0, 0)),
                   pl.BlockSpec((None, N, D), lambda b: (b, 0, 0))],
        out_shape=[jax.ShapeDtypeStruct((B, D, N), bf16), jax.ShapeDtypeStruct((B, N, D), bf16)],
        compiler_params=_cparams("parallel"),
        name="cross_kv",
    )(mem, wkT, wv)


def _mix_cross_kernel(oa_ref, ob_ref, oc_ref, od_ref, w_ref, x_ref, g1_ref, b1_ref,
                      wq_ref, kT_ref, v_ref, wo_ref, g_ref, b_ref, y_ref):
    mix = jnp.zeros(x_ref.shape, f32)
    for n, o_ref in enumerate((oa_ref, ob_ref, oc_ref, od_ref)):
        mix = mix + jnp.dot(o_ref[...], w_ref[n * GW:(n + 1) * GW, :], preferred_element_type=f32)
    x = _layer_norm(ALPHA * x_ref[...] + mix, g1_ref[...], b1_ref[...])
    q = jnp.dot(x.astype(bf16), wq_ref[...], preferred_element_type=f32).astype(bf16)
    outs = []
    for h in range(CROSS_HEADS):
        sl = slice(h * CROSS_DIM, (h + 1) * CROSS_DIM)
        s = jnp.dot(q[:, sl], kT_ref[sl, :], preferred_element_type=f32) * CROSS_DIM ** -0.5
        e = jnp.exp(s - jnp.max(s, axis=1, keepdims=True))
        p = e / jnp.sum(e, axis=1, keepdims=True)
        outs.append(jnp.dot(p.astype(bf16), v_ref[:, sl], preferred_element_type=f32).astype(bf16))
    o = jnp.concatenate(outs, axis=1)
    y = ALPHA * x + jnp.dot(o, wo_ref[...], preferred_element_type=f32)
    y_ref[...] = _layer_norm(y, g_ref[...], b_ref[...])


def _mix_cross(oa, ob, oc, od, w, x, g1, b1, wq, kT, v, wo, g, b, tm=512):
    B, L, D = x.shape
    N = v.shape[1]
    ospec = pl.BlockSpec((None, tm, GW), lambda bb, i: (bb, i, 0))
    xspec = pl.BlockSpec((None, tm, D), lambda bb, i: (bb, i, 0))
    wspec = pl.BlockSpec((D, D), lambda bb, i: (0, 0))
    vec = pl.BlockSpec((1, D), lambda bb, i: (0, 0))
    return pl.pallas_call(
        _mix_cross_kernel,
        grid=(B, L // tm),
        in_specs=[ospec, ospec, ospec, ospec, wspec, xspec, vec, vec, wspec,
                  pl.BlockSpec((None, D, N), lambda bb, i: (bb, 0, 0)),
                  pl.BlockSpec((None, N, D), lambda bb, i: (bb, 0, 0)), wspec, vec, vec],
        out_specs=xspec,
        out_shape=jax.ShapeDtypeStruct((B, L, D), f32),
        compiler_params=_cparams("parallel", "parallel"),
        name="mix_cross_ln",
    )(oa, ob, oc, od, w, x, g1, b1, wq, kT, v, wo, g, b)


def _mlp_kernel(x_ref, w1_ref, w2_ref, g_ref, b_ref, y_ref, acc_ref):
    j = pl.program_id(1)

    @pl.when(j == 0)
    def _():
        acc_ref[...] = jnp.zeros(acc_ref.shape, f32)

    hdn = jnp.dot(x_ref[...].astype(bf16), w1_ref[...], preferred_element_type=f32)
    hdn = jnp.square(jnp.maximum(hdn, 0.0)).astype(bf16)
    acc_ref[...] += jnp.dot(hdn, w2_ref[...], preferred_element_type=f32)

    @pl.when(j == pl.num_programs(1) - 1)
    def _():
        y_ref[...] = _layer_norm(ALPHA * x_ref[...] + acc_ref[...], g_ref[...], b_ref[...])


def _mlp(x, w1, w2, g, b, tm=1024, tf=1024):
    M, D = x.shape
    F = w1.shape[1]
    vec = pl.BlockSpec((1, D), lambda i, j: (0, 0))
    return pl.pallas_call(
        _mlp_kernel,
        grid=(M // tm, F // tf),
        in_specs=[pl.BlockSpec((tm, D), lambda i, j: (i, 0)),
                  pl.BlockSpec((D, tf), lambda i, j: (0, j)),
                  pl.BlockSpec((tf, D), lambda i, j: (j, 0)), vec, vec],
        out_specs=pl.BlockSpec((tm, D), lambda i, j: (i, 0)),
        out_shape=jax.ShapeDtypeStruct((M, D), f32),
        scratch_shapes=[pltpu.VMEM((tm, D), f32)],
        compiler_params=_cparams("parallel", "arbitrary"),
        name="mlp_ln",
    )(x, w1, w2, g, b)


def _split_w_in(w):
    sizes = (GW, GW, GW, IDX_HEADS * IDX_DIM, IDX_DIM, IDX_HEADS,
             GW, HEAD_DIM, HEAD_DIM, HEAD_DIM, HEAD_DIM, HEAD_DIM, HEAD_DIM, 3 * HG,
             GW, GW, GW, GW, GW, GW)
    offs = [0]
    for s in sizes:
        offs.append(offs[-1] + s)
    return [w[:, offs[n]:offs[n + 1]] for n in range(len(sizes))]


def _layout_w_in(w):
    (a_q, a_k, a_v, a_qi, a_ki, a_w, b_q, b_kc, b_vc, b_ks, b_vs, b_kw, b_vw, b_g,
     c_q, c_k, c_v, d_q, d_k, d_v) = _split_w_in(w)
    twice = lambda t: jnp.concatenate([t, t], axis=1)
    wr = jnp.concatenate([a_k, c_k, d_k, twice(b_ks), twice(b_kw), twice(a_ki), b_kc, b_vc], axis=1)
    wt = jnp.concatenate([a_q, a_v, a_qi, b_q, c_q, c_v, d_q, d_v, b_vs, b_vw], axis=1).T
    ws = jnp.concatenate([a_w, b_g, jnp.zeros((w.shape[0], NS - IDX_HEADS - 3 * HG), w.dtype)], axis=1).T
    assert wr.shape[1] == NR and wt.shape[0] == NT
    return wr.astype(bf16), wt.astype(bf16), ws.astype(bf16)


def _mixers(x, bias, w_in_l, pos_k, pos_v, w1_k, w2_k, w1_v, w2_v, lq1, lk1, lq2, lk2, diff_g, lam_init):
    B, L, _ = x.shape
    wr, wt, ws = _layout_w_in(w_in_l)
    hr, ht, hs = _project(x, wr, wt, ws)
    o_a = _dsa(hr, ht, hs, bias[0:HG])

    n = L // CMP_STRIDE
    half = CMP_STRIDE * HEAD_DIM
    xk = hr[:, :, R_BKVC:R_BKVC + HEAD_DIM].reshape(B, n, half)
    xv = hr[:, :, R_BKVC + HEAD_DIM:R_BKVC + 2 * HEAD_DIM].reshape(B, n, half)
    kc, vcT = _compress(
        xk, xv, pos_k.reshape(2, half), pos_v.reshape(2, half),
        w1_k.reshape(2, half, HEAD_DIM).astype(bf16), w1_v.reshape(2, half, HEAD_DIM).astype(bf16),
        jnp.concatenate([w2_k, w2_k], axis=1).astype(bf16), w2_v.T.astype(bf16))
    ovlT, expand = _nsa_tables(L)
    o_b = _nsa(hr, ht, hs, kc, vcT, ovlT, expand, bias[HG:2 * HG])
    o_c = _moba(hr, ht, bias[2 * HG:3 * HG])
    o_d = _diff(hr, ht, bias[3 * HG:4 * HG], lq1.reshape(1, -1), lk1.reshape(1, -1),
                lq2.reshape(1, -1), lk2.reshape(1, -1), jnp.full((1, 1), lam_init, f32),
                diff_g.reshape(-1, 1))
    return o_a, o_b, o_c, o_d


def kernel(x, mem, rel_bias, w_in, w_out, nsa_pos_k, nsa_pos_v, nsa_w1_k, nsa_w2_k, nsa_w1_v, nsa_w2_v, diff_lq1, diff_lk1, diff_lq2, diff_lk2, diff_g, ln1_g, ln1_b, xq, xk, xv, xo, ln2_g, ln2_b, mlp_w1, mlp_w2, ln3_g, ln3_b):
    B, L, D = x.shape
    bias = _bias_tiles(rel_bias)
    for l in range(DEPTH):
        lam_init = 0.8 - 0.6 * math.exp(-0.3 * l)
        o_a, o_b, o_c, o_d = _mixers(
            x, bias, w_in[l], nsa_pos_k[l], nsa_pos_v[l], nsa_w1_k[l], nsa_w2_k[l],
            nsa_w1_v[l], nsa_w2_v[l], diff_lq1[l], diff_lk1[l], diff_lq2[l], diff_lk2[l],
            diff_g[l], lam_init)
        flat = lambda t: t.reshape(B * L, -1)
        row = lambda t: t.reshape(1, D)
        kT, v = _memkv(mem, xk[l].T.astype(bf16), xv[l].astype(bf16))
        x3 = _mix_cross(o_a, o_b, o_c, o_d, w_out[l].astype(bf16), x, row(ln1_g[l]), row(ln1_b[l]),
                        xq[l].astype(bf16), kT, v, xo[l].astype(bf16), row(ln2_g[l]), row(ln2_b[l]))
        x = _mlp(flat(x3), mlp_w1[l].astype(bf16), mlp_w2[l].astype(bf16),
                 row(ln3_g[l]), row(ln3_b[l])).reshape(B, L, D)
    return x
```

```python
import functools
import math

import jax
import jax.numpy as jnp
from jax import lax
from jax.experimental import pallas as pl
from jax.experimental.pallas import tpu as pltpu

f32 = jnp.float32
bf16 = jnp.bfloat16
i32 = jnp.int32
i16 = jnp.int16

D_MODEL = 1024
DEPTH = 4
HEAD_DIM = 64
HG = 4
GW = HG * HEAD_DIM
REL_BUCKETS = 32
REL_MAX_DIST = 128
IDX_HEADS = 8
IDX_DIM = 64
DSA_TOPK = 256
CMP_LEN = 32
CMP_STRIDE = 16
SLC_LEN = 64
SLC_TOPN = 16
WIN = 512
MOBA_BLOCK = 256
MOBA_TOPK = 3
DIFF_HALF = HEAD_DIM // 2
CROSS_HEADS = 4
CROSS_DIM = D_MODEL // CROSS_HEADS
D_FF = 4 * D_MODEL
ALPHA = (2 * DEPTH) ** 0.25

NEG = -1e30
FLOOR = -1e29
INT_MIN = -(2 ** 31)
HALF = 2 ** 15
LANES = 128
TQ = 256
TK = 256
VMEM_LIMIT = 56 * 1024 * 1024
LOG2E = math.log2(math.e)
FAR_GROUP = 4
MATMULS_AHEAD = 6

R_AK, R_CK, R_DK, R_BKS, R_BKW, R_AKI, R_BKVC = 0, 256, 512, 768, 896, 1024, 1152
NR = 1280
T_AQ, T_AV, T_AQI, T_BQ, T_CQ, T_CV, T_DQ, T_DV, T_BVS, T_BVW = (
    0, 256, 512, 1024, 1280, 1536, 1792, 2048, 2304, 2368)
NT = 2432
S_AW, S_BG, NS = 0, 8, 32


def _cparams(*sem):
    return pltpu.CompilerParams(dimension_semantics=sem, vmem_limit_bytes=VMEM_LIMIT)


def _rel_bucket(dist):
    n = jnp.maximum(dist, 0)
    max_exact = REL_BUCKETS // 2
    nf = jnp.maximum(n, max_exact).astype(f32)
    large = max_exact + (jnp.log(nf / max_exact) / math.log(REL_MAX_DIST / max_exact)
                         * (REL_BUCKETS - max_exact)).astype(i32)
    large = jnp.minimum(large, REL_BUCKETS - 1)
    return jnp.where(n < max_exact, n, large)


def _bias_tiles(rel_bias):
    assert 2 * TQ - TK + 1 >= REL_MAX_DIST
    d = jnp.arange(TQ, dtype=i32)[None, :] - jnp.arange(TK, dtype=i32)[:, None]
    bucket = jnp.stack([_rel_bucket(d + off) for off in (0, TQ)])
    far = rel_bias[_rel_bucket(jnp.int32(2 * TQ))]
    table = ((rel_bias - far) * LOG2E).astype(f32)
    onehot = (bucket[None] == jnp.arange(REL_BUCKETS, dtype=i32)[:, None, None, None]).astype(f32)
    return jnp.einsum('bntq,bh->hntq', onehot, table, precision=lax.Precision.HIGHEST)


def _row_iota(shape):
    return lax.broadcasted_iota(i32, shape, 0)


def _col_iota(shape):
    return lax.broadcasted_iota(i32, shape, 1)


def _flash_init(m_ref, l_ref, acc_ref):
    m_ref[...] = jnp.full(m_ref.shape, NEG, f32)
    l_ref[...] = jnp.zeros(l_ref.shape, f32)
    acc_ref[...] = jnp.zeros(acc_ref.shape, f32)


def _fold(x, op, rows):
    while x.shape[0] > rows:
        half = x.shape[0] // 2
        x = op(x[:half], x[half:])
    return x


def _flash_step(c, s, vT, m_ref, l_ref, acc_ref, pen=None):
    m_prev = m_ref[c]
    m_chunk = jnp.max(_fold(s, jnp.maximum, 8), axis=0, keepdims=True)
    if pen is not None:
        m_chunk = m_chunk + pen
    m_new = jnp.maximum(m_prev, m_chunk)
    m_use = jnp.maximum(m_new, FLOOR)
    alpha = jnp.exp2(jnp.maximum(m_prev, FLOOR) - m_use)
    p = jnp.exp2(s - (m_use if pen is None else m_use - pen))
    l_ref[c] = alpha * l_ref[c] + jnp.sum(_fold(p, jnp.add, 8), axis=0, keepdims=True)
    acc_ref[c] = alpha * acc_ref[c] + jnp.dot(vT, p.astype(bf16), preferred_element_type=f32)
    m_ref[c] = m_new


def _flash_result(c, l_ref, acc_ref):
    l = l_ref[c]
    return jnp.where(l > 0.0, acc_ref[c] / jnp.where(l > 0.0, l, 1.0), 0.0)


def _attend(qi, n, raw, finish, vT, m_ref, l_ref, acc_ref, first=0, shared=None, max_far=None):
    def run(chunks):
        items = [(kc, kind, c) for kc, kind in chunks for c in range(n)]
        ctx, logits = {}, {}

        def issue(j):
            kc, kind, c = items[j]
            if c == 0 and shared is not None:
                ctx[j // n] = shared(kc, kind)
            logits[j] = raw(kc, c)

        for j in range(min(MATMULS_AHEAD, len(items))):
            issue(j)
        for j, (kc, kind, c) in enumerate(items):
            if j + MATMULS_AHEAD < len(items):
                issue(j + MATMULS_AHEAD)
            s = finish(kind, c, logits.pop(j), ctx.get(j // n))
            s, pen = s if isinstance(s, tuple) else (s, None)
            _flash_step(c, s, vT(kc, c), m_ref, l_ref, acc_ref, pen)

    _flash_init(m_ref, l_ref, acc_ref)
    n_far = jnp.maximum(qi - 1 - first, 0)
    group = FAR_GROUP
    if max_far is not None and max_far < group:
        group = max_far + 1

    def far_group(j, carry):
        run([(first + group * j + t, 2) for t in range(group)])
        return carry

    lax.fori_loop(0, n_far // group, far_group, 0)
    done = first + (n_far // group) * group
    for rem in range(group):
        @pl.when((qi >= 1) & (n_far % group == rem))
        def _(rem=rem):
            run([(done + t, 2) for t in range(rem)] + [(qi - 1, 1), (qi, 0)])

    @pl.when(qi == 0)
    def _():
        run([(qi, 0)])


def _pair_rows(qT_ref, h, width=HEAD_DIM, offset=0):
    blk = qT_ref[(h // 2) * LANES:(h // 2 + 1) * LANES, :]
    r = _row_iota((LANES, 1))
    lo = (h % 2) * HEAD_DIM + offset
    return jnp.where((r >= lo) & (r < lo + width), blk, jnp.zeros_like(blk))


def _layer_norm(y, g, b):
    mu = jnp.mean(y, axis=-1, keepdims=True)
    yc = y - mu
    var = jnp.mean(yc * yc, axis=-1, keepdims=True)
    return yc * lax.rsqrt(var + 1e-5) * g + b


def _proj_kernel(x_ref, wr_ref, wt_ref, ws_ref, hr_ref, ht_ref, hs_ref):
    xb = x_ref[...].astype(bf16)
    nt = (((1,), (1,)), ((), ()))
    hr_ref[...] = jnp.dot(xb, wr_ref[...], preferred_element_type=f32).astype(bf16)
    ht_ref[...] = lax.dot_general(wt_ref[...], xb, nt, preferred_element_type=f32).astype(bf16)
    hs_ref[...] = lax.dot_general(ws_ref[...], xb, nt, preferred_element_type=f32)


def _project(x, wr, wt, ws):
    B, L, D = x.shape
    nch = L // TK
    return pl.pallas_call(
        _proj_kernel,
        grid=(B, nch),
        in_specs=[
            pl.BlockSpec((None, TK, D), lambda b, i: (b, i, 0)),
            pl.BlockSpec((D, NR), lambda b, i: (0, 0)),
            pl.BlockSpec((NT, D), lambda b, i: (0, 0)),
            pl.BlockSpec((NS, D), lambda b, i: (0, 0)),
        ],
        out_specs=[
            pl.BlockSpec((None, TK, NR), lambda b, i: (b, i, 0)),
            pl.BlockSpec((None, None, NT, TK), lambda b, i: (b, i, 0, 0)),
            pl.BlockSpec((None, None, NS, TK), lambda b, i: (b, i, 0, 0)),
        ],
        out_shape=[
            jax.ShapeDtypeStruct((B, L, NR), bf16),
            jax.ShapeDtypeStruct((B, nch, NT, TK), bf16),
            jax.ShapeDtypeStruct((B, nch, NS, TK), f32),
        ],
        compiler_params=_cparams("parallel", "parallel"),
        name="proj",
    )(x, wr, wt, ws)


def _q_spec(off, rows=GW):
    return pl.BlockSpec((None, None, rows, TQ), lambda b, i: (b, i, off // rows, 0))


def _vT_spec(nch, off, rows=GW):
    return pl.BlockSpec((None, nch, rows, TK), lambda b, i: (b, 0, off // rows, 0))


def _k_spec(L, off, cols=GW):
    return pl.BlockSpec((None, L, cols), lambda b, i: (b, 0, off // cols))


_BIAS_SPEC = pl.BlockSpec((HG, 2, TK, TQ), lambda b, i: (0, 0, 0, 0))
_OUT_SPEC = pl.BlockSpec((None, TQ, GW), lambda b, i: (b, i, 0))


def _flash_scratch(chains, dv=HEAD_DIM):
    return [pltpu.VMEM((chains, 1, TQ), f32), pltpu.VMEM((chains, 1, TQ), f32),
            pltpu.VMEM((chains, dv, TQ), f32)]


def _k_chunk(k_ref, kc, pair=None):
    rows = pl.ds(pl.multiple_of(kc * TK, TK), TK)
    if pair is None:
        return k_ref[rows, :]
    return k_ref[rows, pair * LANES:(pair + 1) * LANES]


def _dsa_kernel(qT_ref, qiT_ref, wT_ref, k_ref, ki_ref, vT_ref, bias_ref, o_ref,
                key_ref, hi_ref, lo_ref, m_ref, l_ref, acc_ref, *, topk):
    qi = pl.program_id(1)
    nch = qi + 1
    qpos = qi * TQ + _col_iota((1, TQ))
    krow = _row_iota((TK, 1))
    wT = wT_ref[...] * (IDX_DIM ** -0.5 * IDX_HEADS ** -0.5)
    qidx = [_pair_rows(qiT_ref, h) for h in range(IDX_HEADS)]

    def score_chunk(kc):
        ki2 = _k_chunk(ki_ref, kc)
        s = jnp.zeros((TK, TQ), f32)
        for h in range(IDX_HEADS):
            r = jnp.dot(ki2, qidx[h], preferred_element_type=f32)
            s = s + wT[S_AW + h:S_AW + h + 1, :] * jnp.maximum(r, 0.0)
        s = jnp.where(s == 0.0, 0.0, s)
        bits = lax.bitcast_convert_type(s, i32)
        key = bits ^ ((bits >> 31) & 0x7FFFFFFF)
        key = jnp.where(kc * TK + krow <= qpos, key, INT_MIN)
        key_ref[kc] = key
        hi_ref[kc] = (key >> 16).astype(i16)
        lo_ref[kc] = ((key & 0xFFFF) - HALF).astype(i16)

    def score_pair(j, carry):
        score_chunk(2 * j)
        score_chunk(2 * j + 1)
        return carry

    lax.fori_loop(0, nch // 2, score_pair, 0)

    @pl.when(nch % 2 == 1)
    def _():
        score_chunk(nch - 1)
        key_ref[nch] = jnp.full((TK, TQ), INT_MIN, i32)
        hi_ref[nch] = jnp.full((TK, TQ), -HALF, i16)
        lo_ref[nch] = jnp.full((TK, TQ), -HALF, i16)

    npair = (nch + 1) // 2

    def count(ref, pred):
        one = jnp.ones((), ref.dtype)
        zero = jnp.zeros((), ref.dtype)
        rows = 8 * 4 // ref.dtype.itemsize

        def body(j, acc):
            hit = jnp.where(pred(ref[2 * j]), one, zero) + jnp.where(pred(ref[2 * j + 1]), one, zero)
            return acc + _fold(hit, jnp.add, rows).astype(f32)
        acc = lax.fori_loop(0, npair, body, jnp.zeros((rows, TQ), f32))
        return jnp.sum(acc, axis=0, keepdims=True)

    def kth_largest_i16(ref, want):
        def bisect(i, t_u):
            cand_u = t_u | jnp.left_shift(jnp.int32(1), 15 - i)
            cand = (cand_u - HALF).astype(i16)
            return jnp.where(count(ref, lambda k: k >= cand) >= want, cand_u, t_u)
        return lax.fori_loop(0, 16, bisect, jnp.zeros((1, TQ), i32)) - HALF

    hi_t = kth_largest_i16(hi_ref, topk)
    hi_t16 = hi_t.astype(i16)
    above = count(hi_ref, lambda k: k > hi_t16)

    def keep_low(kc, carry):
        lo_ref[kc] = jnp.where(hi_ref[kc] == hi_t16, lo_ref[kc], jnp.full((), -HALF, i16))
        return carry

    lax.fori_loop(0, 2 * npair, keep_low, 0)
    lo_t = kth_largest_i16(lo_ref, topk - above)
    thr = jnp.maximum((hi_t << 16) | (lo_t + HALF), INT_MIN + 1)
    cnt_ge = count(key_ref, lambda k: k >= thr)

    @pl.when(jnp.max(cnt_ge) > topk)
    def _():
        need = topk - (cnt_ge - count(key_ref, lambda k: k == thr))
        tri = (_col_iota((TK, TK)) <= _row_iota((TK, TK))).astype(bf16)

        def body(kc, seen):
            k = key_ref[kc]
            tie = k == thr
            tief = jnp.where(tie, 1.0, 0.0)
            pref = jnp.dot(tri, tief.astype(bf16), preferred_element_type=f32) + seen
            key_ref[kc] = jnp.where(tie & (pref > need), INT_MIN, k)
            return seen + jnp.sum(tief, axis=0, keepdims=True)

        lax.fori_loop(0, nch, body, jnp.zeros((1, TQ), f32))

    qs = [_pair_rows(qT_ref, h) for h in range(HG)]

    def finish(kind, h, s, keep):
        s = s * (HEAD_DIM ** -0.5 * LOG2E)
        if kind < 2:
            s = s + bias_ref[h, kind]
        return jnp.where(keep, s, NEG)

    _attend(qi, HG, lambda kc, h: jnp.dot(_k_chunk(k_ref, kc, h // 2), qs[h], preferred_element_type=f32),
            finish, lambda kc, h: vT_ref[kc, h * HEAD_DIM:(h + 1) * HEAD_DIM, :],
            m_ref, l_ref, acc_ref, shared=lambda kc, kind: key_ref[kc] >= thr)
    outT = jnp.concatenate([_flash_result(h, l_ref, acc_ref) for h in range(HG)], axis=0)
    o_ref[...] = outT.T.astype(o_ref.dtype)


def _dsa(hr, ht, hs, bias):
    B, L, _ = hr.shape
    nch = L // TK
    topk = min(DSA_TOPK, L // 4)
    return pl.pallas_call(
        functools.partial(_dsa_kernel, topk=topk),
        grid=(B, L // TQ),
        in_specs=[
            _q_spec(T_AQ), _q_spec(T_AQI, 2 * GW), _q_spec(0, NS),
            _k_spec(L, R_AK), _k_spec(L, R_AKI, LANES), _vT_spec(nch, T_AV), _BIAS_SPEC,
        ],
        out_specs=_OUT_SPEC,
        out_shape=jax.ShapeDtypeStruct((B, L, GW), bf16),
        scratch_shapes=[pltpu.VMEM((nch, TK, TQ), i32), pltpu.VMEM((nch, TK, TQ), i16),
                        pltpu.VMEM((nch, TK, TQ), i16)] + _flash_scratch(HG),
        compiler_params=_cparams("parallel", "arbitrary"),
        name="dsa",
    )(ht, ht, hs, hr, hr, ht, bias)


def _moba_kernel(qT_ref, k_ref, vT_ref, bias_ref, o_ref, km_ref, sel_ref, m_ref, l_ref, acc_ref, *, nch, topk):
    qi = pl.program_id(1)
    nb = km_ref.shape[0] // HG

    @pl.when(qi == 0)
    def _():
        km_ref[...] = jnp.zeros(km_ref.shape, f32)
        lane = _col_iota((1, GW))
        for n in range(nch):
            blk = k_ref[n * MOBA_BLOCK:(n + 1) * MOBA_BLOCK, :].astype(f32)
            mean = jnp.sum(blk, axis=0, keepdims=True) * (1.0 / MOBA_BLOCK)
            for h in range(HG):
                own = (lane >= h * HEAD_DIM) & (lane < (h + 1) * HEAD_DIM)
                km_ref[h * nb + n:h * nb + n + 1, :] = jnp.where(own, mean, 0.0)

    km = km_ref[...]
    km_hi = km.astype(bf16)
    km_lo = (km - km_hi.astype(f32)).astype(bf16)
    qT = qT_ref[...]
    gates = (jnp.dot(km_hi, qT, preferred_element_type=f32)
             + jnp.dot(km_lo, qT, preferred_element_type=f32))
    blk_id = _row_iota((nb, 1))
    blk_f = blk_id.astype(f32)
    for h in range(HG):
        gate = jnp.where(blk_id < qi, gates[h * nb:(h + 1) * nb], -jnp.inf)
        sel = jnp.zeros((nb, TQ), f32)
        for _ in range(topk):
            best = jnp.max(gate, axis=0, keepdims=True)
            first = jnp.min(jnp.where(gate == best, blk_f, float(nb)), axis=0, keepdims=True)
            hit = blk_f == first
            sel = jnp.where(hit & (best > -jnp.inf), 1.0, sel)
            gate = jnp.where(hit, -jnp.inf, gate)
        sel_ref[h] = sel

    qs = [_pair_rows(qT_ref, h) for h in range(HG)]
    causal = _row_iota((TK, 1)) <= _col_iota((1, TQ))

    def finish(kind, h, s, kc):
        s = s * (HEAD_DIM ** -0.5 * LOG2E)
        if kind < 2:
            s = s + bias_ref[h, kind]
        if kind == 0:
            return jnp.where(causal, s, NEG)
        return s, (1.0 - sel_ref[h, pl.ds(kc, 1), :]) * NEG

    _attend(qi, HG, lambda kc, h: jnp.dot(_k_chunk(k_ref, kc, h // 2), qs[h], preferred_element_type=f32),
            finish, lambda kc, h: vT_ref[kc, h * HEAD_DIM:(h + 1) * HEAD_DIM, :],
            m_ref, l_ref, acc_ref, shared=lambda kc, kind: kc)
    outT = jnp.concatenate([_flash_result(h, l_ref, acc_ref) for h in range(HG)], axis=0)
    o_ref[...] = outT.T.astype(o_ref.dtype)


def _moba(hr, ht, bias):
    B, L, _ = hr.shape
    nch = L // TK
    nb = -(-nch // 8) * 8
    topk = min(MOBA_TOPK, nch - 1)
    return pl.pallas_call(
        functools.partial(_moba_kernel, nch=nch, topk=topk),
        grid=(B, L // TQ),
        in_specs=[_q_spec(T_CQ), _k_spec(L, R_CK), _vT_spec(nch, T_CV), _BIAS_SPEC],
        out_specs=_OUT_SPEC,
        out_shape=jax.ShapeDtypeStruct((B, L, GW), bf16),
        scratch_shapes=[pltpu.VMEM((HG * nb, GW), f32), pltpu.VMEM((HG, nb, TQ), f32)] + _flash_scratch(HG),
        compiler_params=_cparams("parallel", "arbitrary"),
        name="moba",
    )(ht, hr, ht, bias)


def _diff_kernel(qT_ref, k_ref, vT_ref, bias_ref, lq1_ref, lk1_ref, lq2_ref, lk2_ref, li_ref, g_ref,
                 o_ref, m_ref, l_ref, acc_ref):
    qi = pl.program_id(1)
    lam_init = li_ref[...]
    lam = (jnp.exp(jnp.sum(lq1_ref[...] * lk1_ref[...], axis=1, keepdims=True))
           - jnp.exp(jnp.sum(lq2_ref[...] * lk2_ref[...], axis=1, keepdims=True)) + lam_init)
    qs = [_pair_rows(qT_ref, h, DIFF_HALF, c * DIFF_HALF) for h in range(HG) for c in range(2)]
    causal = _row_iota((TK, 1)) <= _col_iota((1, TQ))

    def finish(kind, c, s, _):
        s = s * (DIFF_HALF ** -0.5 * LOG2E)
        if kind < 2:
            s = s + bias_ref[c // 2, kind]
        return jnp.where(causal, s, NEG) if kind == 0 else s

    _attend(qi, 2 * HG, lambda kc, c: jnp.dot(_k_chunk(k_ref, kc, c // 4), qs[c], preferred_element_type=f32),
            finish, lambda kc, c: vT_ref[kc, (c // 2) * HEAD_DIM:(c // 2 + 1) * HEAD_DIM, :],
            m_ref, l_ref, acc_ref)
    heads = []
    for h in range(HG):
        o = _flash_result(2 * h, l_ref, acc_ref) - lam * _flash_result(2 * h + 1, l_ref, acc_ref)
        ms = jnp.mean(o * o, axis=0, keepdims=True)
        heads.append(o * lax.rsqrt(ms + 1e-6) * g_ref[...] * (1.0 - lam_init))
    o_ref[...] = jnp.concatenate(heads, axis=0).T.astype(o_ref.dtype)


def _diff(hr, ht, bias, lq1, lk1, lq2, lk2, lam_init, g_col):
    B, L, _ = hr.shape
    nch = L // TK
    vec = pl.BlockSpec((1, DIFF_HALF), lambda b, i: (0, 0))
    return pl.pallas_call(
        _diff_kernel,
        grid=(B, L // TQ),
        in_specs=[
            _q_spec(T_DQ), _k_spec(L, R_DK), _vT_spec(nch, T_DV), _BIAS_SPEC,
            vec, vec, vec, vec,
            pl.BlockSpec((1, 1), lambda b, i: (0, 0)),
            pl.BlockSpec((HEAD_DIM, 1), lambda b, i: (0, 0)),
        ],
        out_specs=_OUT_SPEC,
        out_shape=jax.ShapeDtypeStruct((B, L, GW), bf16),
        scratch_shapes=_flash_scratch(2 * HG),
        compiler_params=_cparams("parallel", "arbitrary"),
        name="diff",
    )(ht, hr, ht, bias, lq1, lk1, lq2, lk2, lam_init, g_col)


def _gelu_tanh(x):
    return 0.5 * x * (1.0 + jnp.tanh(math.sqrt(2.0 / math.pi) * (x + 0.044715 * (x * x * x))))


def _compress_kernel(xk_ref, xv_ref, pk_ref, pv_ref, w1k_ref, w1v_ref, w2k_ref, w2vT_ref, kc_ref, vcT_ref):
    def pre_act(x_ref, p_ref, w1_ref):
        x = x_ref[...].astype(f32)
        first = jnp.dot((x + p_ref[0:1, :]).astype(bf16), w1_ref[0], preferred_element_type=f32)
        second = jnp.dot((x + p_ref[1:2, :]).astype(bf16), w1_ref[1], preferred_element_type=f32)
        n = first.shape[0]
        return _gelu_tanh(first + pltpu.roll(second, n - 1, 0)).astype(bf16)

    gk = pre_act(xk_ref, pk_ref, w1k_ref)
    kc_ref[...] = jnp.dot(gk, w2k_ref[...], preferred_element_type=f32).astype(bf16)
    gv = pre_act(xv_ref, pv_ref, w1v_ref)
    vcT_ref[...] = lax.dot_general(w2vT_ref[...], gv, (((1,), (1,)), ((), ())),
                                   preferred_element_type=f32).astype(bf16)


def _compress(xk, xv, pk, pv, w1k, w1v, w2k, w2vT):
    B, n, W = xk.shape
    xspec = pl.BlockSpec((None, n, W), lambda b: (b, 0, 0))
    full = lambda a: pl.BlockSpec(a.shape, lambda b: (0,) * a.ndim)
    return pl.pallas_call(
        _compress_kernel,
        grid=(B,),
        in_specs=[xspec, xspec, full(pk), full(pv), full(w1k), full(w1v), full(w2k), full(w2vT)],
        out_specs=[
            pl.BlockSpec((None, n, LANES), lambda b: (b, 0, 0)),
            pl.BlockSpec((None, HEAD_DIM, n), lambda b: (b, 0, 0)),
        ],
        out_shape=[
            jax.ShapeDtypeStruct((B, n, LANES), bf16),
            jax.ShapeDtypeStruct((B, HEAD_DIM, n), bf16),
        ],
        compiler_params=_cparams("parallel"),
        name="nsa_compress",
    )(xk, xv, pk, pv, w1k, w1v, w2k, w2vT)


def _split3(x):
    hi = x.astype(bf16)
    r = x - hi.astype(f32)
    mid = r.astype(bf16)
    lo = (r - mid.astype(f32)).astype(bf16)
    return hi, mid, lo


def _nsa_kernel(qT_ref, gT_ref, kc_ref, vcT_ref, ks_ref, vsT_ref, kw_ref, vwT_ref, ovlT_ref, exp_ref,
                bias_ref, o_ref, m_ref, l_ref, acc_ref, *, n_slc, topn):
    qi = pl.program_id(1)
    ncmp = kc_ref.shape[0]
    qpos = qi * TQ + _col_iota((1, TQ))
    pad = jnp.zeros((LANES - HEAD_DIM, TQ), bf16)
    qs = [jnp.concatenate([qT_ref[h * HEAD_DIM:(h + 1) * HEAD_DIM, :], pad], axis=0)
          for h in range(HG)]

    cmp_ok = _row_iota((ncmp, 1)) * CMP_STRIDE + (CMP_LEN - 1) <= qpos
    o_cmp = []
    pc_sum = jnp.zeros((ncmp, TQ), f32)
    for h in range(HG):
        s = jnp.dot(kc_ref[...], qs[h], preferred_element_type=f32) * (HEAD_DIM ** -0.5 * LOG2E)
        s = jnp.where(cmp_ok, s, NEG)
        e = jnp.exp2(s - jnp.maximum(jnp.max(s, axis=0, keepdims=True), FLOOR))
        den = jnp.sum(e, axis=0, keepdims=True)
        pc = e / jnp.where(den > 0.0, den, 1.0)
        pc_sum = pc_sum + pc
        o_cmp.append(jnp.dot(vcT_ref[...], pc.astype(bf16), preferred_element_type=f32))
    nb = ovlT_ref.shape[0]
    imp = jnp.zeros((nb, TQ), f32)
    for part in _split3(pc_sum):
        imp = imp + jnp.dot(ovlT_ref[...], part, preferred_element_type=f32)
    blk = _row_iota((nb, 1))
    blk_f = blk.astype(f32)
    cur = qpos // SLC_LEN
    forced = (blk == 0) | (blk == cur) | (blk == cur - 1)
    imp = jnp.where(forced, jnp.inf, imp)
    imp = jnp.where((blk * SLC_LEN <= qpos) & (blk < n_slc), imp, -jnp.inf)

    def pick(_, st):
        imp, sel = st
        best = jnp.max(imp, axis=0, keepdims=True)
        first = jnp.min(jnp.where(imp == best, blk_f, float(nb)), axis=0, keepdims=True)
        hit = blk_f == first
        return jnp.where(hit, -jnp.inf, imp), jnp.where(hit & (best > -jnp.inf), 1.0, sel)

    _, sel = lax.fori_loop(0, topn, pick, (imp, jnp.zeros((nb, TQ), f32)))

    krow = _row_iota((TK, 1))
    qcol = _col_iota((1, TQ))
    causal = krow <= qcol

    selb = sel.astype(bf16)

    def slc_keep(kc, kind):
        keep = jnp.dot(exp_ref[kc], selb, preferred_element_type=f32) > 0.5
        return keep & causal if kind == 0 else keep

    def slc_finish(kind, h, s, keep):
        s = s * (HEAD_DIM ** -0.5 * LOG2E)
        if kind < 2:
            s = s + bias_ref[h, kind]
        return jnp.where(keep, s, NEG)

    _attend(qi, HG, lambda kc, h: jnp.dot(_k_chunk(ks_ref, kc), qs[h], preferred_element_type=f32),
            slc_finish, lambda kc, h: vsT_ref[kc], m_ref, l_ref, acc_ref, shared=slc_keep)
    o_slc = [_flash_result(h, l_ref, acc_ref) for h in range(HG)]

    def win_finish(kind, h, s, _):
        s = s * (HEAD_DIM ** -0.5 * LOG2E)
        if kind < 2:
            s = s + bias_ref[h, kind]
        if kind == 0:
            return jnp.where(causal, s, NEG)
        return jnp.where(krow > qcol, s, NEG) if kind == 2 else s

    _attend(qi, HG, lambda kc, h: jnp.dot(_k_chunk(kw_ref, kc), qs[h], preferred_element_type=f32),
            win_finish, lambda kc, h: vwT_ref[kc], m_ref, l_ref, acc_ref,
            first=jnp.maximum(qi - WIN // TK, 0), max_far=WIN // TK - 1)
    o_win = [_flash_result(h, l_ref, acc_ref) for h in range(HG)]

    gates = jax.nn.sigmoid(gT_ref[...])
    heads = []
    for h in range(HG):
        c = S_BG + 3 * h
        heads.append(gates[c:c + 1, :] * o_cmp[h] + gates[c + 1:c + 2, :] * o_slc[h]
                     + gates[c + 2:c + 3, :] * o_win[h])
    o_ref[...] = jnp.concatenate(heads, axis=0).T.astype(o_ref.dtype)


def _nsa(hr, ht, hs, kc, vcT, ovlT, expand, bias):
    B, L, _ = hr.shape
    nch = L // TK
    n = kc.shape[1]
    n_slc = L // SLC_LEN
    topn = min(SLC_TOPN, n_slc)
    return pl.pallas_call(
        functools.partial(_nsa_kernel, n_slc=n_slc, topn=topn),
        grid=(B, L // TQ),
        in_specs=[
            _q_spec(T_BQ), _q_spec(0, NS),
            pl.BlockSpec((None, n, LANES), lambda b, i: (b, 0, 0)),
            pl.BlockSpec((None, HEAD_DIM, n), lambda b, i: (b, 0, 0)),
            _k_spec(L, R_BKS, LANES), _vT_spec(nch, T_BVS, HEAD_DIM),
            _k_spec(L, R_BKW, LANES), _vT_spec(nch, T_BVW, HEAD_DIM),
            pl.BlockSpec(ovlT.shape, lambda b, i: (0, 0)),
            pl.BlockSpec(expand.shape, lambda b, i: (0, 0, 0)),
            _BIAS_SPEC,
        ],
        out_specs=_OUT_SPEC,
        out_shape=jax.ShapeDtypeStruct((B, L, GW), bf16),
        scratch_shapes=_flash_scratch(HG),
        compiler_params=_cparams("parallel", "arbitrary"),
        name="nsa",
    )(ht, hs, kc, vcT, hr, ht, hr, ht, ovlT, expand, bias)


def _nsa_tables(L):
    n = L // CMP_STRIDE
    n_slc = L // SLC_LEN
    nb = -(-n_slc // 16) * 16
    c0 = jnp.arange(n, dtype=i32)[None, :] * CMP_STRIDE
    s0 = jnp.arange(nb, dtype=i32)[:, None] * SLC_LEN
    ovlT = (c0 <= s0 + SLC_LEN - 1) & (c0 + CMP_LEN - 1 >= s0)
    ovlT = ovlT & (jnp.arange(n)[None, :] < n - 1) & (jnp.arange(nb)[:, None] < n_slc)
    tokblk = jnp.arange(L, dtype=i32) // SLC_LEN
    expand = tokblk[:, None] == jnp.arange(nb, dtype=i32)[None, :]
    return ovlT.astype(bf16), expand.reshape(L // TK, TK, nb).astype(bf16)


def _memkv_kernel(mem_ref, wkT_ref, wv_ref, kT_ref, v_ref):
    mb = mem_ref[...].astype(bf16)
    kT_ref[...] = lax.dot_general(wkT_ref[...], mb, (((1,), (1,)), ((), ())),
                                  preferred_element_type=f32).astype(bf16)
    v_ref[...] = jnp.dot(mb, wv_ref[...], preferred_element_type=f32).astype(bf16)


def _memkv(mem, wkT, wv):
    B, N, D = mem.shape
    wspec = pl.BlockSpec((D, D), lambda b: (0, 0))
    return pl.pallas_call(
        _memkv_kernel,
        grid=(B,),
        in_specs=[pl.BlockSpec((None, N, D), lambda b: (b, 0, 0)), wspec, wspec],
        out_specs=[pl.BlockSpec((None, D, N), lambda b: (b, 0, 0)),
                   pl.BlockSpec((None, N, D), lambda b: (b, 0, 0))],
        out_shape=[jax.ShapeDtypeStruct((B, D, N), bf16), jax.ShapeDtypeStruct((B, N, D), bf16)],
        compiler_params=_cparams("parallel"),
        name="cross_kv",
    )(mem, wkT, wv)


def _mix_cross_kernel(oa_ref, ob_ref, oc_ref, od_ref, w_ref, x_ref, g1_ref, b1_ref,
                      wq_ref, kT_ref, v_ref, wo_ref, g_ref, b_ref, y_ref):
    mix = jnp.zeros(x_ref.shape, f32)
    for n, o_ref in enumerate((oa_ref, ob_ref, oc_ref, od_ref)):
        mix = mix + jnp.dot(o_ref[...], w_ref[n * GW:(n + 1) * GW, :], preferred_element_type=f32)
    x = _layer_norm(ALPHA * x_ref[...] + mix, g1_ref[...], b1_ref[...])
    q = jnp.dot(x.astype(bf16), wq_ref[...], preferred_element_type=f32).astype(bf16)
    outs = []
    for h in range(CROSS_HEADS):
        sl = slice(h * CROSS_DIM, (h + 1) * CROSS_DIM)
        s = jnp.dot(q[:, sl], kT_ref[sl, :], preferred_element_type=f32) * CROSS_DIM ** -0.5
        e = jnp.exp(s - jnp.max(s, axis=1, keepdims=True))
        p = e / jnp.sum(e, axis=1, keepdims=True)
        outs.append(jnp.dot(p.astype(bf16), v_ref[:, sl], preferred_element_type=f32).astype(bf16))
    o = jnp.concatenate(outs, axis=1)
    y = ALPHA * x + jnp.dot(o, wo_ref[...], preferred_element_type=f32)
    y_ref[...] = _layer_norm(y, g_ref[...], b_ref[...])


def _mix_cross(oa, ob, oc, od, w, x, g1, b1, wq, kT, v, wo, g, b, tm=512):
    B, L, D = x.shape
    N = v.shape[1]
    ospec = pl.BlockSpec((None, tm, GW), lambda bb, i: (bb, i, 0))
    xspec = pl.BlockSpec((None, tm, D), lambda bb, i: (bb, i, 0))
    wspec = pl.BlockSpec((D, D), lambda bb, i: (0, 0))
    vec = pl.BlockSpec((1, D), lambda bb, i: (0, 0))
    return pl.pallas_call(
        _mix_cross_kernel,
        grid=(B, L // tm),
        in_specs=[ospec, ospec, ospec, ospec, wspec, xspec, vec, vec, wspec,
                  pl.BlockSpec((None, D, N), lambda bb, i: (bb, 0, 0)),
                  pl.BlockSpec((None, N, D), lambda bb, i: (bb, 0, 0)), wspec, vec, vec],
        out_specs=xspec,
        out_shape=jax.ShapeDtypeStruct((B, L, D), f32),
        compiler_params=_cparams("parallel", "parallel"),
        name="mix_cross_ln",
    )(oa, ob, oc, od, w, x, g1, b1, wq, kT, v, wo, g, b)


def _mlp_kernel(x_ref, w1_ref, w2_ref, g_ref, b_ref, y_ref, acc_ref):
    j = pl.program_id(1)

    @pl.when(j == 0)
    def _():
        acc_ref[...] = jnp.zeros(acc_ref.shape, f32)

    hdn = jnp.dot(x_ref[...].astype(bf16), w1_ref[...], preferred_element_type=f32)
    hdn = jnp.square(jnp.maximum(hdn, 0.0)).astype(bf16)
    acc_ref[...] += jnp.dot(hdn, w2_ref[...], preferred_element_type=f32)

    @pl.when(j == pl.num_programs(1) - 1)
    def _():
        y_ref[...] = _layer_norm(ALPHA * x_ref[...] + acc_ref[...], g_ref[...], b_ref[...])


def _mlp(x, w1, w2, g, b, tm=1024, tf=1024):
    M, D = x.shape
    F = w1.shape[1]
    vec = pl.BlockSpec((1, D), lambda i, j: (0, 0))
    return pl.pallas_call(
        _mlp_kernel,
        grid=(M // tm, F // tf),
        in_specs=[pl.BlockSpec((tm, D), lambda i, j: (i, 0)),
                  pl.BlockSpec((D, tf), lambda i, j: (0, j)),
                  pl.BlockSpec((tf, D), lambda i, j: (j, 0)), vec, vec],
        out_specs=pl.BlockSpec((tm, D), lambda i, j: (i, 0)),
        out_shape=jax.ShapeDtypeStruct((M, D), f32),
        scratch_shapes=[pltpu.VMEM((tm, D), f32)],
        compiler_params=_cparams("parallel", "arbitrary"),
        name="mlp_ln",
    )(x, w1, w2, g, b)


def _split_w_in(w):
    sizes = (GW, GW, GW, IDX_HEADS * IDX_DIM, IDX_DIM, IDX_HEADS,
             GW, HEAD_DIM, HEAD_DIM, HEAD_DIM, HEAD_DIM, HEAD_DIM, HEAD_DIM, 3 * HG,
             GW, GW, GW, GW, GW, GW)
    offs = [0]
    for s in sizes:
        offs.append(offs[-1] + s)
    return [w[:, offs[n]:offs[n + 1]] for n in range(len(sizes))]


def _layout_w_in(w):
    (a_q, a_k, a_v, a_qi, a_ki, a_w, b_q, b_kc, b_vc, b_ks, b_vs, b_kw, b_vw, b_g,
     c_q, c_k, c_v, d_q, d_k, d_v) = _split_w_in(w)
    twice = lambda t: jnp.concatenate([t, t], axis=1)
    wr = jnp.concatenate([a_k, c_k, d_k, twice(b_ks), twice(b_kw), twice(a_ki), b_kc, b_vc], axis=1)
    wt = jnp.concatenate([a_q, a_v, a_qi, b_q, c_q, c_v, d_q, d_v, b_vs, b_vw], axis=1).T
    ws = jnp.concatenate([a_w, b_g, jnp.zeros((w.shape[0], NS - IDX_HEADS - 3 * HG), w.dtype)], axis=1).T
    assert wr.shape[1] == NR and wt.shape[0] == NT
    return wr.astype(bf16), wt.astype(bf16), ws.astype(bf16)


def _mixers(x, bias, w_in_l, pos_k, pos_v, w1_k, w2_k, w1_v, w2_v, lq1, lk1, lq2, lk2, diff_g, lam_init):
    B, L, _ = x.shape
    wr, wt, ws = _layout_w_in(w_in_l)
    hr, ht, hs = _project(x, wr, wt, ws)
    o_a = _dsa(hr, ht, hs, bias[0:HG])

    n = L // CMP_STRIDE
    half = CMP_STRIDE * HEAD_DIM
    xk = hr[:, :, R_BKVC:R_BKVC + HEAD_DIM].reshape(B, n, half)
    xv = hr[:, :, R_BKVC + HEAD_DIM:R_BKVC + 2 * HEAD_DIM].reshape(B, n, half)
    kc, vcT = _compress(
        xk, xv, pos_k.reshape(2, half), pos_v.reshape(2, half),
        w1_k.reshape(2, half, HEAD_DIM).astype(bf16), w1_v.reshape(2, half, HEAD_DIM).astype(bf16),
        jnp.concatenate([w2_k, w2_k], axis=1).astype(bf16), w2_v.T.astype(bf16))
    ovlT, expand = _nsa_tables(L)
    o_b = _nsa(hr, ht, hs, kc, vcT, ovlT, expand, bias[HG:2 * HG])
    o_c = _moba(hr, ht, bias[2 * HG:3 * HG])
    o_d = _diff(hr, ht, bias[3 * HG:4 * HG], lq1.reshape(1, -1), lk1.reshape(1, -1),
                lq2.reshape(1, -1), lk2.reshape(1, -1), jnp.full((1, 1), lam_init, f32),
                diff_g.reshape(-1, 1))
    return o_a, o_b, o_c, o_d


def kernel(x, mem, rel_bias, w_in, w_out, nsa_pos_k, nsa_pos_v, nsa_w1_k, nsa_w2_k, nsa_w1_v, nsa_w2_v, diff_lq1, diff_lk1, diff_lq2, diff_lk2, diff_g, ln1_g, ln1_b, xq, xk, xv, xo, ln2_g, ln2_b, mlp_w1, mlp_w2, ln3_g, ln3_b):
    B, L, D = x.shape
    bias = _bias_tiles(rel_bias)
    for l in range(DEPTH):
        lam_init = 0.8 - 0.6 * math.exp(-0.3 * l)
        o_a, o_b, o_c, o_d = _mixers(
            x, bias, w_in[l], nsa_pos_k[l], nsa_pos_v[l], nsa_w1_k[l], nsa_w2_k[l],
            nsa_w1_v[l], nsa_w2_v[l], diff_lq1[l], diff_lk1[l], diff_lq2[l], diff_lk2[l],
            diff_g[l], lam_init)
        flat = lambda t: t.reshape(B * L, -1)
        row = lambda t: t.reshape(1, D)
        kT, v = _memkv(mem, xk[l].T.astype(bf16), xv[l].astype(bf16))
        x3 = _mix_cross(o_a, o_b, o_c, o_d, w_out[l].astype(bf16), x, row(ln1_g[l]), row(ln1_b[l]),
                        xq[l].astype(bf16), kT, v, xo[l].astype(bf16), row(ln2_g[l]), row(ln2_b[l]))
        x = _mlp(flat(x3), mlp_w1[l].astype(bf16), mlp_w2[l].astype(bf16),
                 row(ln3_g[l]), row(ln3_b[l])).reshape(B, L, D)
    return x
```

```python
import functools
import math

import jax
import jax.numpy as jnp
from jax import lax
from jax.experimental import pallas as pl
from jax.experimental.pallas import tpu as pltpu

f32 = jnp.float32
bf16 = jnp.bfloat16
i32 = jnp.int32
i16 = jnp.int16

D_MODEL = 1024
DEPTH = 4
HEAD_DIM = 64
HG = 4
GW = HG * HEAD_DIM
REL_BUCKETS = 32
REL_MAX_DIST = 128
IDX_HEADS = 8
IDX_DIM = 64
DSA_TOPK = 256
CMP_LEN = 32
CMP_STRIDE = 16
SLC_LEN = 64
SLC_TOPN = 16
WIN = 512
MOBA_BLOCK = 256
MOBA_TOPK = 3
DIFF_HALF = HEAD_DIM // 2
CROSS_HEADS = 4
CROSS_DIM = D_MODEL // CROSS_HEADS
D_FF = 4 * D_MODEL
ALPHA = (2 * DEPTH) ** 0.25

NEG = -1e30
FLOOR = -1e29
INT_MIN = -(2 ** 31)
HALF = 2 ** 15
LANES = 128
TQ = 256
TK = 256
VMEM_LIMIT = 56 * 1024 * 1024
LOG2E = math.log2(math.e)
FAR_GROUP = 4
MATMULS_AHEAD = 6

R_AK, R_CK, R_DK, R_BKS, R_BKW, R_AKI, R_BKVC = 0, 256, 512, 768, 896, 1024, 1152
NR = 1280
T_AQ, T_AV, T_AQI, T_BQ, T_CQ, T_CV, T_DQ, T_DV, T_BVS, T_BVW = (
    0, 256, 512, 1024, 1280, 1536, 1792, 2048, 2304, 2368)
NT = 2432
S_AW, S_BG, NS = 0, 8, 32


def _cparams(*sem):
    return pltpu.CompilerParams(dimension_semantics=sem, vmem_limit_bytes=VMEM_LIMIT)


def _rel_bucket(dist):
    n = jnp.maximum(dist, 0)
    max_exact = REL_BUCKETS // 2
    nf = jnp.maximum(n, max_exact).astype(f32)
    large = max_exact + (jnp.log(nf / max_exact) / math.log(REL_MAX_DIST / max_exact)
                         * (REL_BUCKETS - max_exact)).astype(i32)
    large = jnp.minimum(large, REL_BUCKETS - 1)
    return jnp.where(n < max_exact, n, large)


def _bias_tiles(rel_bias):
    assert 2 * TQ - TK + 1 >= REL_MAX_DIST
    d = jnp.arange(TQ, dtype=i32)[None, :] - jnp.arange(TK, dtype=i32)[:, None]
    bucket = jnp.stack([_rel_bucket(d + off) for off in (0, TQ)])
    far = rel_bias[_rel_bucket(jnp.int32(2 * TQ))]
    table = ((rel_bias - far) * LOG2E).astype(f32)
    onehot = (bucket[None] == jnp.arange(REL_BUCKETS, dtype=i32)[:, None, None, None]).astype(f32)
    return jnp.einsum('bntq,bh->hntq', onehot, table, precision=lax.Precision.HIGHEST)


def _row_iota(shape):
    return lax.broadcasted_iota(i32, shape, 0)


def _col_iota(shape):
    return lax.broadcasted_iota(i32, shape, 1)


def _flash_init(m_ref, l_ref, acc_ref):
    m_ref[...] = jnp.full(m_ref.shape, NEG, f32)
    l_ref[...] = jnp.zeros(l_ref.shape, f32)
    acc_ref[...] = jnp.zeros(acc_ref.shape, f32)


def _fold(x, op, rows):
    while x.shape[0] > rows:
        half = x.shape[0] // 2
        x = op(x[:half], x[half:])
    return x


def _flash_step(c, s, vT, m_ref, l_ref, acc_ref, pen=None):
    m_prev = m_ref[c]
    m_chunk = jnp.max(_fold(s, jnp.maximum, 8), axis=0, keepdims=True)
    if pen is not None:
        m_chunk = m_chunk + pen
    m_new = jnp.maximum(m_prev, m_chunk)
    m_use = jnp.maximum(m_new, FLOOR)
    alpha = jnp.exp2(jnp.maximum(m_prev, FLOOR) - m_use)
    p = jnp.exp2(s - (m_use if pen is None else m_use - pen))
    l_ref[c] = alpha * l_ref[c] + jnp.sum(_fold(p, jnp.add, 8), axis=0, keepdims=True)
    acc_ref[c] = alpha * acc_ref[c] + jnp.dot(vT, p.astype(bf16), preferred_element_type=f32)
    m_ref[c] = m_new


def _flash_result(c, l_ref, acc_ref):
    l = l_ref[c]
    return jnp.where(l > 0.0, acc_ref[c] / jnp.where(l > 0.0, l, 1.0), 0.0)


def _attend(qi, n, raw, finish, vT, m_ref, l_ref, acc_ref, first=0, shared=None, max_far=None):
    def run(chunks):
        items = [(kc, kind, c) for kc, kind in chunks for c in range(n)]
        ctx, logits = {}, {}

        def issue(j):
            kc, kind, c = items[j]
            if c == 0 and shared is not None:
                ctx[j // n] = shared(kc, kind)
            logits[j] = raw(kc, c)

        for j in range(min(MATMULS_AHEAD, len(items))):
            issue(j)
        for j, (kc, kind, c) in enumerate(items):
            if j + MATMULS_AHEAD < len(items):
                issue(j + MATMULS_AHEAD)
            s = finish(kind, c, logits.pop(j), ctx.get(j // n))
            s, pen = s if isinstance(s, tuple) else (s, None)
            _flash_step(c, s, vT(kc, c), m_ref, l_ref, acc_ref, pen)

    _flash_init(m_ref, l_ref, acc_ref)
    n_far = jnp.maximum(qi - 1 - first, 0)
    group = FAR_GROUP
    if max_far is not None and max_far < group:
        group = max_far + 1

    def far_group(j, carry):
        run([(first + group * j + t, 2) for t in range(group)])
        return carry

    lax.fori_loop(0, n_far // group, far_group, 0)
    done = first + (n_far // group) * group
    for rem in range(group):
        @pl.when((qi >= 1) & (n_far % group == rem))
        def _(rem=rem):
            run([(done + t, 2) for t in range(rem)] + [(qi - 1, 1), (qi, 0)])

    @pl.when(qi == 0)
    def _():
        run([(qi, 0)])


def _pair_rows(qT_ref, h, width=HEAD_DIM, offset=0):
    blk = qT_ref[(h // 2) * LANES:(h // 2 + 1) * LANES, :]
    r = _row_iota((LANES, 1))
    lo = (h % 2) * HEAD_DIM + offset
    return jnp.where((r >= lo) & (r < lo + width), blk, jnp.zeros_like(blk))


def _layer_norm(y, g, b):
    mu = jnp.mean(y, axis=-1, keepdims=True)
    yc = y - mu
    var = jnp.mean(yc * yc, axis=-1, keepdims=True)
    return yc * lax.rsqrt(var + 1e-5) * g + b


def _proj_kernel(x_ref, wr_ref, wt_ref, ws_ref, hr_ref, ht_ref, hs_ref):
    xb = x_ref[...].astype(bf16)
    nt = (((1,), (1,)), ((), ()))
    hr_ref[...] = jnp.dot(xb, wr_ref[...], preferred_element_type=f32).astype(bf16)
    ht_ref[...] = lax.dot_general(wt_ref[...], xb, nt, preferred_element_type=f32).astype(bf16)
    hs_ref[...] = lax.dot_general(ws_ref[...], xb, nt, preferred_element_type=f32)


def _project(x, wr, wt, ws):
    B, L, D = x.shape
    nch = L // TK
    return pl.pallas_call(
        _proj_kernel,
        grid=(B, nch),
        in_specs=[
            pl.BlockSpec((None, TK, D), lambda b, i: (b, i, 0)),
            pl.BlockSpec((D, NR), lambda b, i: (0, 0)),
            pl.BlockSpec((NT, D), lambda b, i: (0, 0)),
            pl.BlockSpec((NS, D), lambda b, i: (0, 0)),
        ],
        out_specs=[
            pl.BlockSpec((None, TK, NR), lambda b, i: (b, i, 0)),
            pl.BlockSpec((None, None, NT, TK), lambda b, i: (b, i, 0, 0)),
            pl.BlockSpec((None, None, NS, TK), lambda b, i: (b, i, 0, 0)),
        ],
        out_shape=[
            jax.ShapeDtypeStruct((B, L, NR), bf16),
            jax.ShapeDtypeStruct((B, nch, NT, TK), bf16),
            jax.ShapeDtypeStruct((B, nch, NS, TK), f32),
        ],
        compiler_params=_cparams("parallel", "parallel"),
        name="proj",
    )(x, wr, wt, ws)


def _q_spec(off, rows=GW):
    return pl.BlockSpec((None, None, rows, TQ), lambda b, i: (b, i, off // rows, 0))


def _vT_spec(nch, off, rows=GW):
    return pl.BlockSpec((None, nch, rows, TK), lambda b, i: (b, 0, off // rows, 0))


def _k_spec(L, off, cols=GW):
    return pl.BlockSpec((None, L, cols), lambda b, i: (b, 0, off // cols))


_BIAS_SPEC = pl.BlockSpec((HG, 2, TK, TQ), lambda b, i: (0, 0, 0, 0))
_OUT_SPEC = pl.BlockSpec((None, TQ, GW), lambda b, i: (b, i, 0))


def _flash_scratch(chains, dv=HEAD_DIM):
    return [pltpu.VMEM((chains, 1, TQ), f32), pltpu.VMEM((chains, 1, TQ), f32),
            pltpu.VMEM((chains, dv, TQ), f32)]


def _k_chunk(k_ref, kc, pair=None):
    rows = pl.ds(pl.multiple_of(kc * TK, TK), TK)
    if pair is None:
        return k_ref[rows, :]
    return k_ref[rows, pair * LANES:(pair + 1) * LANES]


def _dsa_kernel(qT_ref, qiT_ref, wT_ref, k_ref, ki_ref, vT_ref, bias_ref, o_ref,
                key_ref, hi_ref, lo_ref, m_ref, l_ref, acc_ref, *, topk):
    qi = pl.program_id(1)
    nch = qi + 1
    qpos = qi * TQ + _col_iota((1, TQ))
    krow = _row_iota((TK, 1))
    wT = wT_ref[...] * (IDX_DIM ** -0.5 * IDX_HEADS ** -0.5)
    qidx = [_pair_rows(qiT_ref, h) for h in range(IDX_HEADS)]

    def score_chunk(kc):
        ki2 = _k_chunk(ki_ref, kc)
        s = jnp.zeros((TK, TQ), f32)
        for h in range(IDX_HEADS):
            r = jnp.dot(ki2, qidx[h], preferred_element_type=f32)
            s = s + wT[S_AW + h:S_AW + h + 1, :] * jnp.maximum(r, 0.0)
        s = jnp.where(s == 0.0, 0.0, s)
        bits = lax.bitcast_convert_type(s, i32)
        key = bits ^ ((bits >> 31) & 0x7FFFFFFF)
        key = jnp.where(kc * TK + krow <= qpos, key, INT_MIN)
        key_ref[kc] = key
        hi_ref[kc] = (key >> 16).astype(i16)
        lo_ref[kc] = ((key & 0xFFFF) - HALF).astype(i16)

    assert FAR_GROUP == 4

    def score_group(j, carry):
        for t in range(FAR_GROUP):
            score_chunk(FAR_GROUP * j + t)
        return carry

    lax.fori_loop(0, nch // FAR_GROUP, score_group, 0)

    @pl.when((nch // 2) % 2 == 1)
    def _():
        score_chunk(nch - nch % 2 - 2)
        score_chunk(nch - nch % 2 - 1)

    @pl.when(nch % 2 == 1)
    def _():
        score_chunk(nch - 1)
        key_ref[nch] = jnp.full((TK, TQ), INT_MIN, i32)
        hi_ref[nch] = jnp.full((TK, TQ), -HALF, i16)
        lo_ref[nch] = jnp.full((TK, TQ), -HALF, i16)

    npair = (nch + 1) // 2

    def count(ref, pred):
        one = jnp.ones((), ref.dtype)
        zero = jnp.zeros((), ref.dtype)
        rows = 8 * 4 // ref.dtype.itemsize

        def body(j, acc):
            hit = jnp.where(pred(ref[2 * j]), one, zero) + jnp.where(pred(ref[2 * j + 1]), one, zero)
            return acc + _fold(hit, jnp.add, rows).astype(f32)
        acc = lax.fori_loop(0, npair, body, jnp.zeros((rows, TQ), f32))
        return jnp.sum(acc, axis=0, keepdims=True)

    def kth_largest_i16(ref, want):
        def bisect(i, t_u):
            cand_u = t_u | jnp.left_shift(jnp.int32(1), 15 - i)
            cand = (cand_u - HALF).astype(i16)
            return jnp.where(count(ref, lambda k: k >= cand) >= want, cand_u, t_u)
        return lax.fori_loop(0, 16, bisect, jnp.zeros((1, TQ), i32)) - HALF

    hi_t = kth_largest_i16(hi_ref, topk)
    hi_t16 = hi_t.astype(i16)
    above = count(hi_ref, lambda k: k > hi_t16)

    def keep_low(kc, carry):
        lo_ref[kc] = jnp.where(hi_ref[kc] == hi_t16, lo_ref[kc], jnp.full((), -HALF, i16))
        return carry

    lax.fori_loop(0, 2 * npair, keep_low, 0)
    lo_t = kth_largest_i16(lo_ref, topk - above)
    thr = jnp.maximum((hi_t << 16) | (lo_t + HALF), INT_MIN + 1)
    cnt_ge = count(key_ref, lambda k: k >= thr)

    @pl.when(jnp.max(cnt_ge) > topk)
    def _():
        need = topk - (cnt_ge - count(key_ref, lambda k: k == thr))
        tri = (_col_iota((TK, TK)) <= _row_iota((TK, TK))).astype(bf16)

        def body(kc, seen):
            k = key_ref[kc]
            tie = k == thr
            tief = jnp.where(tie, 1.0, 0.0)
            pref = jnp.dot(tri, tief.astype(bf16), preferred_element_type=f32) + seen
            key_ref[kc] = jnp.where(tie & (pref > need), INT_MIN, k)
            return seen + jnp.sum(tief, axis=0, keepdims=True)

        lax.fori_loop(0, nch, body, jnp.zeros((1, TQ), f32))

    qs = [_pair_rows(qT_ref, h) for h in range(HG)]

    def finish(kind, h, s, keep):
        s = s * (HEAD_DIM ** -0.5 * LOG2E)
        if kind < 2:
            s = s + bias_ref[h, kind]
        return jnp.where(keep, s, NEG)

    _attend(qi, HG, lambda kc, h: jnp.dot(_k_chunk(k_ref, kc, h // 2), qs[h], preferred_element_type=f32),
            finish, lambda kc, h: vT_ref[kc, h * HEAD_DIM:(h + 1) * HEAD_DIM, :],
            m_ref, l_ref, acc_ref, shared=lambda kc, kind: key_ref[kc] >= thr)
    outT = jnp.concatenate([_flash_result(h, l_ref, acc_ref) for h in range(HG)], axis=0)
    o_ref[...] = outT.T.astype(o_ref.dtype)


def _dsa(hr, ht, hs, bias):
    B, L, _ = hr.shape
    nch = L // TK
    topk = min(DSA_TOPK, L // 4)
    return pl.pallas_call(
        functools.partial(_dsa_kernel, topk=topk),
        grid=(B, L // TQ),
        in_specs=[
            _q_spec(T_AQ), _q_spec(T_AQI, 2 * GW), _q_spec(0, NS),
            _k_spec(L, R_AK), _k_spec(L, R_AKI, LANES), _vT_spec(nch, T_AV), _BIAS_SPEC,
        ],
        out_specs=_OUT_SPEC,
        out_shape=jax.ShapeDtypeStruct((B, L, GW), bf16),
        scratch_shapes=[pltpu.VMEM((nch, TK, TQ), i32), pltpu.VMEM((nch, TK, TQ), i16),
                        pltpu.VMEM((nch, TK, TQ), i16)] + _flash_scratch(HG),
        compiler_params=_cparams("parallel", "arbitrary"),
        name="dsa",
    )(ht, ht, hs, hr, hr, ht, bias)


def _moba_kernel(qT_ref, k_ref, vT_ref, bias_ref, o_ref, km_ref, sel_ref, m_ref, l_ref, acc_ref, *, nch, topk):
    qi = pl.program_id(1)
    nb = km_ref.shape[0] // HG

    @pl.when(qi == 0)
    def _():
        km_ref[...] = jnp.zeros(km_ref.shape, f32)
        lane = _col_iota((1, GW))
        for n in range(nch):
            blk = k_ref[n * MOBA_BLOCK:(n + 1) * MOBA_BLOCK, :].astype(f32)
            mean = jnp.sum(blk, axis=0, keepdims=True) * (1.0 / MOBA_BLOCK)
            for h in range(HG):
                own = (lane >= h * HEAD_DIM) & (lane < (h + 1) * HEAD_DIM)
                km_ref[h * nb + n:h * nb + n + 1, :] = jnp.where(own, mean, 0.0)

    km = km_ref[...]
    km_hi = km.astype(bf16)
    km_lo = (km - km_hi.astype(f32)).astype(bf16)
    qT = qT_ref[...]
    gates = (jnp.dot(km_hi, qT, preferred_element_type=f32)
             + jnp.dot(km_lo, qT, preferred_element_type=f32))
    blk_id = _row_iota((nb, 1))
    blk_f = blk_id.astype(f32)
    for h in range(HG):
        gate = jnp.where(blk_id < qi, gates[h * nb:(h + 1) * nb], -jnp.inf)
        sel = jnp.zeros((nb, TQ), f32)
        for _ in range(topk):
            best = jnp.max(gate, axis=0, keepdims=True)
            first = jnp.min(jnp.where(gate == best, blk_f, float(nb)), axis=0, keepdims=True)
            hit = blk_f == first
            sel = jnp.where(hit & (best > -jnp.inf), 1.0, sel)
            gate = jnp.where(hit, -jnp.inf, gate)
        sel_ref[h] = sel

    qs = [_pair_rows(qT_ref, h) for h in range(HG)]
    causal = _row_iota((TK, 1)) <= _col_iota((1, TQ))

    def finish(kind, h, s, kc):
        s = s * (HEAD_DIM ** -0.5 * LOG2E)
        if kind < 2:
            s = s + bias_ref[h, kind]
        if kind == 0:
            return jnp.where(causal, s, NEG)
        return s, (1.0 - sel_ref[h, pl.ds(kc, 1), :]) * NEG

    _attend(qi, HG, lambda kc, h: jnp.dot(_k_chunk(k_ref, kc, h // 2), qs[h], preferred_element_type=f32),
            finish, lambda kc, h: vT_ref[kc, h * HEAD_DIM:(h + 1) * HEAD_DIM, :],
            m_ref, l_ref, acc_ref, shared=lambda kc, kind: kc)
    outT = jnp.concatenate([_flash_result(h, l_ref, acc_ref) for h in range(HG)], axis=0)
    o_ref[...] = outT.T.astype(o_ref.dtype)


def _moba(hr, ht, bias):
    B, L, _ = hr.shape
    nch = L // TK
    nb = -(-nch // 8) * 8
    topk = min(MOBA_TOPK, nch - 1)
    return pl.pallas_call(
        functools.partial(_moba_kernel, nch=nch, topk=topk),
        grid=(B, L // TQ),
        in_specs=[_q_spec(T_CQ), _k_spec(L, R_CK), _vT_spec(nch, T_CV), _BIAS_SPEC],
        out_specs=_OUT_SPEC,
        out_shape=jax.ShapeDtypeStruct((B, L, GW), bf16),
        scratch_shapes=[pltpu.VMEM((HG * nb, GW), f32), pltpu.VMEM((HG, nb, TQ), f32)] + _flash_scratch(HG),
        compiler_params=_cparams("parallel", "arbitrary"),
        name="moba",
    )(ht, hr, ht, bias)


def _diff_kernel(qT_ref, k_ref, vT_ref, bias_ref, lq1_ref, lk1_ref, lq2_ref, lk2_ref, li_ref, g_ref,
                 o_ref, m_ref, l_ref, acc_ref):
    qi = pl.program_id(1)
    lam_init = li_ref[...]
    lam = (jnp.exp(jnp.sum(lq1_ref[...] * lk1_ref[...], axis=1, keepdims=True))
           - jnp.exp(jnp.sum(lq2_ref[...] * lk2_ref[...], axis=1, keepdims=True)) + lam_init)
    qs = [_pair_rows(qT_ref, h, DIFF_HALF, c * DIFF_HALF) for h in range(HG) for c in range(2)]
    causal = _row_iota((TK, 1)) <= _col_iota((1, TQ))

    def finish(kind, c, s, _):
        s = s * (DIFF_HALF ** -0.5 * LOG2E)
        if kind < 2:
            s = s + bias_ref[c // 2, kind]
        return jnp.where(causal, s, NEG) if kind == 0 else s

    _attend(qi, 2 * HG, lambda kc, c: jnp.dot(_k_chunk(k_ref, kc, c // 4), qs[c], preferred_element_type=f32),
            finish, lambda kc, c: vT_ref[kc, (c // 2) * HEAD_DIM:(c // 2 + 1) * HEAD_DIM, :],
            m_ref, l_ref, acc_ref)
    heads = []
    for h in range(HG):
        o = _flash_result(2 * h, l_ref, acc_ref) - lam * _flash_result(2 * h + 1, l_ref, acc_ref)
        ms = jnp.mean(o * o, axis=0, keepdims=True)
        heads.append(o * lax.rsqrt(ms + 1e-6) * g_ref[...] * (1.0 - lam_init))
    o_ref[...] = jnp.concatenate(heads, axis=0).T.astype(o_ref.dtype)


def _diff(hr, ht, bias, lq1, lk1, lq2, lk2, lam_init, g_col):
    B, L, _ = hr.shape
    nch = L // TK
    vec = pl.BlockSpec((1, DIFF_HALF), lambda b, i: (0, 0))
    return pl.pallas_call(
        _diff_kernel,
        grid=(B, L // TQ),
        in_specs=[
            _q_spec(T_DQ), _k_spec(L, R_DK), _vT_spec(nch, T_DV), _BIAS_SPEC,
            vec, vec, vec, vec,
            pl.BlockSpec((1, 1), lambda b, i: (0, 0)),
            pl.BlockSpec((HEAD_DIM, 1), lambda b, i: (0, 0)),
        ],
        out_specs=_OUT_SPEC,
        out_shape=jax.ShapeDtypeStruct((B, L, GW), bf16),
        scratch_shapes=_flash_scratch(2 * HG),
        compiler_params=_cparams("parallel", "arbitrary"),
        name="diff",
    )(ht, hr, ht, bias, lq1, lk1, lq2, lk2, lam_init, g_col)


def _gelu_tanh(x):
    return 0.5 * x * (1.0 + jnp.tanh(math.sqrt(2.0 / math.pi) * (x + 0.044715 * (x * x * x))))


def _compress_kernel(xk_ref, xv_ref, pk_ref, pv_ref, w1k_ref, w1v_ref, w2k_ref, w2vT_ref, kc_ref, vcT_ref):
    def pre_act(x_ref, p_ref, w1_ref):
        x = x_ref[...].astype(f32)
        first = jnp.dot((x + p_ref[0:1, :]).astype(bf16), w1_ref[0], preferred_element_type=f32)
        second = jnp.dot((x + p_ref[1:2, :]).astype(bf16), w1_ref[1], preferred_element_type=f32)
        n = first.shape[0]
        return _gelu_tanh(first + pltpu.roll(second, n - 1, 0)).astype(bf16)

    gk = pre_act(xk_ref, pk_ref, w1k_ref)
    kc_ref[...] = jnp.dot(gk, w2k_ref[...], preferred_element_type=f32).astype(bf16)
    gv = pre_act(xv_ref, pv_ref, w1v_ref)
    vcT_ref[...] = lax.dot_general(w2vT_ref[...], gv, (((1,), (1,)), ((), ())),
                                   preferred_element_type=f32).astype(bf16)


def _compress(xk, xv, pk, pv, w1k, w1v, w2k, w2vT):
    B, n, W = xk.shape
    xspec = pl.BlockSpec((None, n, W), lambda b: (b, 0, 0))
    full = lambda a: pl.BlockSpec(a.shape, lambda b: (0,) * a.ndim)
    return pl.pallas_call(
        _compress_kernel,
        grid=(B,),
        in_specs=[xspec, xspec, full(pk), full(pv), full(w1k), full(w1v), full(w2k), full(w2vT)],
        out_specs=[
            pl.BlockSpec((None, n, LANES), lambda b: (b, 0, 0)),
            pl.BlockSpec((None, HEAD_DIM, n), lambda b: (b, 0, 0)),
        ],
        out_shape=[
            jax.ShapeDtypeStruct((B, n, LANES), bf16),
            jax.ShapeDtypeStruct((B, HEAD_DIM, n), bf16),
        ],
        compiler_params=_cparams("parallel"),
        name="nsa_compress",
    )(xk, xv, pk, pv, w1k, w1v, w2k, w2vT)


def _split3(x):
    hi = x.astype(bf16)
    r = x - hi.astype(f32)
    mid = r.astype(bf16)
    lo = (r - mid.astype(f32)).astype(bf16)
    return hi, mid, lo


def _nsa_kernel(qT_ref, gT_ref, kc_ref, vcT_ref, ks_ref, vsT_ref, kw_ref, vwT_ref, ovlT_ref, exp_ref,
                bias_ref, o_ref, m_ref, l_ref, acc_ref, *, n_slc, topn):
    qi = pl.program_id(1)
    ncmp = kc_ref.shape[0]
    qpos = qi * TQ + _col_iota((1, TQ))
    pad = jnp.zeros((LANES - HEAD_DIM, TQ), bf16)
    qs = [jnp.concatenate([qT_ref[h * HEAD_DIM:(h + 1) * HEAD_DIM, :], pad], axis=0)
          for h in range(HG)]

    cmp_ok = _row_iota((ncmp, 1)) * CMP_STRIDE + (CMP_LEN - 1) <= qpos
    o_cmp = []
    pc_sum = jnp.zeros((ncmp, TQ), f32)
    for h in range(HG):
        s = jnp.dot(kc_ref[...], qs[h], preferred_element_type=f32) * (HEAD_DIM ** -0.5 * LOG2E)
        s = jnp.where(cmp_ok, s, NEG)
        e = jnp.exp2(s - jnp.maximum(jnp.max(s, axis=0, keepdims=True), FLOOR))
        den = jnp.sum(e, axis=0, keepdims=True)
        pc = e / jnp.where(den > 0.0, den, 1.0)
        pc_sum = pc_sum + pc
        o_cmp.append(jnp.dot(vcT_ref[...], pc.astype(bf16), preferred_element_type=f32))
    nb = ovlT_ref.shape[0]
    imp = jnp.zeros((nb, TQ), f32)
    for part in _split3(pc_sum):
        imp = imp + jnp.dot(ovlT_ref[...], part, preferred_element_type=f32)
    blk = _row_iota((nb, 1))
    blk_f = blk.astype(f32)
    cur = qpos // SLC_LEN
    forced = (blk == 0) | (blk == cur) | (blk == cur - 1)
    imp = jnp.where(forced, jnp.inf, imp)
    imp = jnp.where((blk * SLC_LEN <= qpos) & (blk < n_slc), imp, -jnp.inf)

    def pick(_, st):
        imp, sel = st
        best = jnp.max(imp, axis=0, keepdims=True)
        first = jnp.min(jnp.where(imp == best, blk_f, float(nb)), axis=0, keepdims=True)
        hit = blk_f == first
        return jnp.where(hit, -jnp.inf, imp), jnp.where(hit & (best > -jnp.inf), 1.0, sel)

    _, sel = lax.fori_loop(0, topn, pick, (imp, jnp.zeros((nb, TQ), f32)))

    krow = _row_iota((TK, 1))
    qcol = _col_iota((1, TQ))
    causal = krow <= qcol

    selb = sel.astype(bf16)

    def slc_keep(kc, kind):
        keep = jnp.dot(exp_ref[kc], selb, preferred_element_type=f32) > 0.5
        return keep & causal if kind == 0 else keep

    def slc_finish(kind, h, s, keep):
        s = s * (HEAD_DIM ** -0.5 * LOG2E)
        if kind < 2:
            s = s + bias_ref[h, kind]
        return jnp.where(keep, s, NEG)

    _attend(qi, HG, lambda kc, h: jnp.dot(_k_chunk(ks_ref, kc), qs[h], preferred_element_type=f32),
            slc_finish, lambda kc, h: vsT_ref[kc], m_ref, l_ref, acc_ref, shared=slc_keep)
    o_slc = [_flash_result(h, l_ref, acc_ref) for h in range(HG)]

    def win_finish(kind, h, s, _):
        s = s * (HEAD_DIM ** -0.5 * LOG2E)
        if kind < 2:
            s = s + bias_ref[h, kind]
        if kind == 0:
            return jnp.where(causal, s, NEG)
        return jnp.where(krow > qcol, s, NEG) if kind == 2 else s

    _attend(qi, HG, lambda kc, h: jnp.dot(_k_chunk(kw_ref, kc), qs[h], preferred_element_type=f32),
            win_finish, lambda kc, h: vwT_ref[kc], m_ref, l_ref, acc_ref,
            first=jnp.maximum(qi - WIN // TK, 0), max_far=WIN // TK - 1)
    o_win = [_flash_result(h, l_ref, acc_ref) for h in range(HG)]

    gates = jax.nn.sigmoid(gT_ref[...])
    heads = []
    for h in range(HG):
        c = S_BG + 3 * h
        heads.append(gates[c:c + 1, :] * o_cmp[h] + gates[c + 1:c + 2, :] * o_slc[h]
                     + gates[c + 2:c + 3, :] * o_win[h])
    o_ref[...] = jnp.concatenate(heads, axis=0).T.astype(o_ref.dtype)


def _nsa(hr, ht, hs, kc, vcT, ovlT, expand, bias):
    B, L, _ = hr.shape
    nch = L // TK
    n = kc.shape[1]
    n_slc = L // SLC_LEN
    topn = min(SLC_TOPN, n_slc)
    return pl.pallas_call(
        functools.partial(_nsa_kernel, n_slc=n_slc, topn=topn),
        grid=(B, L // TQ),
        in_specs=[
            _q_spec(T_BQ), _q_spec(0, NS),
            pl.BlockSpec((None, n, LANES), lambda b, i: (b, 0, 0)),
            pl.BlockSpec((None, HEAD_DIM, n), lambda b, i: (b, 0, 0)),
            _k_spec(L, R_BKS, LANES), _vT_spec(nch, T_BVS, HEAD_DIM),
            _k_spec(L, R_BKW, LANES), _vT_spec(nch, T_BVW, HEAD_DIM),
            pl.BlockSpec(ovlT.shape, lambda b, i: (0, 0)),
            pl.BlockSpec(expand.shape, lambda b, i: (0, 0, 0)),
            _BIAS_SPEC,
        ],
        out_specs=_OUT_SPEC,
        out_shape=jax.ShapeDtypeStruct((B, L, GW), bf16),
        scratch_shapes=_flash_scratch(HG),
        compiler_params=_cparams("parallel", "arbitrary"),
        name="nsa",
    )(ht, hs, kc, vcT, hr, ht, hr, ht, ovlT, expand, bias)


def _nsa_tables(L):
    n = L // CMP_STRIDE
    n_slc = L // SLC_LEN
    nb = -(-n_slc // 16) * 16
    c0 = jnp.arange(n, dtype=i32)[None, :] * CMP_STRIDE
    s0 = jnp.arange(nb, dtype=i32)[:, None] * SLC_LEN
    ovlT = (c0 <= s0 + SLC_LEN - 1) & (c0 + CMP_LEN - 1 >= s0)
    ovlT = ovlT & (jnp.arange(n)[None, :] < n - 1) & (jnp.arange(nb)[:, None] < n_slc)
    tokblk = jnp.arange(L, dtype=i32) // SLC_LEN
    expand = tokblk[:, None] == jnp.arange(nb, dtype=i32)[None, :]
    return ovlT.astype(bf16), expand.reshape(L // TK, TK, nb).astype(bf16)


def _memkv_kernel(mem_ref, wkT_ref, wv_ref, kT_ref, v_ref):
    mb = mem_ref[...].astype(bf16)
    kT_ref[...] = lax.dot_general(wkT_ref[...], mb, (((1,), (1,)), ((), ())),
                                  preferred_element_type=f32).astype(bf16)
    v_ref[...] = jnp.dot(mb, wv_ref[...], preferred_element_type=f32).astype(bf16)


def _memkv(mem, wkT, wv):
    B, N, D = mem.shape
    wspec = pl.BlockSpec((D, D), lambda b: (0, 0))
    return pl.pallas_call(
        _memkv_kernel,
        grid=(B,),
        in_specs=[pl.BlockSpec((None, N, D), lambda b: (b, 0, 0)), wspec, wspec],
        out_specs=[pl.BlockSpec((None, D, N), lambda b: (b, 0, 0)),
                   pl.BlockSpec((None, N, D), lambda b: (b, 0, 0))],
        out_shape=[jax.ShapeDtypeStruct((B, D, N), bf16), jax.ShapeDtypeStruct((B, N, D), bf16)],
        compiler_params=_cparams("parallel"),
        name="cross_kv",
    )(mem, wkT, wv)


def _mix_cross_kernel(oa_ref, ob_ref, oc_ref, od_ref, w_ref, x_ref, g1_ref, b1_ref,
                      wq_ref, kT_ref, v_ref, wo_ref, g_ref, b_ref, y_ref):
    mix = jnp.zeros(x_ref.shape, f32)
    for n, o_ref in enumerate((oa_ref, ob_ref, oc_ref, od_ref)):
        mix = mix + jnp.dot(o_ref[...], w_ref[n * GW:(n + 1) * GW, :], preferred_element_type=f32)
    x = _layer_norm(ALPHA * x_ref[...] + mix, g1_ref[...], b1_ref[...])
    q = jnp.dot(x.astype(bf16), wq_ref[...], preferred_element_type=f32).astype(bf16)
    outs = []
    for h in range(CROSS_HEADS):
        sl = slice(h * CROSS_DIM, (h + 1) * CROSS_DIM)
        s = jnp.dot(q[:, sl], kT_ref[sl, :], preferred_element_type=f32) * CROSS_DIM ** -0.5
        e = jnp.exp(s - jnp.max(s, axis=1, keepdims=True))
        p = e / jnp.sum(e, axis=1, keepdims=True)
        outs.append(jnp.dot(p.astype(bf16), v_ref[:, sl], preferred_element_type=f32).astype(bf16))
    o = jnp.concatenate(outs, axis=1)
    y = ALPHA * x + jnp.dot(o, wo_ref[...], preferred_element_type=f32)
    y_ref[...] = _layer_norm(y, g_ref[...], b_ref[...])


def _mix_cross(oa, ob, oc, od, w, x, g1, b1, wq, kT, v, wo, g, b, tm=512):
    B, L, D = x.shape
    N = v.shape[1]
    ospec = pl.BlockSpec((None, tm, GW), lambda bb, i: (bb, i, 0))
    xspec = pl.BlockSpec((None, tm, D), lambda bb, i: (bb, i, 0))
    wspec = pl.BlockSpec((D, D), lambda bb, i: (0, 0))
    vec = pl.BlockSpec((1, D), lambda bb, i: (0, 0))
    return pl.pallas_call(
        _mix_cross_kernel,
        grid=(B, L // tm),
        in_specs=[ospec, ospec, ospec, ospec, wspec, xspec, vec, vec, wspec,
                  pl.BlockSpec((None, D, N), lambda bb, i: (bb, 0, 0)),
                  pl.BlockSpec((None, N, D), lambda bb, i: (bb, 0, 0)), wspec, vec, vec],
        out_specs=xspec,
        out_shape=jax.ShapeDtypeStruct((B, L, D), f32),
        compiler_params=_cparams("parallel", "parallel"),
        name="mix_cross_ln",
    )(oa, ob, oc, od, w, x, g1, b1, wq, kT, v, wo, g, b)


def _mlp_kernel(x_ref, w1_ref, w2_ref, g_ref, b_ref, y_ref, acc_ref):
    j = pl.program_id(1)

    @pl.when(j == 0)
    def _():
        acc_ref[...] = jnp.zeros(acc_ref.shape, f32)

    hdn = jnp.dot(x_ref[...].astype(bf16), w1_ref[...], preferred_element_type=f32)
    hdn = jnp.square(jnp.maximum(hdn, 0.0)).astype(bf16)
    acc_ref[...] += jnp.dot(hdn, w2_ref[...], preferred_element_type=f32)

    @pl.when(j == pl.num_programs(1) - 1)
    def _():
        y_ref[...] = _layer_norm(ALPHA * x_ref[...] + acc_ref[...], g_ref[...], b_ref[...])


def _mlp(x, w1, w2, g, b, tm=1024, tf=1024):
    M, D = x.shape
    F = w1.shape[1]
    vec = pl.BlockSpec((1, D), lambda i, j: (0, 0))
    return pl.pallas_call(
        _mlp_kernel,
        grid=(M // tm, F // tf),
        in_specs=[pl.BlockSpec((tm, D), lambda i, j: (i, 0)),
                  pl.BlockSpec((D, tf), lambda i, j: (0, j)),
                  pl.BlockSpec((tf, D), lambda i, j: (j, 0)), vec, vec],
        out_specs=pl.BlockSpec((tm, D), lambda i, j: (i, 0)),
        out_shape=jax.ShapeDtypeStruct((M, D), f32),
        scratch_shapes=[pltpu.VMEM((tm, D), f32)],
        compiler_params=_cparams("parallel", "arbitrary"),
        name="mlp_ln",
    )(x, w1, w2, g, b)


def _split_w_in(w):
    sizes = (GW, GW, GW, IDX_HEADS * IDX_DIM, IDX_DIM, IDX_HEADS,
             GW, HEAD_DIM, HEAD_DIM, HEAD_DIM, HEAD_DIM, HEAD_DIM, HEAD_DIM, 3 * HG,
             GW, GW, GW, GW, GW, GW)
    offs = [0]
    for s in sizes:
        offs.append(offs[-1] + s)
    return [w[:, offs[n]:offs[n + 1]] for n in range(len(sizes))]


def _layout_w_in(w):
    (a_q, a_k, a_v, a_qi, a_ki, a_w, b_q, b_kc, b_vc, b_ks, b_vs, b_kw, b_vw, b_g,
     c_q, c_k, c_v, d_q, d_k, d_v) = _split_w_in(w)
    twice = lambda t: jnp.concatenate([t, t], axis=1)
    wr = jnp.concatenate([a_k, c_k, d_k, twice(b_ks), twice(b_kw), twice(a_ki), b_kc, b_vc], axis=1)
    wt = jnp.concatenate([a_q, a_v, a_qi, b_q, c_q, c_v, d_q, d_v, b_vs, b_vw], axis=1).T
    ws = jnp.concatenate([a_w, b_g, jnp.zeros((w.shape[0], NS - IDX_HEADS - 3 * HG), w.dtype)], axis=1).T
    assert wr.shape[1] == NR and wt.shape[0] == NT
    return wr.astype(bf16), wt.astype(bf16), ws.astype(bf16)


def _mixers(x, bias, w_in_l, pos_k, pos_v, w1_k, w2_k, w1_v, w2_v, lq1, lk1, lq2, lk2, diff_g, lam_init):
    B, L, _ = x.shape
    wr, wt, ws = _layout_w_in(w_in_l)
    hr, ht, hs = _project(x, wr, wt, ws)
    o_a = _dsa(hr, ht, hs, bias[0:HG])

    n = L // CMP_STRIDE
    half = CMP_STRIDE * HEAD_DIM
    xk = hr[:, :, R_BKVC:R_BKVC + HEAD_DIM].reshape(B, n, half)
    xv = hr[:, :, R_BKVC + HEAD_DIM:R_BKVC + 2 * HEAD_DIM].reshape(B, n, half)
    kc, vcT = _compress(
        xk, xv, pos_k.reshape(2, half), pos_v.reshape(2, half),
        w1_k.reshape(2, half, HEAD_DIM).astype(bf16), w1_v.reshape(2, half, HEAD_DIM).astype(bf16),
        jnp.concatenate([w2_k, w2_k], axis=1).astype(bf16), w2_v.T.astype(bf16))
    ovlT, expand = _nsa_tables(L)
    o_b = _nsa(hr, ht, hs, kc, vcT, ovlT, expand, bias[HG:2 * HG])
    o_c = _moba(hr, ht, bias[2 * HG:3 * HG])
    o_d = _diff(hr, ht, bias[3 * HG:4 * HG], lq1.reshape(1, -1), lk1.reshape(1, -1),
                lq2.reshape(1, -1), lk2.reshape(1, -1), jnp.full((1, 1), lam_init, f32),
                diff_g.reshape(-1, 1))
    return o_a, o_b, o_c, o_d


def kernel(x, mem, rel_bias, w_in, w_out, nsa_pos_k, nsa_pos_v, nsa_w1_k, nsa_w2_k, nsa_w1_v, nsa_w2_v, diff_lq1, diff_lk1, diff_lq2, diff_lk2, diff_g, ln1_g, ln1_b, xq, xk, xv, xo, ln2_g, ln2_b, mlp_w1, mlp_w2, ln3_g, ln3_b):
    B, L, D = x.shape
    bias = _bias_tiles(rel_bias)
    for l in range(DEPTH):
        lam_init = 0.8 - 0.6 * math.exp(-0.3 * l)
        o_a, o_b, o_c, o_d = _mixers(
            x, bias, w_in[l], nsa_pos_k[l], nsa_pos_v[l], nsa_w1_k[l], nsa_w2_k[l],
            nsa_w1_v[l], nsa_w2_v[l], diff_lq1[l], diff_lk1[l], diff_lq2[l], diff_lk2[l],
            diff_g[l], lam_init)
        flat = lambda t: t.reshape(B * L, -1)
        row = lambda t: t.reshape(1, D)
        kT, v = _memkv(mem, xk[l].T.astype(bf16), xv[l].astype(bf16))
        x3 = _mix_cross(o_a, o_b, o_c, o_d, w_out[l].astype(bf16), x, row(ln1_g[l]), row(ln1_b[l]),
                        xq[l].astype(bf16), kT, v, xo[l].astype(bf16), row(ln2_g[l]), row(ln2_b[l]))
        x = _mlp(flat(x3), mlp_w1[l].astype(bf16), mlp_w2[l].astype(bf16),
                 row(ln3_g[l]), row(ln3_b[l])).reshape(B, L, D)
    return x
```

```python
import functools
import math

import jax
import jax.numpy as jnp
from jax import lax
from jax.experimental import pallas as pl
from jax.experimental.pallas import tpu as pltpu

f32 = jnp.float32
bf16 = jnp.bfloat16
i32 = jnp.int32
i16 = jnp.int16

D_MODEL = 1024
DEPTH = 4
HEAD_DIM = 64
HG = 4
GW = HG * HEAD_DIM
REL_BUCKETS = 32
REL_MAX_DIST = 128
IDX_HEADS = 8
IDX_DIM = 64
DSA_TOPK = 256
CMP_LEN = 32
CMP_STRIDE = 16
SLC_LEN = 64
SLC_TOPN = 16
WIN = 512
MOBA_BLOCK = 256
MOBA_TOPK = 3
DIFF_HALF = HEAD_DIM // 2
CROSS_HEADS = 4
CROSS_DIM = D_MODEL // CROSS_HEADS
D_FF = 4 * D_MODEL
ALPHA = (2 * DEPTH) ** 0.25

NEG = -1e30
FLOOR = -1e29
INT_MIN = -(2 ** 31)
HALF = 2 ** 15
LANES = 128
TQ = 256
TK = 256
VMEM_LIMIT = 56 * 1024 * 1024
LOG2E = math.log2(math.e)
FAR_GROUP = 4
MATMULS_AHEAD = 6

R_AK, R_CK, R_DK, R_BKS, R_BKW, R_AKI, R_BKVC = 0, 256, 512, 768, 896, 1024, 1152
NR = 1280
T_AQ, T_AV, T_AQI, T_BQ, T_CQ, T_CV, T_DQ, T_DV, T_BVS, T_BVW = (
    0, 256, 512, 1024, 1280, 1536, 1792, 2048, 2304, 2368)
NT = 2432
S_AW, S_BG, NS = 0, 8, 32


def _cparams(*sem):
    return pltpu.CompilerParams(dimension_semantics=sem, vmem_limit_bytes=VMEM_LIMIT)


def _rel_bucket(dist):
    n = jnp.maximum(dist, 0)
    max_exact = REL_BUCKETS // 2
    nf = jnp.maximum(n, max_exact).astype(f32)
    large = max_exact + (jnp.log(nf / max_exact) / math.log(REL_MAX_DIST / max_exact)
                         * (REL_BUCKETS - max_exact)).astype(i32)
    large = jnp.minimum(large, REL_BUCKETS - 1)
    return jnp.where(n < max_exact, n, large)


def _bias_tiles(rel_bias):
    assert 2 * TQ - TK + 1 >= REL_MAX_DIST
    d = jnp.arange(TQ, dtype=i32)[None, :] - jnp.arange(TK, dtype=i32)[:, None]
    bucket = jnp.stack([_rel_bucket(d + off) for off in (0, TQ)])
    far = rel_bias[_rel_bucket(jnp.int32(2 * TQ))]
    table = ((rel_bias - far) * LOG2E).astype(f32)
    onehot = (bucket[None] == jnp.arange(REL_BUCKETS, dtype=i32)[:, None, None, None]).astype(f32)
    return jnp.einsum('bntq,bh->hntq', onehot, table, precision=lax.Precision.HIGHEST)


def _row_iota(shape):
    return lax.broadcasted_iota(i32, shape, 0)


def _col_iota(shape):
    return lax.broadcasted_iota(i32, shape, 1)


def _flash_init(m_ref, l_ref, acc_ref):
    m_ref[...] = jnp.full(m_ref.shape, NEG, f32)
    l_ref[...] = jnp.zeros(l_ref.shape, f32)
    acc_ref[...] = jnp.zeros(acc_ref.shape, f32)


def _fold(x, op, rows):
    while x.shape[0] > rows:
        half = x.shape[0] // 2
        x = op(x[:half], x[half:])
    return x


def _flash_step(c, s, vT, m_ref, l_ref, acc_ref, pen=None):
    m_prev = m_ref[c]
    m_chunk = jnp.max(_fold(s, jnp.maximum, 8), axis=0, keepdims=True)
    if pen is not None:
        m_chunk = m_chunk + pen
    m_new = jnp.maximum(m_prev, m_chunk)
    m_use = jnp.maximum(m_new, FLOOR)
    alpha = jnp.exp2(jnp.maximum(m_prev, FLOOR) - m_use)
    p = jnp.exp2(s - (m_use if pen is None else m_use - pen))
    l_ref[c] = alpha * l_ref[c] + jnp.sum(_fold(p, jnp.add, 8), axis=0, keepdims=True)
    acc_ref[c] = alpha * acc_ref[c] + jnp.dot(vT, p.astype(bf16), preferred_element_type=f32)
    m_ref[c] = m_new


def _flash_result(c, l_ref, acc_ref):
    l = l_ref[c]
    return jnp.where(l > 0.0, acc_ref[c] / jnp.where(l > 0.0, l, 1.0), 0.0)


def _attend(qi, n, raw, finish, vT, m_ref, l_ref, acc_ref, first=0, shared=None, max_far=None):
    def run(chunks):
        items = [(kc, kind, c) for kc, kind in chunks for c in range(n)]
        ctx, logits = {}, {}

        def issue(j):
            kc, kind, c = items[j]
            if c == 0 and shared is not None:
                ctx[j // n] = shared(kc, kind)
            logits[j] = raw(kc, c)

        for j in range(min(MATMULS_AHEAD, len(items))):
            issue(j)
        for j, (kc, kind, c) in enumerate(items):
            if j + MATMULS_AHEAD < len(items):
                issue(j + MATMULS_AHEAD)
            s = finish(kind, c, logits.pop(j), ctx.get(j // n))
            s, pen = s if isinstance(s, tuple) else (s, None)
            _flash_step(c, s, vT(kc, c), m_ref, l_ref, acc_ref, pen)

    _flash_init(m_ref, l_ref, acc_ref)
    n_far = jnp.maximum(qi - 1 - first, 0)
    group = FAR_GROUP
    if max_far is not None and max_far < group:
        group = max_far + 1

    def far_group(j, carry):
        run([(first + group * j + t, 2) for t in range(group)])
        return carry

    lax.fori_loop(0, n_far // group, far_group, 0)
    done = first + (n_far // group) * group
    for rem in range(group):
        @pl.when((qi >= 1) & (n_far % group == rem))
        def _(rem=rem):
            run([(done + t, 2) for t in range(rem)] + [(qi - 1, 1), (qi, 0)])

    @pl.when(qi == 0)
    def _():
        run([(qi, 0)])


def _pair_rows(qT_ref, h, width=HEAD_DIM, offset=0):
    blk = qT_ref[(h // 2) * LANES:(h // 2 + 1) * LANES, :]
    r = _row_iota((LANES, 1))
    lo = (h % 2) * HEAD_DIM + offset
    return jnp.where((r >= lo) & (r < lo + width), blk, jnp.zeros_like(blk))


def _layer_norm(y, g, b):
    mu = jnp.mean(y, axis=-1, keepdims=True)
    yc = y - mu
    var = jnp.mean(yc * yc, axis=-1, keepdims=True)
    return yc * lax.rsqrt(var + 1e-5) * g + b


def _proj_kernel(x_ref, wr_ref, wt_ref, ws_ref, hr_ref, ht_ref, hs_ref):
    xb = x_ref[...].astype(bf16)
    nt = (((1,), (1,)), ((), ()))
    hr_ref[...] = jnp.dot(xb, wr_ref[...], preferred_element_type=f32).astype(bf16)
    ht_ref[...] = lax.dot_general(wt_ref[...], xb, nt, preferred_element_type=f32).astype(bf16)
    hs_ref[...] = lax.dot_general(ws_ref[...], xb, nt, preferred_element_type=f32)


def _project(x, wr, wt, ws):
    B, L, D = x.shape
    nch = L // TK
    return pl.pallas_call(
        _proj_kernel,
        grid=(B, nch),
        in_specs=[
            pl.BlockSpec((None, TK, D), lambda b, i: (b, i, 0)),
            pl.BlockSpec((D, NR), lambda b, i: (0, 0)),
            pl.BlockSpec((NT, D), lambda b, i: (0, 0)),
            pl.BlockSpec((NS, D), lambda b, i: (0, 0)),
        ],
        out_specs=[
            pl.BlockSpec((None, TK, NR), lambda b, i: (b, i, 0)),
            pl.BlockSpec((None, None, NT, TK), lambda b, i: (b, i, 0, 0)),
            pl.BlockSpec((None, None, NS, TK), lambda b, i: (b, i, 0, 0)),
        ],
        out_shape=[
            jax.ShapeDtypeStruct((B, L, NR), bf16),
            jax.ShapeDtypeStruct((B, nch, NT, TK), bf16),
            jax.ShapeDtypeStruct((B, nch, NS, TK), f32),
        ],
        compiler_params=_cparams("parallel", "parallel"),
        name="proj",
    )(x, wr, wt, ws)


def _q_spec(off, rows=GW):
    return pl.BlockSpec((None, None, rows, TQ), lambda b, i: (b, i, off // rows, 0))


def _vT_spec(nch, off, rows=GW):
    return pl.BlockSpec((None, nch, rows, TK), lambda b, i: (b, 0, off // rows, 0))


def _k_spec(L, off, cols=GW):
    return pl.BlockSpec((None, L, cols), lambda b, i: (b, 0, off // cols))


_BIAS_SPEC = pl.BlockSpec((HG, 2, TK, TQ), lambda b, i: (0, 0, 0, 0))
_OUT_SPEC = pl.BlockSpec((None, TQ, GW), lambda b, i: (b, i, 0))


def _flash_scratch(chains, dv=HEAD_DIM):
    return [pltpu.VMEM((chains, 1, TQ), f32), pltpu.VMEM((chains, 1, TQ), f32),
            pltpu.VMEM((chains, dv, TQ), f32)]


def _k_chunk(k_ref, kc, pair=None):
    rows = pl.ds(pl.multiple_of(kc * TK, TK), TK)
    if pair is None:
        return k_ref[rows, :]
    return k_ref[rows, pair * LANES:(pair + 1) * LANES]


def _dsa_kernel(qT_ref, qiT_ref, wT_ref, k_ref, ki_ref, vT_ref, bias_ref, o_ref,
                key_ref, hi_ref, lo_ref, m_ref, l_ref, acc_ref, *, topk):
    qi = pl.program_id(1)
    nch = qi + 1
    qpos = qi * TQ + _col_iota((1, TQ))
    krow = _row_iota((TK, 1))
    wT = wT_ref[...] * (IDX_DIM ** -0.5 * IDX_HEADS ** -0.5)
    qidx = [_pair_rows(qiT_ref, h) for h in range(IDX_HEADS)]

    def score_chunk(kc):
        ki2 = _k_chunk(ki_ref, kc)
        s = jnp.zeros((TK, TQ), f32)
        for h in range(IDX_HEADS):
            r = jnp.dot(ki2, qidx[h], preferred_element_type=f32)
            s = s + wT[S_AW + h:S_AW + h + 1, :] * jnp.maximum(r, 0.0)
        s = jnp.where(s == 0.0, 0.0, s)
        bits = lax.bitcast_convert_type(s, i32)
        key = bits ^ ((bits >> 31) & 0x7FFFFFFF)
        key = jnp.where(kc * TK + krow <= qpos, key, INT_MIN)
        key_ref[kc] = key
        hi_ref[kc] = (key >> 16).astype(i16)
        lo_ref[kc] = ((key & 0xFFFF) - HALF).astype(i16)

    assert FAR_GROUP == 4

    def score_group(j, carry):
        for t in range(FAR_GROUP):
            score_chunk(FAR_GROUP * j + t)
        return carry

    lax.fori_loop(0, nch // FAR_GROUP, score_group, 0)

    @pl.when((nch // 2) % 2 == 1)
    def _():
        score_chunk(nch - nch % 2 - 2)
        score_chunk(nch - nch % 2 - 1)

    @pl.when(nch % 2 == 1)
    def _():
        score_chunk(nch - 1)
        key_ref[nch] = jnp.full((TK, TQ), INT_MIN, i32)
        hi_ref[nch] = jnp.full((TK, TQ), -HALF, i16)
        lo_ref[nch] = jnp.full((TK, TQ), -HALF, i16)

    npair = (nch + 1) // 2

    def count(ref, pred):
        one = jnp.ones((), ref.dtype)
        zero = jnp.zeros((), ref.dtype)
        rows = 8 * 4 // ref.dtype.itemsize

        def body(j, acc):
            hit = jnp.where(pred(ref[2 * j]), one, zero) + jnp.where(pred(ref[2 * j + 1]), one, zero)
            return acc + _fold(hit, jnp.add, rows).astype(f32)
        acc = lax.fori_loop(0, npair, body, jnp.zeros((rows, TQ), f32))
        return jnp.sum(acc, axis=0, keepdims=True)

    def kth_largest_i16(ref, want):
        def bisect(i, t_u):
            cand_u = t_u | jnp.left_shift(jnp.int32(1), 15 - i)
            cand = (cand_u - HALF).astype(i16)
            return jnp.where(count(ref, lambda k: k >= cand) >= want, cand_u, t_u)
        return lax.fori_loop(0, 16, bisect, jnp.zeros((1, TQ), i32)) - HALF

    hi_t = kth_largest_i16(hi_ref, topk)
    hi_t16 = hi_t.astype(i16)
    above = count(hi_ref, lambda k: k > hi_t16)

    def keep_low(kc, carry):
        lo_ref[kc] = jnp.where(hi_ref[kc] == hi_t16, lo_ref[kc], jnp.full((), -HALF, i16))
        return carry

    lax.fori_loop(0, 2 * npair, keep_low, 0)
    lo_t = kth_largest_i16(lo_ref, topk - above)
    thr = jnp.maximum((hi_t << 16) | (lo_t + HALF), INT_MIN + 1)
    cnt_ge = count(key_ref, lambda k: k >= thr)

    @pl.when(jnp.max(cnt_ge) > topk)
    def _():
        need = topk - (cnt_ge - count(key_ref, lambda k: k == thr))
        tri = (_col_iota((TK, TK)) <= _row_iota((TK, TK))).astype(bf16)

        def body(kc, seen):
            k = key_ref[kc]
            tie = k == thr
            tief = jnp.where(tie, 1.0, 0.0)
            pref = jnp.dot(tri, tief.astype(bf16), preferred_element_type=f32) + seen
            key_ref[kc] = jnp.where(tie & (pref > need), INT_MIN, k)
            return seen + jnp.sum(tief, axis=0, keepdims=True)

        lax.fori_loop(0, nch, body, jnp.zeros((1, TQ), f32))

    qs = [_pair_rows(qT_ref, h) for h in range(HG)]

    def finish(kind, h, s, keep):
        s = s * (HEAD_DIM ** -0.5 * LOG2E)
        if kind < 2:
            s = s + bias_ref[h, kind]
        return jnp.where(keep, s, NEG)

    _attend(qi, HG, lambda kc, h: jnp.dot(_k_chunk(k_ref, kc, h // 2), qs[h], preferred_element_type=f32),
            finish, lambda kc, h: vT_ref[kc, h * HEAD_DIM:(h + 1) * HEAD_DIM, :],
            m_ref, l_ref, acc_ref, shared=lambda kc, kind: key_ref[kc] >= thr)
    outT = jnp.concatenate([_flash_result(h, l_ref, acc_ref) for h in range(HG)], axis=0)
    o_ref[...] = outT.T.astype(o_ref.dtype)


def _dsa(hr, ht, hs, bias):
    B, L, _ = hr.shape
    nch = L // TK
    topk = min(DSA_TOPK, L // 4)
    return pl.pallas_call(
        functools.partial(_dsa_kernel, topk=topk),
        grid=(B, L // TQ),
        in_specs=[
            _q_spec(T_AQ), _q_spec(T_AQI, 2 * GW), _q_spec(0, NS),
            _k_spec(L, R_AK), _k_spec(L, R_AKI, LANES), _vT_spec(nch, T_AV), _BIAS_SPEC,
        ],
        out_specs=_OUT_SPEC,
        out_shape=jax.ShapeDtypeStruct((B, L, GW), bf16),
        scratch_shapes=[pltpu.VMEM((nch, TK, TQ), i32), pltpu.VMEM((nch, TK, TQ), i16),
                        pltpu.VMEM((nch, TK, TQ), i16)] + _flash_scratch(HG),
        compiler_params=_cparams("parallel", "arbitrary"),
        name="dsa",
    )(ht, ht, hs, hr, hr, ht, bias)


def _moba_kernel(qT_ref, k_ref, vT_ref, bias_ref, o_ref, km_ref, sel_ref, m_ref, l_ref, acc_ref, *, nch, topk):
    qi = pl.program_id(1)
    nb = km_ref.shape[0] // HG

    @pl.when(qi == 0)
    def _():
        km_ref[...] = jnp.zeros(km_ref.shape, f32)
        lane = _col_iota((1, GW))
        for n in range(nch):
            blk = k_ref[n * MOBA_BLOCK:(n + 1) * MOBA_BLOCK, :].astype(f32)
            mean = jnp.sum(blk, axis=0, keepdims=True) * (1.0 / MOBA_BLOCK)
            for h in range(HG):
                own = (lane >= h * HEAD_DIM) & (lane < (h + 1) * HEAD_DIM)
                km_ref[h * nb + n:h * nb + n + 1, :] = jnp.where(own, mean, 0.0)

    km = km_ref[...]
    km_hi = km.astype(bf16)
    km_lo = (km - km_hi.astype(f32)).astype(bf16)
    qT = qT_ref[...]
    gates = (jnp.dot(km_hi, qT, preferred_element_type=f32)
             + jnp.dot(km_lo, qT, preferred_element_type=f32))
    blk_id = _row_iota((nb, 1))
    blk_f = blk_id.astype(f32)
    for h in range(HG):
        gate = jnp.where(blk_id < qi, gates[h * nb:(h + 1) * nb], -jnp.inf)
        sel = jnp.zeros((nb, TQ), f32)
        for _ in range(topk):
            best = jnp.max(gate, axis=0, keepdims=True)
            first = jnp.min(jnp.where(gate == best, blk_f, float(nb)), axis=0, keepdims=True)
            hit = blk_f == first
            sel = jnp.where(hit & (best > -jnp.inf), 1.0, sel)
            gate = jnp.where(hit, -jnp.inf, gate)
        sel_ref[h] = sel

    qs = [_pair_rows(qT_ref, h) for h in range(HG)]
    causal = _row_iota((TK, 1)) <= _col_iota((1, TQ))

    def finish(kind, h, s, kc):
        s = s * (HEAD_DIM ** -0.5 * LOG2E)
        if kind < 2:
            s = s + bias_ref[h, kind]
        if kind == 0:
            return jnp.where(causal, s, NEG)
        return s, (1.0 - sel_ref[h, pl.ds(kc, 1), :]) * NEG

    _attend(qi, HG, lambda kc, h: jnp.dot(_k_chunk(k_ref, kc, h // 2), qs[h], preferred_element_type=f32),
            finish, lambda kc, h: vT_ref[kc, h * HEAD_DIM:(h + 1) * HEAD_DIM, :],
            m_ref, l_ref, acc_ref, shared=lambda kc, kind: kc)
    outT = jnp.concatenate([_flash_result(h, l_ref, acc_ref) for h in range(HG)], axis=0)
    o_ref[...] = outT.T.astype(o_ref.dtype)


def _moba(hr, ht, bias):
    B, L, _ = hr.shape
    nch = L // TK
    nb = -(-nch // 8) * 8
    topk = min(MOBA_TOPK, nch - 1)
    return pl.pallas_call(
        functools.partial(_moba_kernel, nch=nch, topk=topk),
        grid=(B, L // TQ),
        in_specs=[_q_spec(T_CQ), _k_spec(L, R_CK), _vT_spec(nch, T_CV), _BIAS_SPEC],
        out_specs=_OUT_SPEC,
        out_shape=jax.ShapeDtypeStruct((B, L, GW), bf16),
        scratch_shapes=[pltpu.VMEM((HG * nb, GW), f32), pltpu.VMEM((HG, nb, TQ), f32)] + _flash_scratch(HG),
        compiler_params=_cparams("parallel", "arbitrary"),
        name="moba",
    )(ht, hr, ht, bias)


def _diff_kernel(qT_ref, k_ref, vT_ref, bias_ref, lq1_ref, lk1_ref, lq2_ref, lk2_ref, li_ref, g_ref,
                 o_ref, m_ref, l_ref, acc_ref):
    qi = pl.program_id(1)
    lam_init = li_ref[...]
    lam = (jnp.exp(jnp.sum(lq1_ref[...] * lk1_ref[...], axis=1, keepdims=True))
           - jnp.exp(jnp.sum(lq2_ref[...] * lk2_ref[...], axis=1, keepdims=True)) + lam_init)
    qs = [_pair_rows(qT_ref, h, DIFF_HALF, c * DIFF_HALF) for h in range(HG) for c in range(2)]
    causal = _row_iota((TK, 1)) <= _col_iota((1, TQ))

    def finish(kind, c, s, _):
        s = s * (DIFF_HALF ** -0.5 * LOG2E)
        if kind < 2:
            s = s + bias_ref[c // 2, kind]
        return jnp.where(causal, s, NEG) if kind == 0 else s

    _attend(qi, 2 * HG, lambda kc, c: jnp.dot(_k_chunk(k_ref, kc, c // 4), qs[c], preferred_element_type=f32),
            finish, lambda kc, c: vT_ref[kc, (c // 2) * HEAD_DIM:(c // 2 + 1) * HEAD_DIM, :],
            m_ref, l_ref, acc_ref)
    heads = []
    for h in range(HG):
        o = _flash_result(2 * h, l_ref, acc_ref) - lam * _flash_result(2 * h + 1, l_ref, acc_ref)
        ms = jnp.mean(o * o, axis=0, keepdims=True)
        heads.append(o * lax.rsqrt(ms + 1e-6) * g_ref[...] * (1.0 - lam_init))
    o_ref[...] = jnp.concatenate(heads, axis=0).T.astype(o_ref.dtype)


def _diff(hr, ht, bias, lq1, lk1, lq2, lk2, lam_init, g_col):
    B, L, _ = hr.shape
    nch = L // TK
    vec = pl.BlockSpec((1, DIFF_HALF), lambda b, i: (0, 0))
    return pl.pallas_call(
        _diff_kernel,
        grid=(B, L // TQ),
        in_specs=[
            _q_spec(T_DQ), _k_spec(L, R_DK), _vT_spec(nch, T_DV), _BIAS_SPEC,
            vec, vec, vec, vec,
            pl.BlockSpec((1, 1), lambda b, i: (0, 0)),
            pl.BlockSpec((HEAD_DIM, 1), lambda b, i: (0, 0)),
        ],
        out_specs=_OUT_SPEC,
        out_shape=jax.ShapeDtypeStruct((B, L, GW), bf16),
        scratch_shapes=_flash_scratch(2 * HG),
        compiler_params=_cparams("parallel", "arbitrary"),
        name="diff",
    )(ht, hr, ht, bias, lq1, lk1, lq2, lk2, lam_init, g_col)


def _gelu_tanh(x):
    return 0.5 * x * (1.0 + jnp.tanh(math.sqrt(2.0 / math.pi) * (x + 0.044715 * (x * x * x))))


def _compress_kernel(xk_ref, xv_ref, pk_ref, pv_ref, w1k_ref, w1v_ref, w2k_ref, w2vT_ref, kc_ref, vcT_ref):
    def pre_act(x_ref, p_ref, w1_ref):
        x = x_ref[...].astype(f32)
        first = jnp.dot((x + p_ref[0:1, :]).astype(bf16), w1_ref[0], preferred_element_type=f32)
        second = jnp.dot((x + p_ref[1:2, :]).astype(bf16), w1_ref[1], preferred_element_type=f32)
        n = first.shape[0]
        return _gelu_tanh(first + pltpu.roll(second, n - 1, 0)).astype(bf16)

    gk = pre_act(xk_ref, pk_ref, w1k_ref)
    kc_ref[...] = jnp.dot(gk, w2k_ref[...], preferred_element_type=f32).astype(bf16)
    gv = pre_act(xv_ref, pv_ref, w1v_ref)
    vcT_ref[...] = lax.dot_general(w2vT_ref[...], gv, (((1,), (1,)), ((), ())),
                                   preferred_element_type=f32).astype(bf16)


def _compress(xk, xv, pk, pv, w1k, w1v, w2k, w2vT):
    B, n, W = xk.shape
    xspec = pl.BlockSpec((None, n, W), lambda b: (b, 0, 0))
    full = lambda a: pl.BlockSpec(a.shape, lambda b: (0,) * a.ndim)
    return pl.pallas_call(
        _compress_kernel,
        grid=(B,),
        in_specs=[xspec, xspec, full(pk), full(pv), full(w1k), full(w1v), full(w2k), full(w2vT)],
        out_specs=[
            pl.BlockSpec((None, n, LANES), lambda b: (b, 0, 0)),
            pl.BlockSpec((None, HEAD_DIM, n), lambda b: (b, 0, 0)),
        ],
        out_shape=[
            jax.ShapeDtypeStruct((B, n, LANES), bf16),
            jax.ShapeDtypeStruct((B, HEAD_DIM, n), bf16),
        ],
        compiler_params=_cparams("parallel"),
        name="nsa_compress",
    )(xk, xv, pk, pv, w1k, w1v, w2k, w2vT)


def _split3(x):
    hi = x.astype(bf16)
    r = x - hi.astype(f32)
    mid = r.astype(bf16)
    lo = (r - mid.astype(f32)).astype(bf16)
    return hi, mid, lo


def _nsa_kernel(qT_ref, gT_ref, kc_ref, vcT_ref, ks_ref, vsT_ref, kw_ref, vwT_ref, ovlT_ref, exp_ref,
                bias_ref, o_ref, m_ref, l_ref, acc_ref, *, n_slc, topn):
    qi = pl.program_id(1)
    ncmp = kc_ref.shape[0]
    qpos = qi * TQ + _col_iota((1, TQ))
    pad = jnp.zeros((LANES - HEAD_DIM, TQ), bf16)
    qs = [jnp.concatenate([qT_ref[h * HEAD_DIM:(h + 1) * HEAD_DIM, :], pad], axis=0)
          for h in range(HG)]

    cmp_ok = _row_iota((ncmp, 1)) * CMP_STRIDE + (CMP_LEN - 1) <= qpos
    o_cmp = []
    pc_sum = jnp.zeros((ncmp, TQ), f32)
    for h in range(HG):
        s = jnp.dot(kc_ref[...], qs[h], preferred_element_type=f32) * (HEAD_DIM ** -0.5 * LOG2E)
        s = jnp.where(cmp_ok, s, NEG)
        e = jnp.exp2(s - jnp.maximum(jnp.max(s, axis=0, keepdims=True), FLOOR))
        den = jnp.sum(e, axis=0, keepdims=True)
        pc = e / jnp.where(den > 0.0, den, 1.0)
        pc_sum = pc_sum + pc
        o_cmp.append(jnp.dot(vcT_ref[...], pc.astype(bf16), preferred_element_type=f32))
    nb = ovlT_ref.shape[0]
    imp = jnp.zeros((nb, TQ), f32)
    for part in _split3(pc_sum):
        imp = imp + jnp.dot(ovlT_ref[...], part, preferred_element_type=f32)
    blk = _row_iota((nb, 1))
    blk_f = blk.astype(f32)
    cur = qpos // SLC_LEN
    forced = (blk == 0) | (blk == cur) | (blk == cur - 1)
    imp = jnp.where(forced, jnp.inf, imp)
    imp = jnp.where((blk * SLC_LEN <= qpos) & (blk < n_slc), imp, -jnp.inf)

    def pick(_, st):
        imp, sel = st
        best = jnp.max(imp, axis=0, keepdims=True)
        first = jnp.min(jnp.where(imp == best, blk_f, float(nb)), axis=0, keepdims=True)
        hit = blk_f == first
        return jnp.where(hit, -jnp.inf, imp), jnp.where(hit & (best > -jnp.inf), 1.0, sel)

    _, sel = lax.fori_loop(0, topn, pick, (imp, jnp.zeros((nb, TQ), f32)))

    krow = _row_iota((TK, 1))
    qcol = _col_iota((1, TQ))
    causal = krow <= qcol

    selb = sel.astype(bf16)

    def slc_keep(kc, kind):
        keep = jnp.dot(exp_ref[kc], selb, preferred_element_type=f32) > 0.5
        return keep & causal if kind == 0 else keep

    def slc_finish(kind, h, s, keep):
        s = s * (HEAD_DIM ** -0.5 * LOG2E)
        if kind < 2:
            s = s + bias_ref[h, kind]
        return jnp.where(keep, s, NEG)

    _attend(qi, HG, lambda kc, h: jnp.dot(_k_chunk(ks_ref, kc), qs[h], preferred_element_type=f32),
            slc_finish, lambda kc, h: vsT_ref[kc], m_ref, l_ref, acc_ref, shared=slc_keep)
    o_slc = [_flash_result(h, l_ref, acc_ref) for h in range(HG)]

    def win_finish(kind, h, s, _):
        s = s * (HEAD_DIM ** -0.5 * LOG2E)
        if kind < 2:
            s = s + bias_ref[h, kind]
        if kind == 0:
            return jnp.where(causal, s, NEG)
        return jnp.where(krow > qcol, s, NEG) if kind == 2 else s

    _attend(qi, HG, lambda kc, h: jnp.dot(_k_chunk(kw_ref, kc), qs[h], preferred_element_type=f32),
            win_finish, lambda kc, h: vwT_ref[kc], m_ref, l_ref, acc_ref,
            first=jnp.maximum(qi - WIN // TK, 0), max_far=WIN // TK - 1)
    o_win = [_flash_result(h, l_ref, acc_ref) for h in range(HG)]

    gates = jax.nn.sigmoid(gT_ref[...])
    heads = []
    for h in range(HG):
        c = S_BG + 3 * h
        heads.append(gates[c:c + 1, :] * o_cmp[h] + gates[c + 1:c + 2, :] * o_slc[h]
                     + gates[c + 2:c + 3, :] * o_win[h])
    o_ref[...] = jnp.concatenate(heads, axis=0).T.astype(o_ref.dtype)


def _nsa(hr, ht, hs, kc, vcT, ovlT, expand, bias):
    B, L, _ = hr.shape
    nch = L // TK
    n = kc.shape[1]
    n_slc = L // SLC_LEN
    topn = min(SLC_TOPN, n_slc)
    return pl.pallas_call(
        functools.partial(_nsa_kernel, n_slc=n_slc, topn=topn),
        grid=(B, L // TQ),
        in_specs=[
            _q_spec(T_BQ), _q_spec(0, NS),
            pl.BlockSpec((None, n, LANES), lambda b, i: (b, 0, 0)),
            pl.BlockSpec((None, HEAD_DIM, n), lambda b, i: (b, 0, 0)),
            _k_spec(L, R_BKS, LANES), _vT_spec(nch, T_BVS, HEAD_DIM),
            _k_spec(L, R_BKW, LANES), _vT_spec(nch, T_BVW, HEAD_DIM),
            pl.BlockSpec(ovlT.shape, lambda b, i: (0, 0)),
            pl.BlockSpec(expand.shape, lambda b, i: (0, 0, 0)),
            _BIAS_SPEC,
        ],
        out_specs=_OUT_SPEC,
        out_shape=jax.ShapeDtypeStruct((B, L, GW), bf16),
        scratch_shapes=_flash_scratch(HG),
        compiler_params=_cparams("parallel", "arbitrary"),
        name="nsa",
    )(ht, hs, kc, vcT, hr, ht, hr, ht, ovlT, expand, bias)


def _nsa_tables(L):
    n = L // CMP_STRIDE
    n_slc = L // SLC_LEN
    nb = -(-n_slc // 16) * 16
    c0 = jnp.arange(n, dtype=i32)[None, :] * CMP_STRIDE
    s0 = jnp.arange(nb, dtype=i32)[:, None] * SLC_LEN
    ovlT = (c0 <= s0 + SLC_LEN - 1) & (c0 + CMP_LEN - 1 >= s0)
    ovlT = ovlT & (jnp.arange(n)[None, :] < n - 1) & (jnp.arange(nb)[:, None] < n_slc)
    tokblk = jnp.arange(L, dtype=i32) // SLC_LEN
    expand = tokblk[:, None] == jnp.arange(nb, dtype=i32)[None, :]
    return ovlT.astype(bf16), expand.reshape(L // TK, TK, nb).astype(bf16)


def _memkv_kernel(mem_ref, wkT_ref, wv_ref, kT_ref, v_ref):
    mb = mem_ref[...].astype(bf16)
    kT_ref[...] = lax.dot_general(wkT_ref[...], mb, (((1,), (1,)), ((), ())),
                                  preferred_element_type=f32).astype(bf16)
    v_ref[...] = jnp.dot(mb, wv_ref[...], preferred_element_type=f32).astype(bf16)


def _memkv(mem, wkT, wv):
    B, N, D = mem.shape
    wspec = pl.BlockSpec((D, D), lambda b: (0, 0))
    return pl.pallas_call(
        _memkv_kernel,
        grid=(B,),
        in_specs=[pl.BlockSpec((None, N, D), lambda b: (b, 0, 0)), wspec, wspec],
        out_specs=[pl.BlockSpec((None, D, N), lambda b: (b, 0, 0)),
                   pl.BlockSpec((None, N, D), lambda b: (b, 0, 0))],
        out_shape=[jax.ShapeDtypeStruct((B, D, N), bf16), jax.ShapeDtypeStruct((B, N, D), bf16)],
        compiler_params=_cparams("parallel"),
        name="cross_kv",
    )(mem, wkT, wv)


def _mix_cross_kernel(oa_ref, ob_ref, oc_ref, od_ref, w_ref, x_ref, g1_ref, b1_ref,
                      wq_ref, kT_ref, v_ref, wo_ref, g_ref, b_ref, y_ref):
    mix = jnp.zeros(x_ref.shape, f32)
    for n, o_ref in enumerate((oa_ref, ob_ref, oc_ref, od_ref)):
        mix = mix + jnp.dot(o_ref[...], w_ref[n * GW:(n + 1) * GW, :], preferred_element_type=f32)
    x = _layer_norm(ALPHA * x_ref[...] + mix, g1_ref[...], b1_ref[...])
    q = jnp.dot(x.astype(bf16), wq_ref[...], preferred_element_type=f32).astype(bf16)
    outs = []
    for h in range(CROSS_HEADS):
        sl = slice(h * CROSS_DIM, (h + 1) * CROSS_DIM)
        s = jnp.dot(q[:, sl], kT_ref[sl, :], preferred_element_type=f32) * CROSS_DIM ** -0.5
        e = jnp.exp(s - jnp.max(s, axis=1, keepdims=True))
        p = e / jnp.sum(e, axis=1, keepdims=True)
        outs.append(jnp.dot(p.astype(bf16), v_ref[:, sl], preferred_element_type=f32).astype(bf16))
    o = jnp.concatenate(outs, axis=1)
    y = ALPHA * x + jnp.dot(o, wo_ref[...], preferred_element_type=f32)
    y_ref[...] = _layer_norm(y, g_ref[...], b_ref[...])


def _mix_cross(oa, ob, oc, od, w, x, g1, b1, wq, kT, v, wo, g, b, tm=512):
    B, L, D = x.shape
    N = v.shape[1]
    ospec = pl.BlockSpec((None, tm, GW), lambda bb, i: (bb, i, 0))
    xspec = pl.BlockSpec((None, tm, D), lambda bb, i: (bb, i, 0))
    wspec = pl.BlockSpec((D, D), lambda bb, i: (0, 0))
    vec = pl.BlockSpec((1, D), lambda bb, i: (0, 0))
    return pl.pallas_call(
        _mix_cross_kernel,
        grid=(B, L // tm),
        in_specs=[ospec, ospec, ospec, ospec, wspec, xspec, vec, vec, wspec,
                  pl.BlockSpec((None, D, N), lambda bb, i: (bb, 0, 0)),
                  pl.BlockSpec((None, N, D), lambda bb, i: (bb, 0, 0)), wspec, vec, vec],
        out_specs=xspec,
        out_shape=jax.ShapeDtypeStruct((B, L, D), f32),
        compiler_params=_cparams("parallel", "parallel"),
        name="mix_cross_ln",
    )(oa, ob, oc, od, w, x, g1, b1, wq, kT, v, wo, g, b)


def _mlp_kernel(x_ref, w1_ref, w2_ref, g_ref, b_ref, y_ref, *, tf):
    x = x_ref[...]
    xb = x.astype(bf16)
    acc = jnp.zeros(x.shape, f32)
    for j in range(w1_ref.shape[1] // tf):
        hdn = jnp.dot(xb, w1_ref[:, j * tf:(j + 1) * tf], preferred_element_type=f32)
        hdn = jnp.square(jnp.maximum(hdn, 0.0)).astype(bf16)
        acc = acc + jnp.dot(hdn, w2_ref[j * tf:(j + 1) * tf, :], preferred_element_type=f32)
    y_ref[...] = _layer_norm(ALPHA * x + acc, g_ref[...], b_ref[...])


def _mlp(x, w1, w2, g, b, tm=512, tf=1024):
    M, D = x.shape
    F = w1.shape[1]
    vec = pl.BlockSpec((1, D), lambda i: (0, 0))
    return pl.pallas_call(
        functools.partial(_mlp_kernel, tf=tf),
        grid=(M // tm,),
        in_specs=[pl.BlockSpec((tm, D), lambda i: (i, 0)),
                  pl.BlockSpec((D, F), lambda i: (0, 0)),
                  pl.BlockSpec((F, D), lambda i: (0, 0)), vec, vec],
        out_specs=pl.BlockSpec((tm, D), lambda i: (i, 0)),
        out_shape=jax.ShapeDtypeStruct((M, D), f32),
        compiler_params=_cparams("parallel"),
        name="mlp_ln",
    )(x, w1, w2, g, b)


def _split_w_in(w):
    sizes = (GW, GW, GW, IDX_HEADS * IDX_DIM, IDX_DIM, IDX_HEADS,
             GW, HEAD_DIM, HEAD_DIM, HEAD_DIM, HEAD_DIM, HEAD_DIM, HEAD_DIM, 3 * HG,
             GW, GW, GW, GW, GW, GW)
    offs = [0]
    for s in sizes:
        offs.append(offs[-1] + s)
    return [w[:, offs[n]:offs[n + 1]] for n in range(len(sizes))]


def _layout_w_in(w):
    (a_q, a_k, a_v, a_qi, a_ki, a_w, b_q, b_kc, b_vc, b_ks, b_vs, b_kw, b_vw, b_g,
     c_q, c_k, c_v, d_q, d_k, d_v) = _split_w_in(w)
    twice = lambda t: jnp.concatenate([t, t], axis=1)
    wr = jnp.concatenate([a_k, c_k, d_k, twice(b_ks), twice(b_kw), twice(a_ki), b_kc, b_vc], axis=1)
    wt = jnp.concatenate([a_q, a_v, a_qi, b_q, c_q, c_v, d_q, d_v, b_vs, b_vw], axis=1).T
    ws = jnp.concatenate([a_w, b_g, jnp.zeros((w.shape[0], NS - IDX_HEADS - 3 * HG), w.dtype)], axis=1).T
    assert wr.shape[1] == NR and wt.shape[0] == NT
    return wr.astype(bf16), wt.astype(bf16), ws.astype(bf16)


def _mixers(x, bias, w_in_l, pos_k, pos_v, w1_k, w2_k, w1_v, w2_v, lq1, lk1, lq2, lk2, diff_g, lam_init):
    B, L, _ = x.shape
    wr, wt, ws = _layout_w_in(w_in_l)
    hr, ht, hs = _project(x, wr, wt, ws)
    o_a = _dsa(hr, ht, hs, bias[0:HG])

    n = L // CMP_STRIDE
    half = CMP_STRIDE * HEAD_DIM
    xk = hr[:, :, R_BKVC:R_BKVC + HEAD_DIM].reshape(B, n, half)
    xv = hr[:, :, R_BKVC + HEAD_DIM:R_BKVC + 2 * HEAD_DIM].reshape(B, n, half)
    kc, vcT = _compress(
        xk, xv, pos_k.reshape(2, half), pos_v.reshape(2, half),
        w1_k.reshape(2, half, HEAD_DIM).astype(bf16), w1_v.reshape(2, half, HEAD_DIM).astype(bf16),
        jnp.concatenate([w2_k, w2_k], axis=1).astype(bf16), w2_v.T.astype(bf16))
    ovlT, expand = _nsa_tables(L)
    o_b = _nsa(hr, ht, hs, kc, vcT, ovlT, expand, bias[HG:2 * HG])
    o_c = _moba(hr, ht, bias[2 * HG:3 * HG])
    o_d = _diff(hr, ht, bias[3 * HG:4 * HG], lq1.reshape(1, -1), lk1.reshape(1, -1),
                lq2.reshape(1, -1), lk2.reshape(1, -1), jnp.full((1, 1), lam_init, f32),
                diff_g.reshape(-1, 1))
    return o_a, o_b, o_c, o_d


def kernel(x, mem, rel_bias, w_in, w_out, nsa_pos_k, nsa_pos_v, nsa_w1_k, nsa_w2_k, nsa_w1_v, nsa_w2_v, diff_lq1, diff_lk1, diff_lq2, diff_lk2, diff_g, ln1_g, ln1_b, xq, xk, xv, xo, ln2_g, ln2_b, mlp_w1, mlp_w2, ln3_g, ln3_b):
    B, L, D = x.shape
    bias = _bias_tiles(rel_bias)
    for l in range(DEPTH):
        lam_init = 0.8 - 0.6 * math.exp(-0.3 * l)
        o_a, o_b, o_c, o_d = _mixers(
            x, bias, w_in[l], nsa_pos_k[l], nsa_pos_v[l], nsa_w1_k[l], nsa_w2_k[l],
            nsa_w1_v[l], nsa_w2_v[l], diff_lq1[l], diff_lk1[l], diff_lq2[l], diff_lk2[l],
            diff_g[l], lam_init)
        flat = lambda t: t.reshape(B * L, -1)
        row = lambda t: t.reshape(1, D)
        kT, v = _memkv(mem, xk[l].T.astype(bf16), xv[l].astype(bf16))
        x3 = _mix_cross(o_a, o_b, o_c, o_d, w_out[l].astype(bf16), x, row(ln1_g[l]), row(ln1_b[l]),
                        xq[l].astype(bf16), kT, v, xo[l].astype(bf16), row(ln2_g[l]), row(ln2_b[l]))
        x = _mlp(flat(x3), mlp_w1[l].astype(bf16), mlp_w2[l].astype(bf16),
                 row(ln3_g[l]), row(ln3_b[l])).reshape(B, L, D)
    return x
```
